```python
import jax, jax.numpy as jnp
from jax import lax
import numpy as np

D_MODEL = 2048
BATCH = 2
SEQ = 4096
DEPTH = 1
DEC_BATCH = 32
DEC_SEQ = 32
PAST_LEN = 4096

CHUNK = 64
Q_BLOCK = 128
H_A = 8
DK_A = 128
DV_A = 128
W_A = H_A * DK_A
WV_A = H_A * DV_A
H_B = 8
DH_B = 128
W_B = H_B * DH_B
D_FF = 4 * D_MODEL
N_BRANCH = 2
EPS = 1e-6
FOX_FORGET_BIAS = 2.0
IN_SIZES = (W_A, W_A, WV_A, WV_A, W_B, W_B, W_B, H_B, N_BRANCH * D_MODEL)
N_IN = W_A * 2 + WV_A * 2 + W_B * 3 + H_B + N_BRANCH * D_MODEL

kernel_name = 'hgrn2_fox_gated_parallel_streaming_step'


def _in_splits():
    return [int(v) for v in np.cumsum(IN_SIZES)[:-1]]


def rmsnorm(x, g):
    xf = x.astype(jnp.float32)
    y = xf * lax.rsqrt(jnp.mean(xf * xf, axis=-1, keepdims=True) + EPS)
    return (y * g.astype(jnp.float32)).astype(x.dtype)


def _gla_block(S, blk):
    q, k, v, lf = blk
    C = q.shape[1]
    b = jnp.cumsum(lf, axis=1)
    causal = jnp.tril(jnp.ones((C, C), dtype=bool))
    diff = b[:, :, None] - b[:, None, :]
    decay = jnp.exp(jnp.where(causal[None, :, :, None, None], diff, -jnp.inf))
    attn = jnp.einsum('bthd,bshd,btshd->bths', q, k, decay)
    o = (jnp.einsum('bthd,bhde->bthe', q * jnp.exp(b), S)
         + jnp.einsum('bths,bshe->bthe', attn, v))
    b_end = b[:, -1]
    S_new = (jnp.exp(b_end)[..., None] * S
             + jnp.einsum('bshd,bshe->bhde', k * jnp.exp(b_end[:, None] - b), v))
    return S_new, o


def hgrn2_mix(S0, q, k, v, lf):
    B, L = q.shape[0], q.shape[1]
    if L <= CHUNK:
        return _gla_block(S0, (q, k, v, lf))
    n = L // CHUNK
    def blocks(a):
        return jnp.moveaxis(a.reshape((B, n, CHUNK) + a.shape[2:]), 1, 0)
    S_new, o = lax.scan(_gla_block, S0, (blocks(q), blocks(k), blocks(v), blocks(lf)))
    o = jnp.moveaxis(o, 0, 1).reshape((B, L) + o.shape[3:])
    return S_new, o


def fox_attend(q, cq, qpos, k, v, ck, kpos):
    s = jnp.einsum('bthd,bshd->bhts', q, k).astype(jnp.float32) * (DH_B ** -0.5)
    s = s + jnp.moveaxis(cq, 2, 1)[..., None] - jnp.moveaxis(ck, 2, 1)[:, :, None, :]
    mask = kpos[None, :] <= qpos[:, None]
    p = jax.nn.softmax(jnp.where(mask[None, None], s, -jnp.inf), axis=-1)
    return jnp.einsum('bhts,bshd->bthd', p.astype(v.dtype), v)


def fox_prompt(q, k, v, lf):
    B, L = q.shape[0], q.shape[1]
    c = jnp.cumsum(lf, axis=1)
    pos = jnp.arange(L)
    nb = L // Q_BLOCK
    qb = jnp.moveaxis(q.reshape(B, nb, Q_BLOCK, H_B, DH_B), 1, 0)
    cb = jnp.moveaxis(c.reshape(B, nb, Q_BLOCK, H_B), 1, 0)
    pb = pos.reshape(nb, Q_BLOCK)
    o = lax.map(lambda a: fox_attend(a[0], a[1], a[2], k, v, c, pos), (qb, cb, pb))
    return jnp.moveaxis(o, 0, 1).reshape(B, L, H_B, DH_B)


def fox_sample(q, k, v, lf, ck, cv, clf):
    P = ck.shape[1]
    T = q.shape[1]
    k_all = jnp.concatenate([ck, k], axis=1)
    v_all = jnp.concatenate([cv, v], axis=1)
    c_all = jnp.cumsum(jnp.concatenate([clf.astype(jnp.float32), lf], axis=1), axis=1)
    kpos = jnp.arange(P + T)
    return fox_attend(q, c_all[:, P:], P + jnp.arange(T), k_all, v_all, c_all, kpos)


def _layer(x, S0, past, norm1, w_in, b_fox_f, lb, gnorm_a, w_pa, w_pb, w_o, norm2, w1, w2):
    B, L, _ = x.shape
    h = rmsnorm(x, norm1)
    proj = h @ w_in
    a_q, a_f, a_i, a_g, b_q, b_k, b_v, b_fl, gate = jnp.split(proj, _in_splits(), axis=-1)
    fa = lb + (1.0 - lb) * jax.nn.sigmoid(a_f.astype(jnp.float32))
    lf_a = jnp.log(fa).reshape(B, L, H_A, DK_A)
    qa = jax.nn.silu(a_q).reshape(B, L, H_A, DK_A)
    ka = (1.0 - fa).reshape(B, L, H_A, DK_A)
    va = a_i.reshape(B, L, H_A, DV_A)
    S_new, oa = hgrn2_mix(S0, qa, ka, va, lf_a)
    oa = rmsnorm(oa, gnorm_a) * jax.nn.silu(a_g.astype(jnp.float32)).reshape(B, L, H_A, DV_A)
    oa = oa.reshape(B, L, WV_A).astype(x.dtype)
    qb = b_q.reshape(B, L, H_B, DH_B)
    kb = b_k.reshape(B, L, H_B, DH_B)
    vb = b_v.reshape(B, L, H_B, DH_B)
    lf_b = jax.nn.log_sigmoid(b_fl.astype(jnp.float32) + b_fox_f.astype(jnp.float32))
    if past is None:
        ob = fox_prompt(qb, kb, vb, lf_b)
    else:
        ob = fox_sample(qb, kb, vb, lf_b, past[0], past[1], past[2])
    ob = ob.reshape(B, L, W_B).astype(x.dtype)
    g = jax.nn.sigmoid(gate.astype(jnp.float32)).astype(x.dtype)
    g_a, g_b = jnp.split(g, 2, axis=-1)
    merged = g_a * (oa @ w_pa) + g_b * (ob @ w_pb)
    x = x + merged @ w_o
    u = jnp.square(jax.nn.relu(rmsnorm(x, norm2) @ w1))
    x = x + u @ w2
    return x, S_new, kb, vb, lf_b


def setup_inputs(seed: int = 0) -> dict:
    key = jax.random.key(seed)
    ks = jax.random.split(key, 20)
    f32 = jnp.float32
    def nrm(k, shape, scale):
        return jax.random.normal(k, shape, f32) * scale
    return {
        'x_prompt': nrm(ks[0], (BATCH, SEQ, D_MODEL), 1.0),
        'x_sample': nrm(ks[1], (DEC_BATCH, DEC_SEQ, D_MODEL), 1.0),
        'cache_fox_k': nrm(ks[2], (DEPTH, DEC_BATCH, PAST_LEN, H_B, DH_B), 1.0),
        'cache_fox_v': nrm(ks[3], (DEPTH, DEC_BATCH, PAST_LEN, H_B, DH_B), 1.0),
        'cache_fox_logf': jax.nn.log_sigmoid(nrm(ks[4], (DEPTH, DEC_BATCH, PAST_LEN, H_B), 1.0) + FOX_FORGET_BIAS),
        'state_hgrn': nrm(ks[5], (DEPTH, DEC_BATCH, H_A, DK_A, DV_A), 0.5),
        'norm1': 1.0 + nrm(ks[6], (DEPTH, D_MODEL), 0.01),
        'w_in': nrm(ks[7], (DEPTH, D_MODEL, N_IN), D_MODEL ** -0.5),
        'b_fox_f': FOX_FORGET_BIAS + nrm(ks[8], (DEPTH, H_B), 0.1),
        'lb_logits': nrm(ks[9], (DEPTH + 1, W_A), 0.1),
        'gnorm_a': 1.0 + nrm(ks[10], (DEPTH, DV_A), 0.01),
        'w_pa': nrm(ks[11], (DEPTH, WV_A, D_MODEL), WV_A ** -0.5),
        'w_pb': nrm(ks[12], (DEPTH, W_B, D_MODEL), W_B ** -0.5),
        'w_o': nrm(ks[13], (DEPTH, D_MODEL, D_MODEL), D_MODEL ** -0.5),
        'norm2': 1.0 + nrm(ks[14], (DEPTH, D_MODEL), 0.01),
        'w1': nrm(ks[15], (DEPTH, D_MODEL, D_FF), D_MODEL ** -0.5),
        'w2': nrm(ks[16], (DEPTH, D_FF, D_MODEL), D_FF ** -0.5),
        'norm_f': 1.0 + nrm(ks[17], (D_MODEL,), 0.01),
    }


def reference(x_prompt, x_sample, cache_fox_k, cache_fox_v, cache_fox_logf, state_hgrn,
              norm1, w_in, b_fox_f, lb_logits, gnorm_a, w_pa, w_pb, w_o, norm2, w1, w2, norm_f):
    lower = jnp.cumsum(jax.nn.softmax(lb_logits.astype(jnp.float32), axis=0), axis=0)
    xp, xs = x_prompt, x_sample
    kp_l, vp_l, lfp_l, Sp_l, ks_l, vs_l, lfs_l, Ss_l = [], [], [], [], [], [], [], []
    for l in range(DEPTH):
        wts = (norm1[l], w_in[l], b_fox_f[l], lower[l], gnorm_a[l], w_pa[l], w_pb[l],
               w_o[l], norm2[l], w1[l], w2[l])
        S0 = jnp.zeros((xp.shape[0], H_A, DK_A, DV_A), jnp.float32)
        xp, Sp, kp, vp, lfp = _layer(xp, S0, None, *wts)
        xs, Ss, ks, vs, lfs = _layer(xs, state_hgrn[l],
                                     (cache_fox_k[l], cache_fox_v[l], cache_fox_logf[l]), *wts)
        kp_l.append(kp); vp_l.append(vp); lfp_l.append(lfp); Sp_l.append(Sp)
        ks_l.append(ks); vs_l.append(vs); lfs_l.append(lfs); Ss_l.append(Ss)
    y_prompt = rmsnorm(xp, norm_f)
    y_sample = rmsnorm(xs, norm_f)
    return (y_prompt, y_sample,
            jnp.stack(kp_l), jnp.stack(vp_l), jnp.stack(lfp_l), jnp.stack(Sp_l),
            jnp.stack(ks_l), jnp.stack(vs_l), jnp.stack(lfs_l), jnp.stack(Ss_l))
```

```python
import functools

import jax
import jax.numpy as jnp
from jax import lax
from jax.experimental import pallas as pl
from jax.experimental.pallas import tpu as pltpu

F32 = jnp.float32
BF16 = jnp.bfloat16

EPS = 1e-6
N_HEADS = 8
HEAD_DIM = 128
WIDTH = N_HEADS * HEAD_DIM
LANES = 128
VMEM_LIMIT = 56 * 1024 * 1024

NT_DIMS = (((1,), (1,)), ((), ()))
TN_DIMS = (((0,), (0,)), ((), ()))


def _params(*sem):
    return pltpu.CompilerParams(dimension_semantics=sem, vmem_limit_bytes=VMEM_LIMIT)


def _sigmoid(x):
    return 1.0 / (1.0 + jnp.exp(-x))


def _normed(x_ref, g_ref):
    x = x_ref[...]
    ms = jnp.mean(x * x, axis=-1, keepdims=True)
    return (x * lax.rsqrt(ms + EPS) * g_ref[...]).astype(BF16)


def _proj_kernel(x_ref, g_ref, w_ref, o_ref, h_ref, *, act):
    @pl.when(pl.program_id(1) == 0)
    def _():
        h_ref[...] = _normed(x_ref, g_ref)

    acc = jnp.dot(h_ref[...], w_ref[...], preferred_element_type=F32)
    if act == "sigmoid":
        acc = _sigmoid(acc)
    o_ref[...] = acc.astype(o_ref.dtype)


def _norm_proj(x, g, w, *, act, out_dtype, tm, tn):
    T, D = x.shape
    N = w.shape[1]
    return pl.pallas_call(
        functools.partial(_proj_kernel, act=act),
        grid=(T // tm, N // tn),
        in_specs=[
            pl.BlockSpec((tm, D), lambda i, j: (i, 0)),
            pl.BlockSpec((1, D), lambda i, j: (0, 0)),
            pl.BlockSpec((D, tn), lambda i, j: (0, j)),
        ],
        out_specs=pl.BlockSpec((tm, tn), lambda i, j: (i, j)),
        out_shape=jax.ShapeDtypeStruct((T, N), out_dtype),
        scratch_shapes=[pltpu.VMEM((tm, D), BF16)],
        compiler_params=_params("parallel", "arbitrary"),
        name="norm_proj_" + (act or "lin"),
    )(x, g, w)


def _proj_fox_kernel(x_ref, g_ref, w_ref, wfl_ref, q_ref, k_ref, v_ref, kb_ref, vb_ref, fl_ref, h_ref):
    j = pl.program_id(1)

    @pl.when(j == 0)
    def _():
        h = _normed(x_ref, g_ref)
        h_ref[...] = h
        fl_ref[...] = lax.dot_general(wfl_ref[...], h, NT_DIMS, preferred_element_type=F32)

    acc = jnp.dot(h_ref[...], w_ref[...], preferred_element_type=F32)

    @pl.when(j == 0)
    def _():
        q_ref[...] = acc.astype(BF16)

    @pl.when(j == 1)
    def _():
        k_ref[...] = acc
        kb_ref[...] = acc.astype(BF16)

    @pl.when(j == 2)
    def _():
        v_ref[...] = acc
        vb_ref[...] = acc.astype(BF16)


def _norm_proj_fox(x, g, w_qkv, w_fl_t, *, tm):
    T, D = x.shape
    row = lambda i, j: (i, 0)
    tok = pl.BlockSpec((tm, WIDTH), row)
    return pl.pallas_call(
        _proj_fox_kernel,
        grid=(T // tm, 3),
        in_specs=[
            pl.BlockSpec((tm, D), row),
            pl.BlockSpec((1, D), lambda i, j: (0, 0)),
            pl.BlockSpec((D, WIDTH), lambda i, j: (0, j)),
            pl.BlockSpec((N_HEADS, D), lambda i, j: (0, 0)),
        ],
        out_specs=[tok, tok, tok, tok, tok, pl.BlockSpec((N_HEADS, tm), lambda i, j: (0, i))],
        out_shape=[
            jax.ShapeDtypeStruct((T, WIDTH), BF16),
            jax.ShapeDtypeStruct((T, WIDTH), F32),
            jax.ShapeDtypeStruct((T, WIDTH), F32),
            jax.ShapeDtypeStruct((T, WIDTH), BF16),
            jax.ShapeDtypeStruct((T, WIDTH), BF16),
            jax.ShapeDtypeStruct((N_HEADS, T), F32),
        ],
        scratch_shapes=[pltpu.VMEM((tm, D), BF16)],
        compiler_params=_params("parallel", "arbitrary"),
        name="norm_proj_fox",
    )(x, g, w_qkv, w_fl_t)


def _lane_cumsum(x, seg):
    n = x.shape[-1]
    pos = lax.broadcasted_iota(jnp.int32, x.shape, x.ndim - 1) & (seg - 1)
    shift = 1
    while shift < seg:
        x = x + jnp.where(pos >= shift, pltpu.roll(x, shift, x.ndim - 1), 0.0)
        shift *= 2
    del n
    return x


def _gate_kernel(fl_ref, bias_ref, lf_ref, c_ref, *, seg):
    z = fl_ref[...] + bias_ref[...]
    lf = jnp.minimum(z, 0.0) - jnp.log(1.0 + jnp.exp(-jnp.abs(z)))
    lf_ref[...] = lf
    c_ref[...] = _lane_cumsum(lf, seg)


def _fox_gates(fl_t, bias, *, seg, block):
    H, T = fl_t.shape
    spec = pl.BlockSpec((H, block), lambda i: (0, i))
    return pl.pallas_call(
        functools.partial(_gate_kernel, seg=seg),
        grid=(T // block,),
        in_specs=[spec, pl.BlockSpec((H, 1), lambda i: (0, 0))],
        out_specs=[spec, spec],
        out_shape=[jax.ShapeDtypeStruct((H, T), F32)] * 2,
        compiler_params=_params("parallel"),
        name="fox_gates",
    )(fl_t, bias)


def _suffix_kernel(lf_ref, o_ref):
    lf = lf_ref[...]
    c = _lane_cumsum(lf, lf.shape[-1])
    o_ref[...] = c[:, -1:] - c


def _cache_suffix(clf_t):
    B, H, P = clf_t.shape
    spec = pl.BlockSpec((None, H, P), lambda b: (b, 0, 0))
    return pl.pallas_call(
        _suffix_kernel,
        grid=(B,),
        in_specs=[spec],
        out_specs=spec,
        out_shape=jax.ShapeDtypeStruct((B, H, P), F32),
        compiler_params=_params("parallel"),
        name="fox_cache_suffix",
    )(clf_t)


SUB = 16


def _gla_kernel(aq_ref, af_ref, ai_ref, ag_ref, lbl_ref, gn_ref, s0_ref, tri_ref,
                o_ref, s_out_ref, st_ref, oacc_ref, *, chunk, n_chunks):
    t = pl.program_id(1)

    @pl.when(t == 0)
    def _():
        for h in range(N_HEADS):
            st_ref[h] = s0_ref[h].T

    lbl = lbl_ref[...]
    e = jnp.exp(lbl - jnp.max(lbl, axis=0, keepdims=True))
    lb = e[0:1] / jnp.sum(e, axis=0, keepdims=True)
    gn = gn_ref[...]
    tri = tri_ref[...]

    def one_chunk(ci, carry):
        r = pl.ds(pl.multiple_of(ci * chunk, chunk), chunk)
        fa = lb + (1.0 - lb) * _sigmoid(af_ref[r, :])
        lf = jnp.log(fa)
        ka = 1.0 - fa
        aq = aq_ref[r, :]
        qa = aq * _sigmoid(aq)
        v = ai_ref[r, :].astype(BF16)

        hi = lf.astype(BF16)
        r1 = lf - hi.astype(F32)
        mid = r1.astype(BF16)
        lo = (r1 - mid.astype(F32)).astype(BF16)
        b = (jnp.dot(tri, hi, preferred_element_type=F32)
             + jnp.dot(tri, mid, preferred_element_type=F32)
             + jnp.dot(tri, lo, preferred_element_type=F32))
        b_end = b[chunk - 1:chunk, :]

        q_in = (qa * jnp.exp(b)).astype(BF16)
        k_out = (ka * jnp.exp(b_end - b)).astype(BF16)
        for h in range(N_HEADS):
            hs = slice(h * HEAD_DIM, (h + 1) * HEAD_DIM)
            st = st_ref[h]
            oacc_ref[:, hs] = lax.dot_general(q_in[:, hs], st.astype(BF16), NT_DIMS,
                                              preferred_element_type=F32)
            st_ref[h] = st * jnp.exp(b_end[:, hs]) + lax.dot_general(
                v[:, hs], k_out[:, hs], TN_DIMS, preferred_element_type=F32)

        for i in range(chunk // SUB):
            r0, n = i * SUB, (i + 1) * SUB
            m_i = b[r0 + SUB // 2:r0 + SUB // 2 + 1, :]
            qp = (qa[r0:n] * jnp.exp(b[r0:n] - m_i)).astype(BF16)
            kp = (ka[:n] * jnp.exp(m_i - b[:n])).astype(BF16)
            causal = (lax.broadcasted_iota(jnp.int32, (SUB, n), 1)
                      <= lax.broadcasted_iota(jnp.int32, (SUB, n), 0) + r0)
            for h in range(N_HEADS):
                hs = slice(h * HEAD_DIM, (h + 1) * HEAD_DIM)
                a = lax.dot_general(qp[:, hs], kp[:, hs], NT_DIMS, preferred_element_type=F32)
                a = jnp.where(causal, a, 0.0).astype(BF16)
                oacc_ref[r0:n, hs] += jnp.dot(a, v[:n, hs], preferred_element_type=F32)

        ag = ag_ref[r, :]
        gate = ag * _sigmoid(ag)
        for h in range(N_HEADS):
            hs = slice(h * HEAD_DIM, (h + 1) * HEAD_DIM)
            o = oacc_ref[:, hs]
            ms = jnp.mean(o * o, axis=-1, keepdims=True)
            o_ref[r, hs] = (o * lax.rsqrt(ms + EPS) * gn * gate[:, hs]).astype(o_ref.dtype)
        return carry

    lax.fori_loop(0, n_chunks, one_chunk, 0)

    @pl.when(t == pl.num_programs(1) - 1)
    def _():
        for h in range(N_HEADS):
            s_out_ref[h] = st_ref[h].T


def _hgrn2(a_proj, lb_logits, gnorm, s0, *, batch, length, row0, chunk, n_chunks):
    step = chunk * n_chunks
    nt = length // step
    base = row0 // step

    def col(g):
        return pl.BlockSpec((step, WIDTH), lambda b, t: (base + b * nt + t, g))

    st_spec = pl.BlockSpec((None, N_HEADS, HEAD_DIM, HEAD_DIM), lambda b, t: (b, 0, 0, 0))
    tri = jnp.tril(jnp.ones((chunk, chunk), BF16))
    return pl.pallas_call(
        functools.partial(_gla_kernel, chunk=chunk, n_chunks=n_chunks),
        grid=(batch, nt),
        in_specs=[col(0), col(1), col(2), col(3),
                  pl.BlockSpec(lb_logits.shape, lambda b, t: (0, 0)),
                  pl.BlockSpec((1, HEAD_DIM), lambda b, t: (0, 0)),
                  st_spec,
                  pl.BlockSpec((chunk, chunk), lambda b, t: (0, 0))],
        out_specs=[pl.BlockSpec((step, WIDTH), lambda b, t: (b * nt + t, 0)), st_spec],
        out_shape=[jax.ShapeDtypeStruct((batch * length, WIDTH), BF16),
                   jax.ShapeDtypeStruct((batch, N_HEADS, HEAD_DIM, HEAD_DIM), F32)],
        scratch_shapes=[pltpu.VMEM((N_HEADS, HEAD_DIM, HEAD_DIM), F32),
                        pltpu.VMEM((chunk, WIDTH), F32)],
        compiler_params=_params("parallel", "arbitrary"),
        name="hgrn2",
    )(a_proj, a_proj, a_proj, a_proj, lb_logits, gnorm, s0, tri)


def _softmax_step(s, v, m_ref, l_ref, acc_ref):
    m_prev = m_ref[...]
    m_new = jnp.maximum(m_prev, jnp.max(s, axis=-1, keepdims=True))
    alpha = jnp.exp(m_prev - m_new)
    p = jnp.exp(s - m_new)
    l_ref[...] = alpha * l_ref[...] + jnp.sum(p, axis=-1, keepdims=True)
    acc_ref[...] = alpha * acc_ref[...] + jnp.dot(p.astype(BF16), v, preferred_element_type=F32)
    m_ref[...] = m_new


def _attn_prompt_kernel(q_ref, k_ref, v_ref, c_ref, o_ref, m_ref, l_ref, acc_ref, *, blk):
    iq = pl.program_id(2)
    scale = HEAD_DIM ** -0.5
    q = q_ref[...]
    m_ref[...] = jnp.full(m_ref.shape, -jnp.inf, F32)
    l_ref[...] = jnp.zeros(l_ref.shape, F32)
    acc_ref[...] = jnp.zeros(acc_ref.shape, F32)
    q0 = pl.multiple_of(iq * blk, blk)
    c0 = c_ref[:, pl.ds(q0, LANES)][:, :1]

    def scores(j):
        ks = pl.ds(pl.multiple_of(j * blk, blk), blk)
        s = lax.dot_general(q, k_ref[ks, :], NT_DIMS, preferred_element_type=F32)
        return s * scale + (c0 - c_ref[:, ks]), v_ref[ks, :]

    def body(j, carry):
        s, v = scores(j)
        _softmax_step(s, v, m_ref, l_ref, acc_ref)
        return carry

    lax.fori_loop(0, iq, body, 0)
    s, v = scores(iq)
    rows = lax.broadcasted_iota(jnp.int32, s.shape, 0)
    cols = lax.broadcasted_iota(jnp.int32, s.shape, 1)
    _softmax_step(jnp.where(cols <= rows, s, -jnp.inf), v, m_ref, l_ref, acc_ref)
    o_ref[...] = (acc_ref[...] / l_ref[...]).astype(o_ref.dtype)


def _fox_prompt(q, k, v, c, *, batch, length, blk):
    nq = length // blk
    qspec = pl.BlockSpec((blk, HEAD_DIM), lambda b, h, i: (b * nq + i, h))
    kvspec = pl.BlockSpec((length, HEAD_DIM), lambda b, h, i: (b, h))
    return pl.pallas_call(
        functools.partial(_attn_prompt_kernel, blk=blk),
        grid=(batch, N_HEADS, nq),
        in_specs=[qspec, kvspec, kvspec,
                  pl.BlockSpec((None, None, 1, length), lambda b, h, i: (b, h, 0, 0))],
        out_specs=qspec,
        out_shape=jax.ShapeDtypeStruct((batch * length, WIDTH), BF16),
        scratch_shapes=[pltpu.VMEM((blk, 1), F32), pltpu.VMEM((blk, 1), F32),
                        pltpu.VMEM((blk, HEAD_DIM), F32)],
        compiler_params=_params("parallel", "parallel", "arbitrary"),
        name="fox_prompt",
    )(q, k, v, c)


def _attn_sample_kernel(q_ref, kc_ref, vc_ref, bc_ref, kn_ref, vn_ref, bn_ref, o_ref,
                        m_ref, l_ref, acc_ref):
    j = pl.program_id(1)
    scale = HEAD_DIM ** -0.5

    @pl.when(j == 0)
    def _():
        m_ref[...] = jnp.full(m_ref.shape, -jnp.inf, F32)
        l_ref[...] = jnp.zeros(l_ref.shape, F32)
        acc_ref[...] = jnp.zeros(acc_ref.shape, F32)

    for h in range(N_HEADS):
        hs = slice(h * HEAD_DIM, (h + 1) * HEAD_DIM)
        s = lax.dot_general(q_ref[:, hs], kc_ref[:, hs].astype(BF16), NT_DIMS,
                            preferred_element_type=F32)
        s = s * scale + bc_ref[h:h + 1, :]
        _softmax_step(s, vc_ref[:, hs].astype(BF16), m_ref.at[h], l_ref.at[h], acc_ref.at[:, hs])

    @pl.when(j == pl.num_programs(1) - 1)
    def _():
        tq = q_ref.shape[0]
        rows = lax.broadcasted_iota(jnp.int32, (tq, tq), 0)
        cols = lax.broadcasted_iota(jnp.int32, (tq, tq), 1)
        for h in range(N_HEADS):
            hs = slice(h * HEAD_DIM, (h + 1) * HEAD_DIM)
            s = lax.dot_general(q_ref[:, hs], kn_ref[:, hs], NT_DIMS, preferred_element_type=F32)
            s = s * scale - bn_ref[h:h + 1, :]
            s = jnp.where(cols <= rows, s, -jnp.inf)
            _softmax_step(s, vn_ref[:, hs], m_ref.at[h], l_ref.at[h], acc_ref.at[:, hs])
            o_ref[:, hs] = (acc_ref[:, hs] / l_ref[h]).astype(o_ref.dtype)


def _fox_sample(q, k_new, v_new, c_new, cache_k, cache_v, cache_bias, *, tk):
    B, T, _ = q.shape
    P = cache_k.shape[1]
    new = pl.BlockSpec((None, T, WIDTH), lambda b, j: (b, 0, 0))
    cache = pl.BlockSpec((None, tk, WIDTH), lambda b, j: (b, j, 0))
    return pl.pallas_call(
        _attn_sample_kernel,
        grid=(B, P // tk),
        in_specs=[new, cache, cache,
                  pl.BlockSpec((None, N_HEADS, tk), lambda b, j: (b, 0, j)),
                  new, new,
                  pl.BlockSpec((None, N_HEADS, T), lambda b, j: (b, 0, 0))],
        out_specs=new,
        out_shape=jax.ShapeDtypeStruct((B, T, WIDTH), BF16),
        scratch_shapes=[pltpu.VMEM((N_HEADS, T, 1), F32), pltpu.VMEM((N_HEADS, T, 1), F32),
                        pltpu.VMEM((T, WIDTH), F32)],
        compiler_params=_params("parallel", "arbitrary"),
        name="fox_sample",
    )(q, cache_k, cache_v, cache_bias, k_new, v_new, c_new)


MERGE_COLS = 512


def _merge_kernel(x_ref, oa_ref, ob_ref, g_ref, wpa_ref, wpb_ref, wo_ref, o_ref, mg_ref):
    D = x_ref.shape[1]
    oa = oa_ref[...]
    ob = ob_ref[...]
    cols = min(MERGE_COLS, D)
    for n0 in range(0, D, cols):
        ns = slice(n0, n0 + cols)
        pa = jnp.dot(oa, wpa_ref[:, ns], preferred_element_type=F32)
        pb = jnp.dot(ob, wpb_ref[:, ns], preferred_element_type=F32)
        ga = g_ref[:, ns].astype(F32)
        gb = g_ref[:, D + n0:D + n0 + cols].astype(F32)
        mg_ref[:, ns] = (ga * pa + gb * pb).astype(BF16)
    o_ref[...] = x_ref[...] + jnp.dot(mg_ref[...], wo_ref[...], preferred_element_type=F32)


def _merge_out(x, oa, ob, g, w_pa, w_pb, w_o, *, tm):
    T, D = x.shape
    row = lambda i: (i, 0)
    fixed = lambda i: (0, 0)
    return pl.pallas_call(
        _merge_kernel,
        grid=(T // tm,),
        in_specs=[pl.BlockSpec((tm, D), row), pl.BlockSpec((tm, WIDTH), row),
                  pl.BlockSpec((tm, WIDTH), row), pl.BlockSpec((tm, 2 * D), row),
                  pl.BlockSpec(w_pa.shape, fixed), pl.BlockSpec(w_pb.shape, fixed),
                  pl.BlockSpec(w_o.shape, fixed)],
        out_specs=pl.BlockSpec((tm, D), row),
        out_shape=jax.ShapeDtypeStruct((T, D), F32),
        scratch_shapes=[pltpu.VMEM((tm, D), BF16)],
        compiler_params=_params("parallel"),
        name="merge_out",
    )(x, oa, ob, g, w_pa, w_pb, w_o)


def _ffn_kernel(x_ref, g2_ref, w1_ref, w2_ref, gf_ref, y_ref, h_ref, acc_ref):
    f = pl.program_id(1)

    @pl.when(f == 0)
    def _():
        h_ref[...] = _normed(x_ref, g2_ref)
        acc_ref[...] = x_ref[...]

    u = jnp.maximum(jnp.dot(h_ref[...], w1_ref[...], preferred_element_type=F32), 0.0)
    acc_ref[...] += jnp.dot((u * u).astype(BF16), w2_ref[...], preferred_element_type=F32)

    @pl.when(f == pl.num_programs(1) - 1)
    def _():
        x = acc_ref[...]
        ms = jnp.mean(x * x, axis=-1, keepdims=True)
        y_ref[...] = x * lax.rsqrt(ms + EPS) * gf_ref[...]


def _ffn(x, g2, w1, w2, gf, *, tm, tf):
    T, D = x.shape
    F = w1.shape[1]
    row = lambda i, f: (i, 0)
    vec = pl.BlockSpec((1, D), lambda i, f: (0, 0))
    return pl.pallas_call(
        _ffn_kernel,
        grid=(T // tm, F // tf),
        in_specs=[pl.BlockSpec((tm, D), row), vec,
                  pl.BlockSpec((D, tf), lambda i, f: (0, f)),
                  pl.BlockSpec((tf, D), lambda i, f: (f, 0)), vec],
        out_specs=pl.BlockSpec((tm, D), row),
        out_shape=jax.ShapeDtypeStruct((T, D), F32),
        scratch_shapes=[pltpu.VMEM((tm, D), BF16), pltpu.VMEM((tm, D), F32)],
        compiler_params=_params("parallel", "arbitrary"),
        name="ffn",
    )(x, g2, w1, w2, gf)


def _tile(n, pref):
    t = min(pref, n)
    while n % t:
        t //= 2
    return t


def kernel(x_prompt, x_sample, cache_fox_k, cache_fox_v, cache_fox_logf, state_hgrn, norm1, w_in,
           b_fox_f, lb_logits, gnorm_a, w_pa, w_pb, w_o, norm2, w1, w2, norm_f):
    B, L, D = x_prompt.shape
    Bs, Ls, _ = x_sample.shape
    depth, _, P, H, DH = cache_fox_k.shape
    assert depth == 1 and H == N_HEADS and DH == HEAD_DIM
    Tp, Ts = B * L, Bs * Ls
    T = Tp + Ts
    W = WIDTH

    x = jnp.concatenate([x_prompt.reshape(Tp, D), x_sample.reshape(Ts, D)], axis=0)
    w_in0 = w_in[0]
    w_a = w_in0[:, :4 * W].astype(BF16)
    w_b = w_in0[:, 4 * W:7 * W].astype(BF16)
    w_fl_t = w_in0[:, 7 * W:7 * W + H].T.astype(BF16)
    w_g = w_in0[:, 7 * W + H:].astype(BF16)
    g1 = norm1[0].reshape(1, D)

    tm = _tile(T, 1024)
    a_proj = _norm_proj(x, g1, w_a, act=None, out_dtype=F32, tm=tm, tn=W)
    gates = _norm_proj(x, g1, w_g, act="sigmoid", out_dtype=BF16, tm=tm, tn=_tile(2 * D, W))
    q_b, k_f, v_f, k_b, v_b, fl_t = _norm_proj_fox(x, g1, w_b, w_fl_t, tm=_tile(T, 512))

    bias = b_fox_f[0].reshape(H, 1)
    lf_p, c_p = _fox_gates(fl_t[:, :Tp], bias, seg=L, block=L)
    lf_s, c_s = _fox_gates(fl_t[:, Tp:], bias, seg=Ls, block=_tile(Ts, 1024))
    cache_bias = _cache_suffix(jnp.transpose(cache_fox_logf[0], (0, 2, 1)))

    zeros_state = jnp.zeros((B, H, DH, DH), F32)
    chunk_p = _tile(L, 64)
    oa_p, s_p = _hgrn2(a_proj, lb_logits, gnorm_a, zeros_state, batch=B, length=L, row0=0,
                       chunk=chunk_p, n_chunks=_tile(L // chunk_p, 4))
    oa_s, s_s = _hgrn2(a_proj, lb_logits, gnorm_a, state_hgrn[0], batch=Bs, length=Ls, row0=Tp,
                       chunk=_tile(Ls, 64), n_chunks=1)

    c_p4 = c_p.reshape(H, B, 1, L).transpose(1, 0, 2, 3)
    ob_p = _fox_prompt(q_b, k_b, v_b, c_p4, batch=B, length=L, blk=_tile(L, 512))
    c_s3 = c_s.reshape(H, Bs, Ls).transpose(1, 0, 2)
    ob_s = _fox_sample(q_b[Tp:].reshape(Bs, Ls, W), k_b[Tp:].reshape(Bs, Ls, W),
                       v_b[Tp:].reshape(Bs, Ls, W), c_s3,
                       cache_fox_k[0].reshape(Bs, P, W), cache_fox_v[0].reshape(Bs, P, W),
                       cache_bias, tk=_tile(P, 1024))

    oa = jnp.concatenate([oa_p, oa_s], axis=0)
    ob = jnp.concatenate([ob_p, ob_s.reshape(Ts, W)], axis=0)
    x1 = _merge_out(x, oa, ob, gates, w_pa[0].astype(BF16), w_pb[0].astype(BF16),
                    w_o[0].astype(BF16), tm=_tile(T, 512))
    y = _ffn(x1, norm2[0].reshape(1, D), w1[0].astype(BF16), w2[0].astype(BF16),
             norm_f.reshape(1, D), tm=_tile(T, 512), tf=_tile(w1.shape[-1], 1024))

    def heads(a, b, l):
        return a.reshape(1, b, l, H, DH)

    return (y[:Tp].reshape(B, L, D), y[Tp:].reshape(Bs, Ls, D),
            heads(k_f[:Tp], B, L), heads(v_f[:Tp], B, L), lf_p.T.reshape(1, B, L, H), s_p[None],
            heads(k_f[Tp:], Bs, Ls), heads(v_f[Tp:], Bs, Ls), lf_s.T.reshape(1, Bs, Ls, H),
            s_s[None])
```

```python
import functools
import math

import jax
import jax.numpy as jnp
from jax import lax
from jax.experimental import pallas as pl
from jax.experimental.pallas import tpu as pltpu

F32 = jnp.float32
BF16 = jnp.bfloat16

EPS = 1e-6
N_HEADS = 8
HEAD_DIM = 128
WIDTH = N_HEADS * HEAD_DIM
LANES = 128
VMEM_LIMIT = 56 * 1024 * 1024
LOG2E = math.log2(math.e)
QK_SCALE = HEAD_DIM ** -0.5 * LOG2E

NT_DIMS = (((1,), (1,)), ((), ()))
TN_DIMS = (((0,), (0,)), ((), ()))


def _params(*sem):
    return pltpu.CompilerParams(dimension_semantics=sem, vmem_limit_bytes=VMEM_LIMIT)


def _sigmoid(x):
    return 1.0 / (1.0 + jnp.exp(-x))


def _head(h):
    return slice(h * HEAD_DIM, (h + 1) * HEAD_DIM)


def _two_streams(rows, width, n_first):
    return (pl.BlockSpec((rows, width), lambda i, *_: (jnp.minimum(i, n_first - 1), 0)),
            pl.BlockSpec((rows, width), lambda i, *_: (jnp.maximum(i - n_first, 0), 0)))


def _head_rows(h, rows):
    return pl.ds(h, rows, stride=N_HEADS)


def _normed(x, g_ref):
    ms = jnp.mean(x * x, axis=-1, keepdims=True)
    return (x * lax.rsqrt(ms + EPS) * g_ref[...]).astype(BF16)


def _proj_kernel(xp_ref, xs_ref, g_ref, w_ref, o_ref, h_ref, *, act, n_first):
    @pl.when(pl.program_id(1) == 0)
    def _():
        x = jnp.where(pl.program_id(0) < n_first, xp_ref[...], xs_ref[...])
        h_ref[...] = _normed(x, g_ref)

    acc = jnp.dot(h_ref[...], w_ref[...], preferred_element_type=F32)
    if act == "sigmoid":
        acc = _sigmoid(acc)
    o_ref[...] = acc.astype(o_ref.dtype)


def _norm_proj(xp, xs, g, w, *, act, out_dtype, tm, tn):
    D = xp.shape[1]
    N = w.shape[1]
    n_first = xp.shape[0] // tm
    T = xp.shape[0] + xs.shape[0]
    return pl.pallas_call(
        functools.partial(_proj_kernel, act=act, n_first=n_first),
        grid=(T // tm, N // tn),
        in_specs=[*_two_streams(tm, D, n_first),
                  pl.BlockSpec((1, D), lambda i, j: (0, 0)),
                  pl.BlockSpec((D, tn), lambda i, j: (0, j))],
        out_specs=pl.BlockSpec((tm, tn), lambda i, j: (i, j)),
        out_shape=jax.ShapeDtypeStruct((T, N), out_dtype),
        scratch_shapes=[pltpu.VMEM((tm, D), BF16)],
        compiler_params=_params("parallel", "arbitrary"),
        name="norm_proj_" + (act or "lin"),
    )(xp, xs, g, w)


def _proj_fox_kernel(xp_ref, xs_ref, g_ref, w_ref, wfl_ref,
                     q_ref, kb_ref, vb_ref, kp_ref, ks_ref, vp_ref, vs_ref, fl_ref, h_ref,
                     *, n_first):
    i, j = pl.program_id(0), pl.program_id(1)

    @pl.when(j == 0)
    def _():
        x = jnp.where(i < n_first, xp_ref[...], xs_ref[...])
        h = _normed(x, g_ref)
        h_ref[...] = h
        fl_ref[...] = lax.dot_general(wfl_ref[...], h, NT_DIMS, preferred_element_type=F32)

    acc = jnp.dot(h_ref[...], w_ref[...], preferred_element_type=F32)

    @pl.when(j == 0)
    def _():
        q_ref[...] = (acc * QK_SCALE).astype(BF16)

    def heads_out(first_ref, second_ref):
        @pl.when(i < n_first)
        def _():
            for h in range(N_HEADS):
                first_ref[_head_rows(h, acc.shape[0]), :] = acc[:, _head(h)]

        @pl.when(i >= n_first)
        def _():
            for h in range(N_HEADS):
                second_ref[_head_rows(h, acc.shape[0]), :] = acc[:, _head(h)]

    @pl.when(j == 1)
    def _():
        kb_ref[...] = acc.astype(BF16)
        heads_out(kp_ref, ks_ref)

    @pl.when(j == 2)
    def _():
        vb_ref[...] = acc.astype(BF16)
        heads_out(vp_ref, vs_ref)


def _norm_proj_fox(xp, xs, g, w_qkv, w_fl_t, *, tm):
    D = xp.shape[1]
    Tp, Ts = xp.shape[0], xs.shape[0]
    T = Tp + Ts
    n_first = Tp // tm
    kv_p, kv_s = _two_streams(tm * N_HEADS, HEAD_DIM, n_first)
    kv_shape = lambda rows: jax.ShapeDtypeStruct((rows * N_HEADS, HEAD_DIM), F32)
    tok = pl.BlockSpec((tm, WIDTH), lambda i, j: (i, 0))
    tok_shape = jax.ShapeDtypeStruct((T, WIDTH), BF16)
    return pl.pallas_call(
        functools.partial(_proj_fox_kernel, n_first=n_first),
        grid=(T // tm, 3),
        in_specs=[*_two_streams(tm, D, n_first),
                  pl.BlockSpec((1, D), lambda i, j: (0, 0)),
                  pl.BlockSpec((D, WIDTH), lambda i, j: (0, j)),
                  pl.BlockSpec((N_HEADS, D), lambda i, j: (0, 0))],
        out_specs=[tok, tok, tok, kv_p, kv_s, kv_p, kv_s,
                   pl.BlockSpec((N_HEADS, tm), lambda i, j: (0, i))],
        out_shape=[tok_shape, tok_shape, tok_shape,
                   kv_shape(Tp), kv_shape(Ts), kv_shape(Tp), kv_shape(Ts),
                   jax.ShapeDtypeStruct((N_HEADS, T), F32)],
        scratch_shapes=[pltpu.VMEM((tm, D), BF16)],
        compiler_params=_params("arbitrary", "arbitrary"),
        name="norm_proj_fox",
    )(xp, xs, g, w_qkv, w_fl_t)


def _lane_cumsum(x, seg):
    pos = lax.broadcasted_iota(jnp.int32, x.shape, x.ndim - 1) & (seg - 1)
    shift = 1
    while shift < seg:
        x = x + jnp.where(pos >= shift, pltpu.roll(x, shift, x.ndim - 1), 0.0)
        shift *= 2
    return x


def _gate_kernel(fl_ref, bias_ref, lf_ref, c_ref, *, seg):
    z = fl_ref[...] + bias_ref[...]
    lf = jnp.minimum(z, 0.0) - jnp.log(1.0 + jnp.exp(-jnp.abs(z)))
    lf_ref[...] = lf
    c = _lane_cumsum(lf, seg) * LOG2E
    c_ref[...] = jnp.zeros(c_ref.shape, F32)
    c_ref[0:c.shape[0], :] = c


def _fox_gates(fl_t, bias, *, col0, cols, seg, block, pad_rows):
    H = fl_t.shape[0]
    nb = cols // block
    base = col0 // block
    return pl.pallas_call(
        functools.partial(_gate_kernel, seg=seg),
        grid=(nb,),
        in_specs=[pl.BlockSpec((H, block), lambda i: (0, base + i)),
                  pl.BlockSpec((H, 1), lambda i: (0, 0))],
        out_specs=[pl.BlockSpec((H, block), lambda i: (0, i)),
                   pl.BlockSpec((None, pad_rows, block), lambda i: (i, 0, 0))],
        out_shape=[jax.ShapeDtypeStruct((H, cols), F32),
                   jax.ShapeDtypeStruct((nb, pad_rows, block), F32)],
        compiler_params=_params("parallel"),
        name="fox_gates",
    )(fl_t, bias)


def _suffix_kernel(lf_ref, o_ref):
    lf = lf_ref[...]
    c = _lane_cumsum(lf, lf.shape[-1])
    o_ref[...] = (c[:, -1:] - c) * LOG2E


def _cache_suffix(clf_t):
    B, H, P = clf_t.shape
    spec = pl.BlockSpec((None, H, P), lambda b: (b, 0, 0))
    return pl.pallas_call(
        _suffix_kernel,
        grid=(B,),
        in_specs=[spec],
        out_specs=spec,
        out_shape=jax.ShapeDtypeStruct((B, H, P), F32),
        compiler_params=_params("parallel"),
        name="fox_cache_suffix",
    )(clf_t)


SUB = 16


def _gla_kernel(aq_ref, af_ref, ai_ref, ag_ref, lbl_ref, gn_ref, s0_ref, tri_ref,
                o_ref, s_out_ref, st_ref, *, chunk, n_chunks):
    t = pl.program_id(1)

    @pl.when(t == 0)
    def _():
        for h in range(N_HEADS):
            st_ref[h] = s0_ref[h].T

    lbl = lbl_ref[...]
    e = jnp.exp(lbl - jnp.max(lbl, axis=0, keepdims=True))
    lb = e[0:1] / jnp.sum(e, axis=0, keepdims=True)
    gn = gn_ref[...]
    tri = tri_ref[...]
    n_sub = chunk // SUB

    def one_chunk(ci, carry):
        r = pl.ds(pl.multiple_of(ci * chunk, chunk), chunk)
        fa = lb + (1.0 - lb) * _sigmoid(af_ref[r, :])
        lf = jnp.log(fa)
        ka = 1.0 - fa
        aq = aq_ref[r, :]
        qa = aq * _sigmoid(aq)
        v = ai_ref[r, :].astype(BF16)
        ag = ag_ref[r, :]
        gate = ag * _sigmoid(ag)

        hi = lf.astype(BF16)
        r1 = lf - hi.astype(F32)
        mid = r1.astype(BF16)
        lo = (r1 - mid.astype(F32)).astype(BF16)
        b = (jnp.dot(tri, hi, preferred_element_type=F32)
             + jnp.dot(tri, mid, preferred_element_type=F32)
             + jnp.dot(tri, lo, preferred_element_type=F32))
        b_end = b[chunk - 1:chunk, :]

        q_in = (qa * jnp.exp(b)).astype(BF16)
        k_out = (ka * jnp.exp(b_end - b)).astype(BF16)
        decay = jnp.exp(b_end)
        qp, kp = [], []
        for i in range(n_sub):
            r0, n = i * SUB, (i + 1) * SUB
            m_i = b[r0 + SUB // 2:r0 + SUB // 2 + 1, :]
            qp.append((qa[r0:n] * jnp.exp(b[r0:n] - m_i)).astype(BF16))
            kp.append((ka[:n] * jnp.exp(m_i - b[:n])).astype(BF16))

        st = [st_ref[h] for h in range(N_HEADS)]
        inter = [lax.dot_general(q_in[:, _head(h)], st[h].astype(BF16), NT_DIMS,
                                 preferred_element_type=F32) for h in range(N_HEADS)]
        att = [[lax.dot_general(qp[i][:, _head(h)], kp[i][:, _head(h)], NT_DIMS,
                                preferred_element_type=F32) for h in range(N_HEADS)]
               for i in range(n_sub)]
        for h in range(N_HEADS):
            st_ref[h] = st[h] * decay[:, _head(h)] + lax.dot_general(
                v[:, _head(h)], k_out[:, _head(h)], TN_DIMS, preferred_element_type=F32)
        for i in range(n_sub):
            r0, n = i * SUB, (i + 1) * SUB
            causal = (lax.broadcasted_iota(jnp.int32, (SUB, n), 1)
                      <= lax.broadcasted_iota(jnp.int32, (SUB, n), 0) + r0)
            att[i] = [jnp.where(causal, a, 0.0).astype(BF16) for a in att[i]]
        for h in range(N_HEADS):
            o = jnp.concatenate(
                [inter[h][i * SUB:(i + 1) * SUB]
                 + jnp.dot(att[i][h], v[:(i + 1) * SUB, _head(h)], preferred_element_type=F32)
                 for i in range(n_sub)], axis=0)
            ms = jnp.mean(o * o, axis=-1, keepdims=True)
            o_ref[r, _head(h)] = (o * lax.rsqrt(ms + EPS) * gn * gate[:, _head(h)]).astype(o_ref.dtype)
        return carry

    lax.fori_loop(0, n_chunks, one_chunk, 0)

    @pl.when(t == pl.num_programs(1) - 1)
    def _():
        for h in range(N_HEADS):
            s_out_ref[h] = st_ref[h].T


def _hgrn2(a_proj, lb_logits, gnorm, s0, *, batch, length, row0, chunk, n_chunks):
    step = chunk * n_chunks
    nt = length // step
    base = row0 // step

    def col(g):
        return pl.BlockSpec((step, WIDTH), lambda b, t: (base + b * nt + t, g))

    st_spec = pl.BlockSpec((None, N_HEADS, HEAD_DIM, HEAD_DIM), lambda b, t: (b, 0, 0, 0))
    tri = jnp.tril(jnp.ones((chunk, chunk), BF16))
    return pl.pallas_call(
        functools.partial(_gla_kernel, chunk=chunk, n_chunks=n_chunks),
        grid=(batch, nt),
        in_specs=[col(0), col(1), col(2), col(3),
                  pl.BlockSpec(lb_logits.shape, lambda b, t: (0, 0)),
                  pl.BlockSpec((1, HEAD_DIM), lambda b, t: (0, 0)),
                  st_spec,
                  pl.BlockSpec((chunk, chunk), lambda b, t: (0, 0))],
        out_specs=[pl.BlockSpec((step, WIDTH), lambda b, t: (b * nt + t, 0)), st_spec],
        out_shape=[jax.ShapeDtypeStruct((batch * length, WIDTH), BF16),
                   jax.ShapeDtypeStruct((batch, N_HEADS, HEAD_DIM, HEAD_DIM), F32)],
        scratch_shapes=[pltpu.VMEM((N_HEADS, HEAD_DIM, HEAD_DIM), F32)],
        compiler_params=_params("parallel", "arbitrary"),
        name="hgrn2",
    )(a_proj, a_proj, a_proj, a_proj, lb_logits, gnorm, s0, tri)


def _attn_prompt_kernel(q_ref, k_ref, v_ref, c_ref, o_ref,
                        vt_ref, cb_ref, m_ref, l_ref, acc_ref, *, blk):
    h, iq = pl.program_id(1), pl.program_id(2)
    length = k_ref.shape[0]
    halves = (slice(0, blk // 2), slice(blk // 2, blk))

    @pl.when(iq == 0)
    def _():
        for r0 in range(0, length, blk):
            rs = slice(r0, r0 + blk)
            vt_ref[:, rs] = v_ref[rs, :].astype(F32).T.astype(BF16)
            c_cols = c_ref[:, rs].T
            onehot = lax.broadcasted_iota(jnp.int32, c_cols.shape, 1) == h
            cb_ref[rs, :] = jnp.broadcast_to(
                jnp.sum(jnp.where(onehot, c_cols, 0.0), axis=1, keepdims=True), (blk, LANES))

    q = q_ref[...]
    m_ref[...] = jnp.full(m_ref.shape, -jnp.inf, F32)
    l_ref[...] = jnp.zeros(l_ref.shape, F32)
    acc_ref[...] = jnp.zeros(acc_ref.shape, F32)

    def step(j, masked):
        ks = pl.ds(pl.multiple_of(j * blk, blk), blk)
        kb = k_ref[ks, :]
        cb = cb_ref[ks, :]
        vt = vt_ref[:, ks]
        s = [lax.dot_general(kb, q[hv, :], NT_DIMS, preferred_element_type=F32) for hv in halves]
        p = []
        for x, hv in zip(s, halves):
            x = x - jnp.concatenate([cb] * (x.shape[1] // LANES), axis=1)
            if masked:
                key = lax.broadcasted_iota(jnp.int32, x.shape, 0)
                qry = lax.broadcasted_iota(jnp.int32, x.shape, 1) + hv.start
                x = jnp.where(key <= qry, x, -jnp.inf)
            m_prev = m_ref[:, hv]
            m_new = jnp.maximum(m_prev, jnp.max(x, axis=0, keepdims=True))
            alpha = jnp.exp2(m_prev - m_new)
            e = jnp.exp2(x - m_new)
            l_ref[:, hv] = alpha * l_ref[:, hv] + jnp.sum(e, axis=0, keepdims=True)
            m_ref[:, hv] = m_new
            acc_ref[:, hv] = alpha * acc_ref[:, hv]
            p.append(e.astype(BF16))
        for e, hv in zip(p, halves):
            acc_ref[:, hv] += jnp.dot(vt, e, preferred_element_type=F32)

    def body(j, carry):
        step(j, False)
        return carry

    lax.fori_loop(0, iq, body, 0)
    step(iq, True)
    o_ref[...] = (acc_ref[...] / l_ref[...]).T.astype(o_ref.dtype)


def _fox_prompt(q, k, v, c, *, batch, length, blk):
    nq = length // blk
    qspec = pl.BlockSpec((blk, HEAD_DIM), lambda b, h, i: (b * nq + i, h))
    kvspec = pl.BlockSpec((length, HEAD_DIM), lambda b, h, i: (b, h))
    return pl.pallas_call(
        functools.partial(_attn_prompt_kernel, blk=blk),
        grid=(batch, N_HEADS, nq),
        in_specs=[qspec, kvspec, kvspec,
                  pl.BlockSpec((None,) + c.shape[1:], lambda b, h, i: (b, 0, 0))],
        out_specs=qspec,
        out_shape=jax.ShapeDtypeStruct((batch * length, WIDTH), BF16),
        scratch_shapes=[pltpu.VMEM((HEAD_DIM, length), BF16),
                        pltpu.VMEM((length, LANES), F32),
                        pltpu.VMEM((1, blk), F32), pltpu.VMEM((1, blk), F32),
                        pltpu.VMEM((HEAD_DIM, blk), F32)],
        compiler_params=_params("parallel", "parallel", "arbitrary"),
        name="fox_prompt",
    )(q, k, v, c)


def _softmax_heads(s, v, m_ref, l_ref, acc_ref):
    stats = []
    for h in range(N_HEADS):
        m_prev = m_ref[h]
        m_new = jnp.maximum(m_prev, jnp.max(s[h], axis=-1, keepdims=True))
        stats.append((jnp.exp2(m_prev - m_new), m_new))
    p = [jnp.exp2(s[h] - stats[h][1]) for h in range(N_HEADS)]
    for h in range(N_HEADS):
        alpha, m_new = stats[h]
        l_ref[h] = alpha * l_ref[h] + jnp.sum(p[h], axis=-1, keepdims=True)
        m_ref[h] = m_new
        acc_ref[:, _head(h)] = alpha * acc_ref[:, _head(h)] + jnp.dot(
            p[h].astype(BF16), v[h], preferred_element_type=F32)


def _attn_sample_kernel(q_ref, kc_ref, vc_ref, bc_ref, kn_ref, vn_ref, bn_ref, o_ref,
                        m_ref, l_ref, acc_ref):
    j = pl.program_id(1)
    tk = bc_ref.shape[1]

    @pl.when(j == 0)
    def _():
        m_ref[...] = jnp.full(m_ref.shape, -jnp.inf, F32)
        l_ref[...] = jnp.zeros(l_ref.shape, F32)
        acc_ref[...] = jnp.zeros(acc_ref.shape, F32)

    s = [lax.dot_general(q_ref[:, _head(h)], kc_ref[_head_rows(h, tk), :].astype(BF16), NT_DIMS,
                         preferred_element_type=F32) + bc_ref[h:h + 1, :] for h in range(N_HEADS)]
    _softmax_heads(s, [vc_ref[_head_rows(h, tk), :].astype(BF16) for h in range(N_HEADS)], m_ref, l_ref, acc_ref)

    @pl.when(j == pl.num_programs(1) - 1)
    def _():
        tq = q_ref.shape[0]
        causal = (lax.broadcasted_iota(jnp.int32, (tq, tq), 1)
                  <= lax.broadcasted_iota(jnp.int32, (tq, tq), 0))
        s = [jnp.where(causal,
                       lax.dot_general(q_ref[:, _head(h)], kn_ref[:, _head(h)], NT_DIMS,
                                       preferred_element_type=F32) - bn_ref[h:h + 1, :],
                       -jnp.inf) for h in range(N_HEADS)]
        _softmax_heads(s, [vn_ref[:, _head(h)] for h in range(N_HEADS)],
                       m_ref, l_ref, acc_ref)
        for h in range(N_HEADS):
            o_ref[:, _head(h)] = (acc_ref[:, _head(h)] / l_ref[h]).astype(o_ref.dtype)


def _fox_sample(q, k_new, v_new, c_new, cache_k, cache_v, cache_bias, *, row0, tk):
    B, H, P = cache_bias.shape
    Ls = c_new.shape[2]
    base = row0 // Ls
    new = pl.BlockSpec((Ls, WIDTH), lambda b, j: (base + b, 0))
    cache = pl.BlockSpec((None, tk * H, HEAD_DIM), lambda b, j: (b, j, 0))
    return pl.pallas_call(
        _attn_sample_kernel,
        grid=(B, P // tk),
        in_specs=[new, cache, cache,
                  pl.BlockSpec((None, N_HEADS, tk), lambda b, j: (b, 0, j)),
                  new, new,
                  pl.BlockSpec((None, N_HEADS, Ls), lambda b, j: (b, 0, 0))],
        out_specs=pl.BlockSpec((Ls, WIDTH), lambda b, j: (b, 0)),
        out_shape=jax.ShapeDtypeStruct((B * Ls, WIDTH), BF16),
        scratch_shapes=[pltpu.VMEM((N_HEADS, Ls, 1), F32), pltpu.VMEM((N_HEADS, Ls, 1), F32),
                        pltpu.VMEM((Ls, WIDTH), F32)],
        compiler_params=_params("parallel", "arbitrary"),
        name="fox_sample",
    )(q, cache_k, cache_v, cache_bias, k_new, v_new, c_new)


MERGE_COLS = 512


def _merge_kernel(xp_ref, xs_ref, oap_ref, oas_ref, obp_ref, obs_ref, g_ref,
                  wpa_ref, wpb_ref, wo_ref, o_ref, mg_ref, *, n_first):
    first = pl.program_id(0) < n_first
    D = xp_ref.shape[1]
    oa = jnp.where(first, oap_ref[...], oas_ref[...])
    ob = jnp.where(first, obp_ref[...], obs_ref[...])
    cols = min(MERGE_COLS, D)
    for n0 in range(0, D, cols):
        ns = slice(n0, n0 + cols)
        pa = jnp.dot(oa, wpa_ref[:, ns], preferred_element_type=F32)
        pb = jnp.dot(ob, wpb_ref[:, ns], preferred_element_type=F32)
        ga = g_ref[:, ns].astype(F32)
        gb = g_ref[:, D + n0:D + n0 + cols].astype(F32)
        mg_ref[:, ns] = (ga * pa + gb * pb).astype(BF16)
    x = jnp.where(first, xp_ref[...], xs_ref[...])
    o_ref[...] = x + jnp.dot(mg_ref[...], wo_ref[...], preferred_element_type=F32)


def _merge_out(xp, xs, oa_p, oa_s, ob_p, ob_s, g, w_pa, w_pb, w_o, *, tm):
    D = xp.shape[1]
    T = xp.shape[0] + xs.shape[0]
    n_first = xp.shape[0] // tm
    fixed = lambda i: (0, 0)
    return pl.pallas_call(
        functools.partial(_merge_kernel, n_first=n_first),
        grid=(T // tm,),
        in_specs=[*_two_streams(tm, D, n_first), *_two_streams(tm, WIDTH, n_first),
                  *_two_streams(tm, WIDTH, n_first),
                  pl.BlockSpec((tm, 2 * D), lambda i: (i, 0)),
                  *[pl.BlockSpec(w.shape, fixed, pipeline_mode=pl.Buffered(1))
                    for w in (w_pa, w_pb, w_o)]],
        out_specs=pl.BlockSpec((tm, D), lambda i: (i, 0)),
        out_shape=jax.ShapeDtypeStruct((T, D), F32),
        scratch_shapes=[pltpu.VMEM((tm, D), BF16)],
        compiler_params=_params("parallel"),
        name="merge_out",
    )(xp, xs, oa_p, oa_s, ob_p, ob_s, g, w_pa, w_pb, w_o)


def _ffn_kernel(x_ref, g2_ref, w1_ref, w2_ref, gf_ref, yp_ref, ys_ref, h_ref, acc_ref, *, n_first):
    i, f = pl.program_id(0), pl.program_id(1)

    @pl.when(f == 0)
    def _():
        h_ref[...] = _normed(x_ref[...], g2_ref)
        acc_ref[...] = x_ref[...]

    u = jnp.maximum(jnp.dot(h_ref[...], w1_ref[...], preferred_element_type=F32), 0.0)
    acc_ref[...] += jnp.dot((u * u).astype(BF16), w2_ref[...], preferred_element_type=F32)

    @pl.when(f == pl.num_programs(1) - 1)
    def _():
        x = acc_ref[...]
        ms = jnp.mean(x * x, axis=-1, keepdims=True)
        y = x * lax.rsqrt(ms + EPS) * gf_ref[...]

        @pl.when(i < n_first)
        def _():
            yp_ref[...] = y

        @pl.when(i >= n_first)
        def _():
            ys_ref[...] = y


def _ffn(x, g2, w1, w2, gf, *, rows_first, tm, tf):
    T, D = x.shape
    F = w1.shape[1]
    n_first = rows_first // tm
    vec = pl.BlockSpec((1, D), lambda i, f: (0, 0))
    return pl.pallas_call(
        functools.partial(_ffn_kernel, n_first=n_first),
        grid=(T // tm, F // tf),
        in_specs=[pl.BlockSpec((tm, D), lambda i, f: (i, 0)), vec,
                  pl.BlockSpec((D, tf), lambda i, f: (0, f)),
                  pl.BlockSpec((tf, D), lambda i, f: (f, 0)), vec],
        out_specs=list(_two_streams(tm, D, n_first)),
        out_shape=[jax.ShapeDtypeStruct((rows_first, D), F32),
                   jax.ShapeDtypeStruct((T - rows_first, D), F32)],
        scratch_shapes=[pltpu.VMEM((tm, D), BF16), pltpu.VMEM((tm, D), F32)],
        compiler_params=_params("arbitrary", "arbitrary"),
        name="ffn",
    )(x, g2, w1, w2, gf)


def _tile(n, pref):
    t = min(pref, n)
    while n % t:
        t //= 2
    return t


def kernel(x_prompt, x_sample, cache_fox_k, cache_fox_v, cache_fox_logf, state_hgrn, norm1, w_in,
           b_fox_f, lb_logits, gnorm_a, w_pa, w_pb, w_o, norm2, w1, w2, norm_f):
    B, L, D = x_prompt.shape
    Bs, Ls, _ = x_sample.shape
    depth, _, P, H, DH = cache_fox_k.shape
    assert depth == 1 and H == N_HEADS and DH == HEAD_DIM
    Tp, Ts = B * L, Bs * Ls
    T = Tp + Ts
    W = WIDTH

    xp, xs = x_prompt.reshape(Tp, D), x_sample.reshape(Ts, D)
    w_in0 = w_in[0]
    w_a = w_in0[:, :4 * W].astype(BF16)
    w_b = w_in0[:, 4 * W:7 * W].astype(BF16)
    w_fl_t = w_in0[:, 7 * W:7 * W + H].T.astype(BF16)
    w_g = w_in0[:, 7 * W + H:].astype(BF16)
    g1 = norm1[0].reshape(1, D)

    tm = _tile(math.gcd(Tp, Ts), 512)
    a_proj = _norm_proj(xp, xs, g1, w_a, act=None, out_dtype=F32, tm=tm, tn=W)
    gates = _norm_proj(xp, xs, g1, w_g, act="sigmoid", out_dtype=BF16, tm=tm, tn=_tile(2 * D, W))
    q_b, k_b, v_b, k_p, k_s, v_p, v_s, fl_t = _norm_proj_fox(xp, xs, g1, w_b, w_fl_t,
                                                              tm=_tile(tm, 512))

    bias = b_fox_f[0].reshape(H, 1)
    lf_p, c_p = _fox_gates(fl_t, bias, col0=0, cols=Tp, seg=L, block=L, pad_rows=LANES)
    lf_s, c_s = _fox_gates(fl_t, bias, col0=Tp, cols=Ts, seg=Ls, block=_tile(Ts, 1024), pad_rows=H)
    cache_bias = _cache_suffix(jnp.transpose(cache_fox_logf[0], (0, 2, 1)))

    zeros_state = jnp.zeros((B, H, DH, DH), F32)
    chunk_p = _tile(L, 64)
    oa_p, s_p = _hgrn2(a_proj, lb_logits, gnorm_a, zeros_state, batch=B, length=L, row0=0,
                       chunk=chunk_p, n_chunks=_tile(L // chunk_p, 4))
    oa_s, s_s = _hgrn2(a_proj, lb_logits, gnorm_a, state_hgrn[0], batch=Bs, length=Ls, row0=Tp,
                       chunk=_tile(Ls, 64), n_chunks=1)

    ob_p = _fox_prompt(q_b, k_b, v_b, c_p, batch=B, length=L, blk=_tile(L, 512))
    c_s3 = c_s.transpose(1, 0, 2).reshape(H, Bs, Ls).transpose(1, 0, 2)
    ob_s = _fox_sample(q_b, k_b, v_b, c_s3, cache_fox_k.reshape(Bs, P * H, DH),
                       cache_fox_v.reshape(Bs, P * H, DH), cache_bias,
                       row0=Tp, tk=_tile(P, 1024))

    tm5 = _tile(tm, 512)
    x1 = _merge_out(xp, xs, oa_p, oa_s, ob_p, ob_s, gates, w_pa[0].astype(BF16),
                    w_pb[0].astype(BF16), w_o[0].astype(BF16), tm=_tile(tm, 256))
    y_p, y_s = _ffn(x1, norm2[0].reshape(1, D), w1[0].astype(BF16), w2[0].astype(BF16),
                    norm_f.reshape(1, D), rows_first=Tp, tm=tm5, tf=_tile(w1.shape[-1], 1024))

    return (y_p.reshape(B, L, D), y_s.reshape(Bs, Ls, D),
            k_p.reshape(1, B, L, H, DH), v_p.reshape(1, B, L, H, DH),
            lf_p.T.reshape(1, B, L, H), s_p[None],
            k_s.reshape(1, Bs, Ls, H, DH), v_s.reshape(1, Bs, Ls, H, DH),
            lf_s.T.reshape(1, Bs, Ls, H), s_s[None])
```

```python
import functools
import math

import jax
import jax.numpy as jnp
from jax import lax
from jax.experimental import pallas as pl
from jax.experimental.pallas import tpu as pltpu

F32 = jnp.float32
BF16 = jnp.bfloat16

EPS = 1e-6
N_HEADS = 8
HEAD_DIM = 128
WIDTH = N_HEADS * HEAD_DIM
LANES = 128
VMEM_LIMIT = 56 * 1024 * 1024
LOG2E = math.log2(math.e)
QK_SCALE = HEAD_DIM ** -0.5 * LOG2E

NT_DIMS = (((1,), (1,)), ((), ()))
TN_DIMS = (((0,), (0,)), ((), ()))


def _params(*sem):
    return pltpu.CompilerParams(dimension_semantics=sem, vmem_limit_bytes=VMEM_LIMIT)


def _sigmoid(x):
    return 1.0 / (1.0 + jnp.exp(-x))


def _head(h):
    return slice(h * HEAD_DIM, (h + 1) * HEAD_DIM)


def _two_streams(rows, width, n_first, n_second=None):
    mode = {"pipeline_mode": pl.Buffered(1)} if n_second == 1 else {}
    return (pl.BlockSpec((rows, width), lambda i, *_: (jnp.minimum(i, n_first - 1), 0)),
            pl.BlockSpec((rows, width), lambda i, *_: (jnp.maximum(i - n_first, 0), 0), **mode))


def _head_rows(h, rows):
    return pl.ds(h, rows, stride=N_HEADS)


def _normed(x, g_ref):
    ms = jnp.mean(x * x, axis=-1, keepdims=True)
    return (x * lax.rsqrt(ms + EPS) * g_ref[...]).astype(BF16)


def _proj_kernel(xp_ref, xs_ref, g_ref, w_ref, o_ref, h_ref, *, act, n_first):
    @pl.when(pl.program_id(1) == 0)
    def _():
        x = jnp.where(pl.program_id(0) < n_first, xp_ref[...], xs_ref[...])
        h_ref[...] = _normed(x, g_ref)

    acc = jnp.dot(h_ref[...], w_ref[...], preferred_element_type=F32)
    if act == "sigmoid":
        acc = _sigmoid(acc)
    o_ref[...] = acc.astype(o_ref.dtype)


def _norm_proj(xp, xs, g, w, *, act, out_dtype, tm, tn):
    D = xp.shape[1]
    N = w.shape[1]
    n_first = xp.shape[0] // tm
    T = xp.shape[0] + xs.shape[0]
    return pl.pallas_call(
        functools.partial(_proj_kernel, act=act, n_first=n_first),
        grid=(T // tm, N // tn),
        in_specs=[*_two_streams(tm, D, n_first, xs.shape[0] // tm),
                  pl.BlockSpec((1, D), lambda i, j: (0, 0)),
                  pl.BlockSpec((D, tn), lambda i, j: (0, j))],
        out_specs=pl.BlockSpec((tm, tn), lambda i, j: (i, j)),
        out_shape=jax.ShapeDtypeStruct((T, N), out_dtype),
        scratch_shapes=[pltpu.VMEM((tm, D), BF16)],
        compiler_params=_params("parallel", "arbitrary"),
        name="norm_proj_" + (act or "lin"),
    )(xp, xs, g, w)


def _proj_fox_kernel(xp_ref, xs_ref, g_ref, w_ref, wfl_ref,
                     q_ref, kb_ref, vb_ref, kp_ref, ks_ref, vp_ref, vs_ref, fl_ref, h_ref,
                     *, n_first):
    i, j = pl.program_id(0), pl.program_id(1)

    @pl.when(j == 0)
    def _():
        x = jnp.where(i < n_first, xp_ref[...], xs_ref[...])
        h = _normed(x, g_ref)
        h_ref[...] = h
        fl_ref[...] = lax.dot_general(wfl_ref[...], h, NT_DIMS, preferred_element_type=F32)

    acc = jnp.dot(h_ref[...], w_ref[...], preferred_element_type=F32)

    @pl.when(j == 0)
    def _():
        q_ref[...] = (acc * QK_SCALE).astype(BF16)

    def heads_out(first_ref, second_ref):
        @pl.when(i < n_first)
        def _():
            for h in range(N_HEADS):
                first_ref[_head_rows(h, acc.shape[0]), :] = acc[:, _head(h)]

        @pl.when(i >= n_first)
        def _():
            for h in range(N_HEADS):
                second_ref[_head_rows(h, acc.shape[0]), :] = acc[:, _head(h)]

    @pl.when(j == 1)
    def _():
        kb_ref[...] = acc.astype(BF16)
        heads_out(kp_ref, ks_ref)

    @pl.when(j == 2)
    def _():
        vb_ref[...] = acc.astype(BF16)
        heads_out(vp_ref, vs_ref)


def _norm_proj_fox(xp, xs, g, w_qkv, w_fl_t, *, tm):
    D = xp.shape[1]
    Tp, Ts = xp.shape[0], xs.shape[0]
    T = Tp + Ts
    n_first = Tp // tm
    kv_p, kv_s = _two_streams(tm * N_HEADS, HEAD_DIM, n_first)
    kv_shape = lambda rows: jax.ShapeDtypeStruct((rows * N_HEADS, HEAD_DIM), F32)
    tok = pl.BlockSpec((tm, WIDTH), lambda i, j: (i, 0))
    tok_shape = jax.ShapeDtypeStruct((T, WIDTH), BF16)
    return pl.pallas_call(
        functools.partial(_proj_fox_kernel, n_first=n_first),
        grid=(T // tm, 3),
        in_specs=[*_two_streams(tm, D, n_first),
                  pl.BlockSpec((1, D), lambda i, j: (0, 0)),
                  pl.BlockSpec((D, WIDTH), lambda i, j: (0, j)),
                  pl.BlockSpec((N_HEADS, D), lambda i, j: (0, 0))],
        out_specs=[tok, tok, tok, kv_p, kv_s, kv_p, kv_s,
                   pl.BlockSpec((N_HEADS, tm), lambda i, j: (0, i))],
        out_shape=[tok_shape, tok_shape, tok_shape,
                   kv_shape(Tp), kv_shape(Ts), kv_shape(Tp), kv_shape(Ts),
                   jax.ShapeDtypeStruct((N_HEADS, T), F32)],
        scratch_shapes=[pltpu.VMEM((tm, D), BF16)],
        compiler_params=_params("arbitrary", "arbitrary"),
        name="norm_proj_fox",
    )(xp, xs, g, w_qkv, w_fl_t)


def _lane_cumsum(x, seg):
    pos = lax.broadcasted_iota(jnp.int32, x.shape, x.ndim - 1) & (seg - 1)
    shift = 1
    while shift < seg:
        x = x + jnp.where(pos >= shift, pltpu.roll(x, shift, x.ndim - 1), 0.0)
        shift *= 2
    return x


def _gate_kernel(fl_ref, bias_ref, lf_ref, c_ref, *, seg):
    z = fl_ref[...] + bias_ref[...]
    lf = jnp.minimum(z, 0.0) - jnp.log(1.0 + jnp.exp(-jnp.abs(z)))
    lf_ref[...] = lf
    c = _lane_cumsum(lf, seg) * LOG2E
    c_ref[...] = jnp.zeros(c_ref.shape, F32)
    c_ref[0:c.shape[0], :] = c


def _fox_gates(fl_t, bias, *, col0, cols, seg, block, pad_rows):
    H = fl_t.shape[0]
    nb = cols // block
    base = col0 // block
    return pl.pallas_call(
        functools.partial(_gate_kernel, seg=seg),
        grid=(nb,),
        in_specs=[pl.BlockSpec((H, block), lambda i: (0, base + i)),
                  pl.BlockSpec((H, 1), lambda i: (0, 0))],
        out_specs=[pl.BlockSpec((H, block), lambda i: (0, i)),
                   pl.BlockSpec((None, pad_rows, block), lambda i: (i, 0, 0))],
        out_shape=[jax.ShapeDtypeStruct((H, cols), F32),
                   jax.ShapeDtypeStruct((nb, pad_rows, block), F32)],
        compiler_params=_params("parallel"),
        name="fox_gates",
    )(fl_t, bias)


def _suffix_kernel(lf_ref, o_ref):
    lf = lf_ref[...]
    c = _lane_cumsum(lf, lf.shape[-1])
    o_ref[...] = (c[:, -1:] - c) * LOG2E


def _cache_suffix(clf_t):
    B, H, P = clf_t.shape
    spec = pl.BlockSpec((None, H, P), lambda b: (b, 0, 0))
    return pl.pallas_call(
        _suffix_kernel,
        grid=(B,),
        in_specs=[spec],
        out_specs=spec,
        out_shape=jax.ShapeDtypeStruct((B, H, P), F32),
        compiler_params=_params("parallel"),
        name="fox_cache_suffix",
    )(clf_t)


SUB = 16


def _gla_kernel(aq_ref, af_ref, ai_ref, ag_ref, lbl_ref, gn_ref, s0_ref, tri_ref,
                o_ref, s_out_ref, st_ref, *, chunk, n_chunks):
    t = pl.program_id(1)

    @pl.when(t == 0)
    def _():
        for h in range(N_HEADS):
            st_ref[h] = s0_ref[h].T

    lbl = lbl_ref[...]
    e = jnp.exp(lbl - jnp.max(lbl, axis=0, keepdims=True))
    lb = e[0:1] / jnp.sum(e, axis=0, keepdims=True)
    gn = gn_ref[...]
    tri = tri_ref[...]
    n_sub = chunk // SUB

    def one_chunk(ci, carry):
        r = pl.ds(pl.multiple_of(ci * chunk, chunk), chunk)
        fa = lb + (1.0 - lb) * _sigmoid(af_ref[r, :])
        lf = jnp.log(fa)
        ka = 1.0 - fa
        aq = aq_ref[r, :]
        qa = aq * _sigmoid(aq)
        v = ai_ref[r, :].astype(BF16)
        ag = ag_ref[r, :]
        gate = ag * _sigmoid(ag)

        hi = lf.astype(BF16)
        r1 = lf - hi.astype(F32)
        mid = r1.astype(BF16)
        lo = (r1 - mid.astype(F32)).astype(BF16)
        b = (jnp.dot(tri, hi, preferred_element_type=F32)
             + jnp.dot(tri, mid, preferred_element_type=F32)
             + jnp.dot(tri, lo, preferred_element_type=F32))
        b_end = b[chunk - 1:chunk, :]

        q_in = (qa * jnp.exp(b)).astype(BF16)
        k_out = (ka * jnp.exp(b_end - b)).astype(BF16)
        decay = jnp.exp(b_end)
        qp, kp = [], []
        for i in range(n_sub):
            r0, n = i * SUB, (i + 1) * SUB
            m_i = b[r0 + SUB // 2:r0 + SUB // 2 + 1, :]
            qp.append((qa[r0:n] * jnp.exp(b[r0:n] - m_i)).astype(BF16))
            kp.append((ka[:n] * jnp.exp(m_i - b[:n])).astype(BF16))

        st = [st_ref[h] for h in range(N_HEADS)]
        inter = [lax.dot_general(q_in[:, _head(h)], st[h].astype(BF16), NT_DIMS,
                                 preferred_element_type=F32) for h in range(N_HEADS)]
        att = [[lax.dot_general(qp[i][:, _head(h)], kp[i][:, _head(h)], NT_DIMS,
                                preferred_element_type=F32) for h in range(N_HEADS)]
               for i in range(n_sub)]
        for h in range(N_HEADS):
            st_ref[h] = st[h] * decay[:, _head(h)] + lax.dot_general(
                v[:, _head(h)], k_out[:, _head(h)], TN_DIMS, preferred_element_type=F32)
        for i in range(n_sub):
            r0, n = i * SUB, (i + 1) * SUB
            causal = (lax.broadcasted_iota(jnp.int32, (SUB, n), 1)
                      <= lax.broadcasted_iota(jnp.int32, (SUB, n), 0) + r0)
            att[i] = [jnp.where(causal, a, 0.0).astype(BF16) for a in att[i]]
        for h in range(N_HEADS):
            o = jnp.concatenate(
                [inter[h][i * SUB:(i + 1) * SUB]
                 + jnp.dot(att[i][h], v[:(i + 1) * SUB, _head(h)], preferred_element_type=F32)
                 for i in range(n_sub)], axis=0)
            ms = jnp.mean(o * o, axis=-1, keepdims=True)
            o_ref[r, _head(h)] = (o * lax.rsqrt(ms + EPS) * gn * gate[:, _head(h)]).astype(o_ref.dtype)
        return carry

    lax.fori_loop(0, n_chunks, one_chunk, 0)

    @pl.when(t == pl.num_programs(1) - 1)
    def _():
        for h in range(N_HEADS):
            s_out_ref[h] = st_ref[h].T


def _hgrn2(a_proj, lb_logits, gnorm, s0, *, batch, length, row0, chunk, n_chunks):
    step = chunk * n_chunks
    nt = length // step
    base = row0 // step

    def col(g):
        return pl.BlockSpec((step, WIDTH), lambda b, t: (base + b * nt + t, g))

    st_spec = pl.BlockSpec((None, N_HEADS, HEAD_DIM, HEAD_DIM), lambda b, t: (b, 0, 0, 0))
    tri = jnp.tril(jnp.ones((chunk, chunk), BF16))
    return pl.pallas_call(
        functools.partial(_gla_kernel, chunk=chunk, n_chunks=n_chunks),
        grid=(batch, nt),
        in_specs=[col(0), col(1), col(2), col(3),
                  pl.BlockSpec(lb_logits.shape, lambda b, t: (0, 0)),
                  pl.BlockSpec((1, HEAD_DIM), lambda b, t: (0, 0)),
                  st_spec,
                  pl.BlockSpec((chunk, chunk), lambda b, t: (0, 0))],
        out_specs=[pl.BlockSpec((step, WIDTH), lambda b, t: (b * nt + t, 0)), st_spec],
        out_shape=[jax.ShapeDtypeStruct((batch * length, WIDTH), BF16),
                   jax.ShapeDtypeStruct((batch, N_HEADS, HEAD_DIM, HEAD_DIM), F32)],
        scratch_shapes=[pltpu.VMEM((N_HEADS, HEAD_DIM, HEAD_DIM), F32)],
        compiler_params=_params("parallel", "arbitrary"),
        name="hgrn2",
    )(a_proj, a_proj, a_proj, a_proj, lb_logits, gnorm, s0, tri)


def _attn_prompt_kernel(q_ref, k_ref, v_ref, c_ref, o_ref,
                        vt_ref, cb_ref, s0_ref, s1_ref, m_ref, l_ref, acc_ref, *, blk):
    h, iq = pl.program_id(1), pl.program_id(2)
    length = k_ref.shape[0]
    halves = (slice(0, blk // 2), slice(blk // 2, blk))

    @pl.when(iq == 0)
    def _():
        for r0 in range(0, length, blk):
            rs = slice(r0, r0 + blk)
            vt_ref[:, rs] = v_ref[rs, :].astype(F32).T.astype(BF16)
            c_cols = c_ref[:, rs].T
            onehot = lax.broadcasted_iota(jnp.int32, c_cols.shape, 1) == h
            cb_ref[rs, :] = jnp.broadcast_to(
                jnp.sum(jnp.where(onehot, c_cols, 0.0), axis=1, keepdims=True), (blk, LANES))

    q = q_ref[...]
    m_ref[...] = jnp.full(m_ref.shape, -jnp.inf, F32)
    l_ref[...] = jnp.zeros(l_ref.shape, F32)
    acc_ref[...] = jnp.zeros(acc_ref.shape, F32)

    def scores(j, s_ref):
        ks = pl.ds(pl.multiple_of(j * blk, blk), blk)
        kb = k_ref[ks, :]
        cb = cb_ref[ks, :]
        for hv in halves:
            x = lax.dot_general(kb, q[hv, :], NT_DIMS, preferred_element_type=F32)
            s_ref[:, hv] = x - jnp.concatenate([cb] * (x.shape[1] // LANES), axis=1)

    def consume(j, s_ref, masked):
        vt = vt_ref[:, pl.ds(pl.multiple_of(j * blk, blk), blk)]
        p = []
        for hv in halves:
            x = s_ref[:, hv]
            if masked:
                key = lax.broadcasted_iota(jnp.int32, x.shape, 0)
                qry = lax.broadcasted_iota(jnp.int32, x.shape, 1) + hv.start
                x = jnp.where(key <= qry, x, -jnp.inf)
            m_prev = m_ref[:, hv]
            m_new = jnp.maximum(m_prev, jnp.max(x, axis=0, keepdims=True))
            alpha = jnp.exp2(m_prev - m_new)
            e = jnp.exp2(x - m_new)
            l_ref[:, hv] = alpha * l_ref[:, hv] + jnp.sum(e, axis=0, keepdims=True)
            m_ref[:, hv] = m_new
            acc_ref[:, hv] = alpha * acc_ref[:, hv]
            p.append(e.astype(BF16))
        for e, hv in zip(p, halves):
            acc_ref[:, hv] += jnp.dot(vt, e, preferred_element_type=F32)

    scores(0, s0_ref)

    def pair(p, carry):
        j = 2 * p
        scores(j + 1, s1_ref)
        consume(j, s0_ref, False)
        scores(j + 2, s0_ref)
        consume(j + 1, s1_ref, False)
        return carry

    lax.fori_loop(0, iq // 2, pair, 0)

    @pl.when(iq % 2 == 0)
    def _():
        consume(iq, s0_ref, True)

    @pl.when(iq % 2 == 1)
    def _():
        scores(iq, s1_ref)
        consume(iq - 1, s0_ref, False)
        consume(iq, s1_ref, True)

    o_ref[...] = (acc_ref[...] / l_ref[...]).T.astype(o_ref.dtype)


def _fox_prompt(q, k, v, c, *, batch, length, blk):
    nq = length // blk
    qspec = pl.BlockSpec((blk, HEAD_DIM), lambda b, h, i: (b * nq + i, h))
    kvspec = pl.BlockSpec((length, HEAD_DIM), lambda b, h, i: (b, h))
    return pl.pallas_call(
        functools.partial(_attn_prompt_kernel, blk=blk),
        grid=(batch, N_HEADS, nq),
        in_specs=[qspec, kvspec, kvspec,
                  pl.BlockSpec((None,) + c.shape[1:], lambda b, h, i: (b, 0, 0))],
        out_specs=qspec,
        out_shape=jax.ShapeDtypeStruct((batch * length, WIDTH), BF16),
        scratch_shapes=[pltpu.VMEM((HEAD_DIM, length), BF16),
                        pltpu.VMEM((length, LANES), F32),
                        pltpu.VMEM((blk, blk), F32), pltpu.VMEM((blk, blk), F32),
                        pltpu.VMEM((1, blk), F32), pltpu.VMEM((1, blk), F32),
                        pltpu.VMEM((HEAD_DIM, blk), F32)],
        compiler_params=_params("parallel", "parallel", "arbitrary"),
        name="fox_prompt",
    )(q, k, v, c)


def _attend_heads(score_fn, value_fn, m_ref, l_ref, acc_ref):
    s = [score_fn(h) for h in range(N_HEADS)]
    stats = []
    for h in range(N_HEADS):
        m_prev = m_ref[h]
        m_new = jnp.maximum(m_prev, jnp.max(s[h], axis=-1, keepdims=True))
        stats.append((jnp.exp2(m_prev - m_new), m_new))
    p = [jnp.exp2(s[h] - stats[h][1]) for h in range(N_HEADS)]
    for h in range(N_HEADS):
        alpha, m_new = stats[h]
        l_ref[h] = alpha * l_ref[h] + jnp.sum(p[h], axis=-1, keepdims=True)
        m_ref[h] = m_new
        acc_ref[:, _head(h)] = alpha * acc_ref[:, _head(h)] + jnp.dot(
            p[h].astype(BF16), value_fn(h), preferred_element_type=F32)


def _attn_sample_kernel(q_ref, kc_ref, vc_ref, bc_ref, kn_ref, vn_ref, bn_ref, o_ref,
                        m_ref, l_ref, acc_ref):
    j = pl.program_id(1)
    tk = bc_ref.shape[1]

    @pl.when(j == 0)
    def _():
        m_ref[...] = jnp.full(m_ref.shape, -jnp.inf, F32)
        l_ref[...] = jnp.zeros(l_ref.shape, F32)
        acc_ref[...] = jnp.zeros(acc_ref.shape, F32)

    _attend_heads(
        lambda h: lax.dot_general(q_ref[:, _head(h)], kc_ref[_head_rows(h, tk), :].astype(BF16),
                                  NT_DIMS, preferred_element_type=F32) + bc_ref[h:h + 1, :],
        lambda h: vc_ref[_head_rows(h, tk), :].astype(BF16), m_ref, l_ref, acc_ref)

    @pl.when(j == pl.num_programs(1) - 1)
    def _():
        tq = q_ref.shape[0]
        causal = (lax.broadcasted_iota(jnp.int32, (tq, tq), 1)
                  <= lax.broadcasted_iota(jnp.int32, (tq, tq), 0))
        _attend_heads(
            lambda h: jnp.where(causal,
                                lax.dot_general(q_ref[:, _head(h)], kn_ref[:, _head(h)], NT_DIMS,
                                                preferred_element_type=F32) - bn_ref[h:h + 1, :],
                                -jnp.inf),
            lambda h: vn_ref[:, _head(h)], m_ref, l_ref, acc_ref)
        for h in range(N_HEADS):
            o_ref[:, _head(h)] = (acc_ref[:, _head(h)] / l_ref[h]).astype(o_ref.dtype)


def _fox_sample(q, k_new, v_new, c_new, cache_k, cache_v, cache_bias, *, row0, tk):
    B, H, P = cache_bias.shape
    Ls = c_new.shape[2]
    base = row0 // Ls
    new = pl.BlockSpec((Ls, WIDTH), lambda b, j: (base + b, 0))
    cache = pl.BlockSpec((None, tk * H, HEAD_DIM), lambda b, j: (b, j, 0))
    return pl.pallas_call(
        _attn_sample_kernel,
        grid=(B, P // tk),
        in_specs=[new, cache, cache,
                  pl.BlockSpec((None, N_HEADS, tk), lambda b, j: (b, 0, j)),
                  new, new,
                  pl.BlockSpec((None, N_HEADS, Ls), lambda b, j: (b, 0, 0))],
        out_specs=pl.BlockSpec((Ls, WIDTH), lambda b, j: (b, 0)),
        out_shape=jax.ShapeDtypeStruct((B * Ls, WIDTH), BF16),
        scratch_shapes=[pltpu.VMEM((N_HEADS, Ls, 1), F32), pltpu.VMEM((N_HEADS, Ls, 1), F32),
                        pltpu.VMEM((Ls, WIDTH), F32)],
        compiler_params=_params("parallel", "arbitrary"),
        name="fox_sample",
    )(q, cache_k, cache_v, cache_bias, k_new, v_new, c_new)


MERGE_COLS = 512


def _merge_kernel(xp_ref, xs_ref, oap_ref, oas_ref, obp_ref, obs_ref, g_ref,
                  wpa_ref, wpb_ref, wo_ref, o_ref, mg_ref, *, n_first):
    first = pl.program_id(0) < n_first
    D = xp_ref.shape[1]
    oa = jnp.where(first, oap_ref[...], oas_ref[...])
    ob = jnp.where(first, obp_ref[...], obs_ref[...])
    cols = min(MERGE_COLS, D)
    for n0 in range(0, D, cols):
        ns = slice(n0, n0 + cols)
        pa = jnp.dot(oa, wpa_ref[:, ns], preferred_element_type=F32)
        pb = jnp.dot(ob, wpb_ref[:, ns], preferred_element_type=F32)
        ga = g_ref[:, ns].astype(F32)
        gb = g_ref[:, D + n0:D + n0 + cols].astype(F32)
        mg_ref[:, ns] = (ga * pa + gb * pb).astype(BF16)
    x = jnp.where(first, xp_ref[...], xs_ref[...])
    o_ref[...] = x + jnp.dot(mg_ref[...], wo_ref[...], preferred_element_type=F32)


def _merge_out(xp, xs, oa_p, oa_s, ob_p, ob_s, g, w_pa, w_pb, w_o, *, tm):
    D = xp.shape[1]
    T = xp.shape[0] + xs.shape[0]
    n_first = xp.shape[0] // tm
    fixed = lambda i: (0, 0)
    return pl.pallas_call(
        functools.partial(_merge_kernel, n_first=n_first),
        grid=(T // tm,),
        in_specs=[*_two_streams(tm, D, n_first), *_two_streams(tm, WIDTH, n_first),
                  *_two_streams(tm, WIDTH, n_first),
                  pl.BlockSpec((tm, 2 * D), lambda i: (i, 0)),
                  *[pl.BlockSpec(w.shape, fixed, pipeline_mode=pl.Buffered(1))
                    for w in (w_pa, w_pb, w_o)]],
        out_specs=pl.BlockSpec((tm, D), lambda i: (i, 0)),
        out_shape=jax.ShapeDtypeStruct((T, D), F32),
        scratch_shapes=[pltpu.VMEM((tm, D), BF16)],
        compiler_params=_params("parallel"),
        name="merge_out",
    )(xp, xs, oa_p, oa_s, ob_p, ob_s, g, w_pa, w_pb, w_o)


def _ffn_kernel(x_ref, g2_ref, w1_ref, w2_ref, gf_ref, yp_ref, ys_ref, h_ref, acc_ref, *, n_first):
    i, f = pl.program_id(0), pl.program_id(1)

    @pl.when(f == 0)
    def _():
        h_ref[...] = _normed(x_ref[...], g2_ref)
        acc_ref[...] = x_ref[...]

    u = jnp.maximum(jnp.dot(h_ref[...], w1_ref[...], preferred_element_type=F32), 0.0)
    acc_ref[...] += jnp.dot((u * u).astype(BF16), w2_ref[...], preferred_element_type=F32)

    @pl.when(f == pl.num_programs(1) - 1)
    def _():
        x = acc_ref[...]
        ms = jnp.mean(x * x, axis=-1, keepdims=True)
        y = x * lax.rsqrt(ms + EPS) * gf_ref[...]

        @pl.when(i < n_first)
        def _():
            yp_ref[...] = y

        @pl.when(i >= n_first)
        def _():
            ys_ref[...] = y


def _ffn(x, g2, w1, w2, gf, *, rows_first, tm, tf):
    T, D = x.shape
    F = w1.shape[1]
    n_first = rows_first // tm
    vec = pl.BlockSpec((1, D), lambda i, f: (0, 0))
    return pl.pallas_call(
        functools.partial(_ffn_kernel, n_first=n_first),
        grid=(T // tm, F // tf),
        in_specs=[pl.BlockSpec((tm, D), lambda i, f: (i, 0)), vec,
                  pl.BlockSpec((D, tf), lambda i, f: (0, f)),
                  pl.BlockSpec((tf, D), lambda i, f: (f, 0)), vec],
        out_specs=list(_two_streams(tm, D, n_first)),
        out_shape=[jax.ShapeDtypeStruct((rows_first, D), F32),
                   jax.ShapeDtypeStruct((T - rows_first, D), F32)],
        scratch_shapes=[pltpu.VMEM((tm, D), BF16), pltpu.VMEM((tm, D), F32)],
        compiler_params=_params("arbitrary", "arbitrary"),
        name="ffn",
    )(x, g2, w1, w2, gf)


def _tile(n, pref):
    t = min(pref, n)
    while n % t:
        t //= 2
    return t


def kernel(x_prompt, x_sample, cache_fox_k, cache_fox_v, cache_fox_logf, state_hgrn, norm1, w_in,
           b_fox_f, lb_logits, gnorm_a, w_pa, w_pb, w_o, norm2, w1, w2, norm_f):
    B, L, D = x_prompt.shape
    Bs, Ls, _ = x_sample.shape
    depth, _, P, H, DH = cache_fox_k.shape
    assert depth == 1 and H == N_HEADS and DH == HEAD_DIM
    Tp, Ts = B * L, Bs * Ls
    T = Tp + Ts
    W = WIDTH

    xp, xs = x_prompt.reshape(Tp, D), x_sample.reshape(Ts, D)
    w_in0 = w_in[0]
    w_a = w_in0[:, :4 * W].astype(BF16)
    w_b = w_in0[:, 4 * W:7 * W].astype(BF16)
    w_fl_t = w_in0[:, 7 * W:7 * W + H].T.astype(BF16)
    w_g = w_in0[:, 7 * W + H:].astype(BF16)
    g1 = norm1[0].reshape(1, D)

    tm = _tile(math.gcd(Tp, Ts), 512)
    tm1 = _tile(math.gcd(Tp, Ts), 1024)
    a_proj = _norm_proj(xp, xs, g1, w_a, act=None, out_dtype=F32, tm=tm1, tn=W // 2)
    gates = _norm_proj(xp, xs, g1, w_g, act="sigmoid", out_dtype=BF16, tm=tm1, tn=_tile(2 * D, W))
    q_b, k_b, v_b, k_p, k_s, v_p, v_s, fl_t = _norm_proj_fox(xp, xs, g1, w_b, w_fl_t,
                                                              tm=_tile(tm, 512))

    bias = b_fox_f[0].reshape(H, 1)
    lf_p, c_p = _fox_gates(fl_t, bias, col0=0, cols=Tp, seg=L, block=L, pad_rows=LANES)
    lf_s, c_s = _fox_gates(fl_t, bias, col0=Tp, cols=Ts, seg=Ls, block=_tile(Ts, 1024), pad_rows=H)
    cache_bias = _cache_suffix(jnp.transpose(cache_fox_logf[0], (0, 2, 1)))

    zeros_state = jnp.zeros((B, H, DH, DH), F32)
    chunk_p = _tile(L, 64)
    oa_p, s_p = _hgrn2(a_proj, lb_logits, gnorm_a, zeros_state, batch=B, length=L, row0=0,
                       chunk=chunk_p, n_chunks=_tile(L // chunk_p, 4))
    oa_s, s_s = _hgrn2(a_proj, lb_logits, gnorm_a, state_hgrn[0], batch=Bs, length=Ls, row0=Tp,
                       chunk=_tile(Ls, 64), n_chunks=1)

    ob_p = _fox_prompt(q_b, k_b, v_b, c_p, batch=B, length=L, blk=_tile(L, 512))
    c_s3 = c_s.transpose(1, 0, 2).reshape(H, Bs, Ls).transpose(1, 0, 2)
    ob_s = _fox_sample(q_b, k_b, v_b, c_s3, cache_fox_k.reshape(Bs, P * H, DH),
                       cache_fox_v.reshape(Bs, P * H, DH), cache_bias,
                       row0=Tp, tk=_tile(P, 1024))

    tm5 = _tile(tm, 512)
    x1 = _merge_out(xp, xs, oa_p, oa_s, ob_p, ob_s, gates, w_pa[0].astype(BF16),
                    w_pb[0].astype(BF16), w_o[0].astype(BF16), tm=_tile(tm, 256))
    y_p, y_s = _ffn(x1, norm2[0].reshape(1, D), w1[0].astype(BF16), w2[0].astype(BF16),
                    norm_f.reshape(1, D), rows_first=Tp, tm=tm5, tf=_tile(w1.shape[-1], 1024))

    return (y_p.reshape(B, L, D), y_s.reshape(Bs, Ls, D),
            k_p.reshape(1, B, L, H, DH), v_p.reshape(1, B, L, H, DH),
            lf_p.T.reshape(1, B, L, H), s_p[None],
            k_s.reshape(1, Bs, Ls, H, DH), v_s.reshape(1, Bs, Ls, H, DH),
            lf_s.T.reshape(1, Bs, Ls, H), s_s[None])
```

```python
import functools
import math

import jax
import jax.numpy as jnp
from jax import lax
from jax.experimental import pallas as pl
from jax.experimental.pallas import tpu as pltpu

F32 = jnp.float32
BF16 = jnp.bfloat16

EPS = 1e-6
N_HEADS = 8
HEAD_DIM = 128
WIDTH = N_HEADS * HEAD_DIM
LANES = 128
VMEM_LIMIT = 56 * 1024 * 1024
LOG2E = math.log2(math.e)
QK_SCALE = HEAD_DIM ** -0.5 * LOG2E

NT_DIMS = (((1,), (1,)), ((), ()))
TN_DIMS = (((0,), (0,)), ((), ()))


def _params(*sem):
    return pltpu.CompilerParams(dimension_semantics=sem, vmem_limit_bytes=VMEM_LIMIT)


def _sigmoid(x):
    return 1.0 / (1.0 + jnp.exp(-x))


def _head(h):
    return slice(h * HEAD_DIM, (h + 1) * HEAD_DIM)


def _two_streams(rows, width, n_first, n_second=None):
    mode = {"pipeline_mode": pl.Buffered(1)} if n_second == 1 else {}
    return (pl.BlockSpec((rows, width), lambda i, *_: (jnp.minimum(i, n_first - 1), 0)),
            pl.BlockSpec((rows, width), lambda i, *_: (jnp.maximum(i - n_first, 0), 0), **mode))


def _head_rows(h, rows):
    return pl.ds(h, rows, stride=N_HEADS)


def _normed(x, g_ref):
    ms = jnp.mean(x * x, axis=-1, keepdims=True)
    return (x * lax.rsqrt(ms + EPS) * g_ref[...]).astype(BF16)


def _proj_kernel(xp_ref, xs_ref, g_ref, w_ref, o_ref, h_ref, *, act, n_first):
    @pl.when(pl.program_id(1) == 0)
    def _():
        x = jnp.where(pl.program_id(0) < n_first, xp_ref[...], xs_ref[...])
        h_ref[...] = _normed(x, g_ref)

    acc = jnp.dot(h_ref[...], w_ref[...], preferred_element_type=F32)
    if act == "sigmoid":
        acc = _sigmoid(acc)
    o_ref[...] = acc.astype(o_ref.dtype)


def _norm_proj(xp, xs, g, w, *, n_cols, act, out_dtype, tm, tn):
    D = xp.shape[1]
    N = n_cols
    n_first = xp.shape[0] // tm
    T = xp.shape[0] + xs.shape[0]
    return pl.pallas_call(
        functools.partial(_proj_kernel, act=act, n_first=n_first),
        grid=(T // tm, N // tn),
        in_specs=[*_two_streams(tm, D, n_first, xs.shape[0] // tm),
                  pl.BlockSpec((1, D), lambda i, j: (0, 0)),
                  pl.BlockSpec((D, tn), lambda i, j: (0, j))],
        out_specs=pl.BlockSpec((tm, tn), lambda i, j: (i, j)),
        out_shape=jax.ShapeDtypeStruct((T, N), out_dtype),
        scratch_shapes=[pltpu.VMEM((tm, D), BF16)],
        compiler_params=_params("parallel", "arbitrary"),
        name="norm_proj_" + (act or "lin"),
    )(xp, xs, g, w)


def _proj_fox_kernel(xp_ref, xs_ref, g_ref, w_ref, wfl_ref,
                     q_ref, kb_ref, vb_ref, kp_ref, ks_ref, vp_ref, vs_ref, fl_ref, h_ref,
                     *, n_first):
    i, j = pl.program_id(0), pl.program_id(1)

    @pl.when(j == 0)
    def _():
        x = jnp.where(i < n_first, xp_ref[...], xs_ref[...])
        h = _normed(x, g_ref)
        h_ref[...] = h
        fl_ref[...] = lax.dot_general(wfl_ref[...], h, NT_DIMS, preferred_element_type=F32)

    acc = jnp.dot(h_ref[...], w_ref[...], preferred_element_type=F32)

    @pl.when(j == 0)
    def _():
        q_ref[...] = (acc * QK_SCALE).astype(BF16)

    def heads_out(first_ref, second_ref):
        @pl.when(i < n_first)
        def _():
            for h in range(N_HEADS):
                first_ref[_head_rows(h, acc.shape[0]), :] = acc[:, _head(h)]

        @pl.when(i >= n_first)
        def _():
            for h in range(N_HEADS):
                second_ref[_head_rows(h, acc.shape[0]), :] = acc[:, _head(h)]

    @pl.when(j == 1)
    def _():
        kb_ref[...] = acc.astype(BF16)
        heads_out(kp_ref, ks_ref)

    @pl.when(j == 2)
    def _():
        vb_ref[...] = acc.astype(BF16)
        heads_out(vp_ref, vs_ref)


def _norm_proj_fox(xp, xs, g, w, w_fl_t, *, col0, tm):
    D = xp.shape[1]
    Tp, Ts = xp.shape[0], xs.shape[0]
    T = Tp + Ts
    n_first = Tp // tm
    kv_p, kv_s = _two_streams(tm * N_HEADS, HEAD_DIM, n_first)
    kv_shape = lambda rows: jax.ShapeDtypeStruct((rows * N_HEADS, HEAD_DIM), F32)
    tok = pl.BlockSpec((tm, WIDTH), lambda i, j: (i, 0))
    tok_shape = jax.ShapeDtypeStruct((T, WIDTH), BF16)
    return pl.pallas_call(
        functools.partial(_proj_fox_kernel, n_first=n_first),
        grid=(T // tm, 3),
        in_specs=[*_two_streams(tm, D, n_first),
                  pl.BlockSpec((1, D), lambda i, j: (0, 0)),
                  pl.BlockSpec((D, WIDTH), lambda i, j: (0, col0 // WIDTH + j)),
                  pl.BlockSpec((N_HEADS, D), lambda i, j: (0, 0))],
        out_specs=[tok, tok, tok, kv_p, kv_s, kv_p, kv_s,
                   pl.BlockSpec((N_HEADS, tm), lambda i, j: (0, i))],
        out_shape=[tok_shape, tok_shape, tok_shape,
                   kv_shape(Tp), kv_shape(Ts), kv_shape(Tp), kv_shape(Ts),
                   jax.ShapeDtypeStruct((N_HEADS, T), F32)],
        scratch_shapes=[pltpu.VMEM((tm, D), BF16)],
        compiler_params=_params("arbitrary", "arbitrary"),
        name="norm_proj_fox",
    )(xp, xs, g, w, w_fl_t)


def _lane_cumsum(x, seg):
    pos = lax.broadcasted_iota(jnp.int32, x.shape, x.ndim - 1) & (seg - 1)
    shift = 1
    while shift < seg:
        x = x + jnp.where(pos >= shift, pltpu.roll(x, shift, x.ndim - 1), 0.0)
        shift *= 2
    return x


def _gate_kernel(fl_ref, bias_ref, lf_ref, c_ref, *, seg):
    z = fl_ref[...] + bias_ref[...]
    lf = jnp.minimum(z, 0.0) - jnp.log(1.0 + jnp.exp(-jnp.abs(z)))
    lf_ref[...] = lf
    c = _lane_cumsum(lf, seg) * LOG2E
    c_ref[...] = jnp.zeros(c_ref.shape, F32)
    c_ref[0:c.shape[0], :] = c


def _fox_gates(fl_t, bias, *, col0, cols, seg, block, pad_rows):
    H = fl_t.shape[0]
    nb = cols // block
    base = col0 // block
    return pl.pallas_call(
        functools.partial(_gate_kernel, seg=seg),
        grid=(nb,),
        in_specs=[pl.BlockSpec((H, block), lambda i: (0, base + i)),
                  pl.BlockSpec((H, 1), lambda i: (0, 0))],
        out_specs=[pl.BlockSpec((H, block), lambda i: (0, i)),
                   pl.BlockSpec((None, pad_rows, block), lambda i: (i, 0, 0))],
        out_shape=[jax.ShapeDtypeStruct((H, cols), F32),
                   jax.ShapeDtypeStruct((nb, pad_rows, block), F32)],
        compiler_params=_params("parallel"),
        name="fox_gates",
    )(fl_t, bias)


def _suffix_kernel(lf_ref, o_ref):
    lf = lf_ref[...]
    c = _lane_cumsum(lf, lf.shape[-1])
    o_ref[...] = (c[:, -1:] - c) * LOG2E


def _cache_suffix(clf_t):
    B, H, P = clf_t.shape
    spec = pl.BlockSpec((None, H, P), lambda b: (b, 0, 0))
    return pl.pallas_call(
        _suffix_kernel,
        grid=(B,),
        in_specs=[spec],
        out_specs=spec,
        out_shape=jax.ShapeDtypeStruct((B, H, P), F32),
        compiler_params=_params("parallel"),
        name="fox_cache_suffix",
    )(clf_t)


SUB = 16


def _gla_kernel(aq_ref, af_ref, ai_ref, ag_ref, lbl_ref, gn_ref, s0_ref, tri_ref,
                o_ref, s_out_ref, st_ref, *, chunk, n_chunks):
    t = pl.program_id(1)

    @pl.when(t == 0)
    def _():
        for h in range(N_HEADS):
            st_ref[h] = s0_ref[h].T

    lbl = lbl_ref[...]
    e = jnp.exp(lbl - jnp.max(lbl, axis=0, keepdims=True))
    lb = e[0:1] / jnp.sum(e, axis=0, keepdims=True)
    gn = gn_ref[...]
    tri = tri_ref[...]
    n_sub = chunk // SUB

    def one_chunk(ci, carry):
        r = pl.ds(pl.multiple_of(ci * chunk, chunk), chunk)
        fa = lb + (1.0 - lb) * _sigmoid(af_ref[r, :])
        lf = jnp.log(fa)
        ka = 1.0 - fa
        aq = aq_ref[r, :]
        qa = aq * _sigmoid(aq)
        v = ai_ref[r, :].astype(BF16)
        ag = ag_ref[r, :]
        gate = ag * _sigmoid(ag)

        hi = lf.astype(BF16)
        r1 = lf - hi.astype(F32)
        mid = r1.astype(BF16)
        lo = (r1 - mid.astype(F32)).astype(BF16)
        b = (jnp.dot(tri, hi, preferred_element_type=F32)
             + jnp.dot(tri, mid, preferred_element_type=F32)
             + jnp.dot(tri, lo, preferred_element_type=F32))
        b_end = b[chunk - 1:chunk, :]

        q_in = (qa * jnp.exp(b)).astype(BF16)
        k_out = (ka * jnp.exp(b_end - b)).astype(BF16)
        decay = jnp.exp(b_end)
        qp, kp = [], []
        for i in range(n_sub):
            r0, n = i * SUB, (i + 1) * SUB
            m_i = b[r0 + SUB // 2:r0 + SUB // 2 + 1, :]
            qp.append((qa[r0:n] * jnp.exp(b[r0:n] - m_i)).astype(BF16))
            kp.append((ka[:n] * jnp.exp(m_i - b[:n])).astype(BF16))

        st = [st_ref[h] for h in range(N_HEADS)]
        inter = [lax.dot_general(q_in[:, _head(h)], st[h].astype(BF16), NT_DIMS,
                                 preferred_element_type=F32) for h in range(N_HEADS)]
        att = [[lax.dot_general(qp[i][:, _head(h)], kp[i][:, _head(h)], NT_DIMS,
                                preferred_element_type=F32) for h in range(N_HEADS)]
               for i in range(n_sub)]
        for h in range(N_HEADS):
            st_ref[h] = st[h] * decay[:, _head(h)] + lax.dot_general(
                v[:, _head(h)], k_out[:, _head(h)], TN_DIMS, preferred_element_type=F32)
        for i in range(n_sub):
            r0, n = i * SUB, (i + 1) * SUB
            causal = (lax.broadcasted_iota(jnp.int32, (SUB, n), 1)
                      <= lax.broadcasted_iota(jnp.int32, (SUB, n), 0) + r0)
            att[i] = [jnp.where(causal, a, 0.0).astype(BF16) for a in att[i]]
        for h in range(N_HEADS):
            o = jnp.concatenate(
                [inter[h][i * SUB:(i + 1) * SUB]
                 + jnp.dot(att[i][h], v[:(i + 1) * SUB, _head(h)], preferred_element_type=F32)
                 for i in range(n_sub)], axis=0)
            ms = jnp.mean(o * o, axis=-1, keepdims=True)
            o_ref[r, _head(h)] = (o * lax.rsqrt(ms + EPS) * gn * gate[:, _head(h)]).astype(o_ref.dtype)
        return carry

    lax.fori_loop(0, n_chunks, one_chunk, 0)

    @pl.when(t == pl.num_programs(1) - 1)
    def _():
        for h in range(N_HEADS):
            s_out_ref[h] = st_ref[h].T


def _hgrn2(a_proj, lb_logits, gnorm, s0, *, batch, length, row0, chunk, n_chunks):
    step = chunk * n_chunks
    nt = length // step
    base = row0 // step

    def col(g):
        return pl.BlockSpec((step, WIDTH), lambda b, t: (base + b * nt + t, g))

    st_spec = pl.BlockSpec((None, N_HEADS, HEAD_DIM, HEAD_DIM), lambda b, t: (b, 0, 0, 0))
    tri = jnp.tril(jnp.ones((chunk, chunk), BF16))
    return pl.pallas_call(
        functools.partial(_gla_kernel, chunk=chunk, n_chunks=n_chunks),
        grid=(batch, nt),
        in_specs=[col(0), col(1), col(2), col(3),
                  pl.BlockSpec(lb_logits.shape, lambda b, t: (0, 0)),
                  pl.BlockSpec((1, HEAD_DIM), lambda b, t: (0, 0)),
                  st_spec,
                  pl.BlockSpec((chunk, chunk), lambda b, t: (0, 0))],
        out_specs=[pl.BlockSpec((step, WIDTH), lambda b, t: (b * nt + t, 0)), st_spec],
        out_shape=[jax.ShapeDtypeStruct((batch * length, WIDTH), BF16),
                   jax.ShapeDtypeStruct((batch, N_HEADS, HEAD_DIM, HEAD_DIM), F32)],
        scratch_shapes=[pltpu.VMEM((N_HEADS, HEAD_DIM, HEAD_DIM), F32)],
        compiler_params=_params("parallel", "arbitrary"),
        name="hgrn2",
    )(a_proj, a_proj, a_proj, a_proj, lb_logits, gnorm, s0, tri)


def _attn_prompt_kernel(q_ref, k_ref, v_ref, c_ref, o_ref,
                        vt_ref, cb_ref, s0_ref, s1_ref, m_ref, l_ref, acc_ref, *, blk):
    h, iq = pl.program_id(1), pl.program_id(2)
    length = k_ref.shape[0]
    halves = (slice(0, blk // 2), slice(blk // 2, blk))

    @pl.when(iq == 0)
    def _():
        for r0 in range(0, length, blk):
            rs = slice(r0, r0 + blk)
            vt_ref[:, rs] = v_ref[rs, :].astype(F32).T.astype(BF16)
            c_cols = c_ref[:, rs].T
            onehot = lax.broadcasted_iota(jnp.int32, c_cols.shape, 1) == h
            cb_ref[rs, :] = jnp.broadcast_to(
                jnp.sum(jnp.where(onehot, c_cols, 0.0), axis=1, keepdims=True), (blk, LANES))

    q = q_ref[...]
    m_ref[...] = jnp.full(m_ref.shape, -jnp.inf, F32)
    l_ref[...] = jnp.zeros(l_ref.shape, F32)
    acc_ref[...] = jnp.zeros(acc_ref.shape, F32)

    def scores(j, s_ref):
        ks = pl.ds(pl.multiple_of(j * blk, blk), blk)
        kb = k_ref[ks, :]
        cb = cb_ref[ks, :]
        for hv in halves:
            x = lax.dot_general(kb, q[hv, :], NT_DIMS, preferred_element_type=F32)
            s_ref[:, hv] = x - jnp.concatenate([cb] * (x.shape[1] // LANES), axis=1)

    def consume(j, s_ref, masked):
        vt = vt_ref[:, pl.ds(pl.multiple_of(j * blk, blk), blk)]
        p = []
        for hv in halves:
            x = s_ref[:, hv]
            if masked:
                key = lax.broadcasted_iota(jnp.int32, x.shape, 0)
                qry = lax.broadcasted_iota(jnp.int32, x.shape, 1) + hv.start
                x = jnp.where(key <= qry, x, -jnp.inf)
            m_prev = m_ref[:, hv]
            m_new = jnp.maximum(m_prev, jnp.max(x, axis=0, keepdims=True))
            alpha = jnp.exp2(m_prev - m_new)
            e = jnp.exp2(x - m_new)
            l_ref[:, hv] = alpha * l_ref[:, hv] + jnp.sum(e, axis=0, keepdims=True)
            m_ref[:, hv] = m_new
            acc_ref[:, hv] = alpha * acc_ref[:, hv]
            p.append(e.astype(BF16))
        for e, hv in zip(p, halves):
            acc_ref[:, hv] += jnp.dot(vt, e, preferred_element_type=F32)

    scores(0, s0_ref)

    def pair(p, carry):
        j = 2 * p
        scores(j + 1, s1_ref)
        consume(j, s0_ref, False)
        scores(j + 2, s0_ref)
        consume(j + 1, s1_ref, False)
        return carry

    lax.fori_loop(0, iq // 2, pair, 0)

    @pl.when(iq % 2 == 0)
    def _():
        consume(iq, s0_ref, True)

    @pl.when(iq % 2 == 1)
    def _():
        scores(iq, s1_ref)
        consume(iq - 1, s0_ref, False)
        consume(iq, s1_ref, True)

    o_ref[...] = (acc_ref[...] / l_ref[...]).T.astype(o_ref.dtype)


def _fox_prompt(q, k, v, c, *, batch, length, blk):
    nq = length // blk
    qspec = pl.BlockSpec((blk, HEAD_DIM), lambda b, h, i: (b * nq + i, h))
    kvspec = pl.BlockSpec((length, HEAD_DIM), lambda b, h, i: (b, h))
    return pl.pallas_call(
        functools.partial(_attn_prompt_kernel, blk=blk),
        grid=(batch, N_HEADS, nq),
        in_specs=[qspec, kvspec, kvspec,
                  pl.BlockSpec((None,) + c.shape[1:], lambda b, h, i: (b, 0, 0))],
        out_specs=qspec,
        out_shape=jax.ShapeDtypeStruct((batch * length, WIDTH), BF16),
        scratch_shapes=[pltpu.VMEM((HEAD_DIM, length), BF16),
                        pltpu.VMEM((length, LANES), F32),
                        pltpu.VMEM((blk, blk), F32), pltpu.VMEM((blk, blk), F32),
                        pltpu.VMEM((1, blk), F32), pltpu.VMEM((1, blk), F32),
                        pltpu.VMEM((HEAD_DIM, blk), F32)],
        compiler_params=_params("parallel", "parallel", "arbitrary"),
        name="fox_prompt",
    )(q, k, v, c)


def _attend_heads(score_fn, value_fn, m_ref, l_ref, acc_ref):
    s = [score_fn(h) for h in range(N_HEADS)]
    stats = []
    for h in range(N_HEADS):
        m_prev = m_ref[h]
        m_new = jnp.maximum(m_prev, jnp.max(s[h], axis=-1, keepdims=True))
        stats.append((jnp.exp2(m_prev - m_new), m_new))
    p = [jnp.exp2(s[h] - stats[h][1]) for h in range(N_HEADS)]
    for h in range(N_HEADS):
        alpha, m_new = stats[h]
        l_ref[h] = alpha * l_ref[h] + jnp.sum(p[h], axis=-1, keepdims=True)
        m_ref[h] = m_new
        acc_ref[:, _head(h)] = alpha * acc_ref[:, _head(h)] + jnp.dot(
            p[h].astype(BF16), value_fn(h), preferred_element_type=F32)


def _attn_sample_kernel(q_ref, kc_hbm, vc_hbm, bc_ref, kn_ref, vn_ref, bn_ref, o_ref,
                        kbuf_ref, vbuf_ref, sem_ref, m_ref, l_ref, acc_ref):
    b, j = pl.program_id(0), pl.program_id(1)
    nj = pl.num_programs(1)
    tk = bc_ref.shape[1]
    step = b * nj + j
    slot = step % 2

    def cache_copies(stream, blk, into):
        return [pltpu.make_async_copy(hbm.at[stream, pl.ds(blk * tk, tk), h, :],
                                      buf.at[into, h], sem_ref.at[t, into])
                for t, (hbm, buf) in enumerate(((kc_hbm, kbuf_ref), (vc_hbm, vbuf_ref)))
                for h in range(N_HEADS)]

    @pl.when(step == 0)
    def _():
        for cp in cache_copies(b, j, slot):
            cp.start()

    @pl.when(step + 1 < pl.num_programs(0) * nj)
    def _():
        for cp in cache_copies((step + 1) // nj, (step + 1) % nj, 1 - slot):
            cp.start()

    @pl.when(j == 0)
    def _():
        m_ref[...] = jnp.full(m_ref.shape, -jnp.inf, F32)
        l_ref[...] = jnp.zeros(l_ref.shape, F32)
        acc_ref[...] = jnp.zeros(acc_ref.shape, F32)

    for cp in cache_copies(b, j, slot):
        cp.wait()
    _attend_heads(
        lambda h: lax.dot_general(q_ref[:, _head(h)], kbuf_ref[slot, h].astype(BF16),
                                  NT_DIMS, preferred_element_type=F32) + bc_ref[h:h + 1, :],
        lambda h: vbuf_ref[slot, h].astype(BF16), m_ref, l_ref, acc_ref)

    @pl.when(j == pl.num_programs(1) - 1)
    def _():
        tq = q_ref.shape[0]
        causal = (lax.broadcasted_iota(jnp.int32, (tq, tq), 1)
                  <= lax.broadcasted_iota(jnp.int32, (tq, tq), 0))
        _attend_heads(
            lambda h: jnp.where(causal,
                                lax.dot_general(q_ref[:, _head(h)], kn_ref[:, _head(h)], NT_DIMS,
                                                preferred_element_type=F32) - bn_ref[h:h + 1, :],
                                -jnp.inf),
            lambda h: vn_ref[:, _head(h)], m_ref, l_ref, acc_ref)
        for h in range(N_HEADS):
            o_ref[:, _head(h)] = (acc_ref[:, _head(h)] / l_ref[h]).astype(o_ref.dtype)


def _fox_sample(q, k_new, v_new, c_new, cache_k, cache_v, cache_bias, *, row0, tk):
    B, H, P = cache_bias.shape
    Ls = c_new.shape[2]
    base = row0 // Ls
    new = pl.BlockSpec((Ls, WIDTH), lambda b, j: (base + b, 0))
    cache = pl.BlockSpec(memory_space=pl.ANY)
    cache_buf = pltpu.VMEM((2, H, tk, HEAD_DIM), F32)
    return pl.pallas_call(
        _attn_sample_kernel,
        grid=(B, P // tk),
        in_specs=[new, cache, cache,
                  pl.BlockSpec((None, N_HEADS, tk), lambda b, j: (b, 0, j)),
                  new, new,
                  pl.BlockSpec((None, N_HEADS, Ls), lambda b, j: (b, 0, 0))],
        out_specs=pl.BlockSpec((Ls, WIDTH), lambda b, j: (b, 0)),
        out_shape=jax.ShapeDtypeStruct((B * Ls, WIDTH), BF16),
        scratch_shapes=[cache_buf, cache_buf, pltpu.SemaphoreType.DMA((2, 2)),
                        pltpu.VMEM((N_HEADS, Ls, 1), F32), pltpu.VMEM((N_HEADS, Ls, 1), F32),
                        pltpu.VMEM((Ls, WIDTH), F32)],
        compiler_params=_params("arbitrary", "arbitrary"),
        name="fox_sample",
    )(q, cache_k, cache_v, cache_bias, k_new, v_new, c_new)


MERGE_COLS = 512


def _merge_kernel(xp_ref, xs_ref, oap_ref, oas_ref, obp_ref, obs_ref, g_ref,
                  wpa_ref, wpb_ref, wo_ref, o_ref, mg_ref, *, n_first):
    first = pl.program_id(0) < n_first
    D = xp_ref.shape[1]
    oa = jnp.where(first, oap_ref[...], oas_ref[...])
    ob = jnp.where(first, obp_ref[...], obs_ref[...])
    cols = min(MERGE_COLS, D)
    for n0 in range(0, D, cols):
        ns = slice(n0, n0 + cols)
        pa = jnp.dot(oa, wpa_ref[:, ns], preferred_element_type=F32)
        pb = jnp.dot(ob, wpb_ref[:, ns], preferred_element_type=F32)
        ga = g_ref[:, ns].astype(F32)
        gb = g_ref[:, D + n0:D + n0 + cols].astype(F32)
        mg_ref[:, ns] = (ga * pa + gb * pb).astype(BF16)
    x = jnp.where(first, xp_ref[...], xs_ref[...])
    o_ref[...] = x + jnp.dot(mg_ref[...], wo_ref[...], preferred_element_type=F32)


def _merge_out(xp, xs, oa_p, oa_s, ob_p, ob_s, g, w_pa, w_pb, w_o, *, tm):
    D = xp.shape[1]
    T = xp.shape[0] + xs.shape[0]
    n_first = xp.shape[0] // tm
    fixed = lambda i: (0, 0)
    return pl.pallas_call(
        functools.partial(_merge_kernel, n_first=n_first),
        grid=(T // tm,),
        in_specs=[*_two_streams(tm, D, n_first), *_two_streams(tm, WIDTH, n_first),
                  *_two_streams(tm, WIDTH, n_first),
                  pl.BlockSpec((tm, 2 * D), lambda i: (i, 0)),
                  *[pl.BlockSpec(w.shape, fixed, pipeline_mode=pl.Buffered(1))
                    for w in (w_pa, w_pb, w_o)]],
        out_specs=pl.BlockSpec((tm, D), lambda i: (i, 0)),
        out_shape=jax.ShapeDtypeStruct((T, D), F32),
        scratch_shapes=[pltpu.VMEM((tm, D), BF16)],
        compiler_params=_params("parallel"),
        name="merge_out",
    )(xp, xs, oa_p, oa_s, ob_p, ob_s, g, w_pa, w_pb, w_o)


def _ffn_kernel(x_ref, g2_ref, w1_ref, w2_ref, gf_ref, yp_ref, ys_ref, h_ref, acc_ref, *, n_first):
    i, f = pl.program_id(0), pl.program_id(1)

    @pl.when(f == 0)
    def _():
        h_ref[...] = _normed(x_ref[...], g2_ref)
        acc_ref[...] = x_ref[...]

    u = jnp.maximum(jnp.dot(h_ref[...], w1_ref[...], preferred_element_type=F32), 0.0)
    acc_ref[...] += jnp.dot((u * u).astype(BF16), w2_ref[...], preferred_element_type=F32)

    @pl.when(f == pl.num_programs(1) - 1)
    def _():
        x = acc_ref[...]
        ms = jnp.mean(x * x, axis=-1, keepdims=True)
        y = x * lax.rsqrt(ms + EPS) * gf_ref[...]

        @pl.when(i < n_first)
        def _():
            yp_ref[...] = y

        @pl.when(i >= n_first)
        def _():
            ys_ref[...] = y


def _ffn(x, g2, w1, w2, gf, *, rows_first, tm, tf):
    T, D = x.shape
    F = w1.shape[1]
    n_first = rows_first // tm
    vec = pl.BlockSpec((1, D), lambda i, f: (0, 0))
    return pl.pallas_call(
        functools.partial(_ffn_kernel, n_first=n_first),
        grid=(T // tm, F // tf),
        in_specs=[pl.BlockSpec((tm, D), lambda i, f: (i, 0)), vec,
                  pl.BlockSpec((D, tf), lambda i, f: (0, f)),
                  pl.BlockSpec((tf, D), lambda i, f: (f, 0)), vec],
        out_specs=list(_two_streams(tm, D, n_first)),
        out_shape=[jax.ShapeDtypeStruct((rows_first, D), F32),
                   jax.ShapeDtypeStruct((T - rows_first, D), F32)],
        scratch_shapes=[pltpu.VMEM((tm, D), BF16), pltpu.VMEM((tm, D), F32)],
        compiler_params=_params("arbitrary", "arbitrary"),
        name="ffn",
    )(x, g2, w1, w2, gf)


def _tile(n, pref):
    t = min(pref, n)
    while n % t:
        t //= 2
    return t


def kernel(x_prompt, x_sample, cache_fox_k, cache_fox_v, cache_fox_logf, state_hgrn, norm1, w_in,
           b_fox_f, lb_logits, gnorm_a, w_pa, w_pb, w_o, norm2, w1, w2, norm_f):
    B, L, D = x_prompt.shape
    Bs, Ls, _ = x_sample.shape
    depth, _, P, H, DH = cache_fox_k.shape
    assert depth == 1 and H == N_HEADS and DH == HEAD_DIM
    Tp, Ts = B * L, Bs * Ls
    T = Tp + Ts
    W = WIDTH

    xp, xs = x_prompt.reshape(Tp, D), x_sample.reshape(Ts, D)
    w_inb = w_in[0].astype(BF16)
    w_fl_t = w_inb[:, 7 * W:7 * W + H].T
    w_g = w_inb[:, 7 * W + H:]
    g1 = norm1[0].reshape(1, D)

    tm = _tile(math.gcd(Tp, Ts), 512)
    tm1 = _tile(math.gcd(Tp, Ts), 1024)
    a_proj = _norm_proj(xp, xs, g1, w_inb, n_cols=4 * W, act=None, out_dtype=F32, tm=tm1,
                        tn=W // 2)
    gates = _norm_proj(xp, xs, g1, w_g, n_cols=2 * D, act="sigmoid", out_dtype=BF16, tm=tm1,
                       tn=_tile(2 * D, W))
    q_b, k_b, v_b, k_p, k_s, v_p, v_s, fl_t = _norm_proj_fox(xp, xs, g1, w_inb, w_fl_t, col0=4 * W,
                                                              tm=_tile(tm, 512))

    bias = b_fox_f[0].reshape(H, 1)
    lf_p, c_p = _fox_gates(fl_t, bias, col0=0, cols=Tp, seg=L, block=L, pad_rows=LANES)
    lf_s, c_s = _fox_gates(fl_t, bias, col0=Tp, cols=Ts, seg=Ls, block=_tile(Ts, 1024), pad_rows=H)
    cache_bias = _cache_suffix(jnp.transpose(cache_fox_logf[0], (0, 2, 1)))

    zeros_state = jnp.zeros((B, H, DH, DH), F32)
    chunk_p = _tile(L, 64)
    oa_p, s_p = _hgrn2(a_proj, lb_logits, gnorm_a, zeros_state, batch=B, length=L, row0=0,
                       chunk=chunk_p, n_chunks=_tile(L // chunk_p, 4))
    oa_s, s_s = _hgrn2(a_proj, lb_logits, gnorm_a, state_hgrn[0], batch=Bs, length=Ls, row0=Tp,
                       chunk=_tile(Ls, 64), n_chunks=1)

    ob_p = _fox_prompt(q_b, k_b, v_b, c_p, batch=B, length=L, blk=_tile(L, 512))
    c_s3 = c_s.transpose(1, 0, 2).reshape(H, Bs, Ls).transpose(1, 0, 2)
    ob_s = _fox_sample(q_b, k_b, v_b, c_s3, cache_fox_k[0], cache_fox_v[0], cache_bias,
                       row0=Tp, tk=_tile(P, 1024))

    tm5 = _tile(tm, 512)
    x1 = _merge_out(xp, xs, oa_p, oa_s, ob_p, ob_s, gates, w_pa[0].astype(BF16),
                    w_pb[0].astype(BF16), w_o[0].astype(BF16), tm=_tile(tm, 256))
    y_p, y_s = _ffn(x1, norm2[0].reshape(1, D), w1[0].astype(BF16), w2[0].astype(BF16),
                    norm_f.reshape(1, D), rows_first=Tp, tm=tm5, tf=_tile(w1.shape[-1], 1024))

    return (y_p.reshape(B, L, D), y_s.reshape(Bs, Ls, D),
            k_p.reshape(1, B, L, H, DH), v_p.reshape(1, B, L, H, DH),
            lf_p.T.reshape(1, B, L, H), s_p[None],
            k_s.reshape(1, Bs, Ls, H, DH), v_s.reshape(1, Bs, Ls, H, DH),
            lf_s.T.reshape(1, Bs, Ls, H), s_s[None])
```

```python
import functools
import math

import jax
import jax.numpy as jnp
from jax import lax
from jax.experimental import pallas as pl
from jax.experimental.pallas import tpu as pltpu

F32 = jnp.float32
BF16 = jnp.bfloat16

EPS = 1e-6
N_HEADS = 8
HEAD_DIM = 128
WIDTH = N_HEADS * HEAD_DIM
LANES = 128
VMEM_LIMIT = 56 * 1024 * 1024
LOG2E = math.log2(math.e)
QK_SCALE = HEAD_DIM ** -0.5 * LOG2E

NT_DIMS = (((1,), (1,)), ((), ()))
TN_DIMS = (((0,), (0,)), ((), ()))


def _params(*sem):
    return pltpu.CompilerParams(dimension_semantics=sem, vmem_limit_bytes=VMEM_LIMIT)


def _sigmoid(x):
    return 1.0 / (1.0 + jnp.exp(-x))


def _head(h):
    return slice(h * HEAD_DIM, (h + 1) * HEAD_DIM)


def _two_streams(rows, width, n_first, n_second=None):
    mode = {"pipeline_mode": pl.Buffered(1)} if n_second == 1 else {}
    return (pl.BlockSpec((rows, width), lambda i, *_: (jnp.minimum(i, n_first - 1), 0)),
            pl.BlockSpec((rows, width), lambda i, *_: (jnp.maximum(i - n_first, 0), 0), **mode))


def _head_rows(h, rows):
    return pl.ds(h, rows, stride=N_HEADS)


def _normed(x, g_ref):
    ms = jnp.mean(x * x, axis=-1, keepdims=True)
    return (x * lax.rsqrt(ms + EPS) * g_ref[...]).astype(BF16)


def _rmsnorm_kernel(xp_ref, xs_ref, g_ref, o_ref, *, n_first):
    @pl.when(pl.program_id(0) < n_first)
    def _():
        o_ref[...] = _normed(xp_ref[...], g_ref)

    @pl.when(pl.program_id(0) >= n_first)
    def _():
        o_ref[...] = _normed(xs_ref[...], g_ref)


def _rmsnorm_rows(xp, xs, g, *, tm):
    D = xp.shape[1]
    n_first = xp.shape[0] // tm
    T = xp.shape[0] + xs.shape[0]
    return pl.pallas_call(
        functools.partial(_rmsnorm_kernel, n_first=n_first),
        grid=(T // tm,),
        in_specs=[*_two_streams(tm, D, n_first), pl.BlockSpec((1, D), lambda i: (0, 0))],
        out_specs=pl.BlockSpec((tm, D), lambda i: (i, 0)),
        out_shape=jax.ShapeDtypeStruct((T, D), BF16),
        compiler_params=_params("parallel"),
        name="rmsnorm_rows",
    )(xp, xs, g)


def _proj_kernel(h_ref, w_ref, o_ref, *, act):
    acc = jnp.dot(h_ref[...], w_ref[...], preferred_element_type=F32)
    if act == "sigmoid":
        acc = _sigmoid(acc)
    o_ref[...] = acc.astype(o_ref.dtype)


def _proj(h, w, *, n_cols, act, out_dtype, tm, tn):
    T, D = h.shape
    return pl.pallas_call(
        functools.partial(_proj_kernel, act=act),
        grid=(T // tm, n_cols // tn),
        in_specs=[pl.BlockSpec((tm, D), lambda i, j: (i, 0)),
                  pl.BlockSpec((D, tn), lambda i, j: (0, j))],
        out_specs=pl.BlockSpec((tm, tn), lambda i, j: (i, j)),
        out_shape=jax.ShapeDtypeStruct((T, n_cols), out_dtype),
        compiler_params=_params("parallel", "arbitrary"),
        name="proj_" + (act or "lin"),
    )(h, w)


def _proj_fox_kernel(h_ref, w_ref, wfl_ref,
                     q_ref, kb_ref, vb_ref, kp_ref, ks_ref, vp_ref, vs_ref, fl_ref, *, n_first):
    i, j = pl.program_id(0), pl.program_id(1)

    @pl.when(j == 0)
    def _():
        fl_ref[...] = lax.dot_general(wfl_ref[...], h_ref[...], NT_DIMS,
                                      preferred_element_type=F32)

    acc = jnp.dot(h_ref[...], w_ref[...], preferred_element_type=F32)

    @pl.when(j == 0)
    def _():
        q_ref[...] = (acc * QK_SCALE).astype(BF16)

    def heads_out(first_ref, second_ref):
        @pl.when(i < n_first)
        def _():
            for h in range(N_HEADS):
                first_ref[_head_rows(h, acc.shape[0]), :] = acc[:, _head(h)]

        @pl.when(i >= n_first)
        def _():
            for h in range(N_HEADS):
                second_ref[_head_rows(h, acc.shape[0]), :] = acc[:, _head(h)]

    @pl.when(j == 1)
    def _():
        kb_ref[...] = acc.astype(BF16)
        heads_out(kp_ref, ks_ref)

    @pl.when(j == 2)
    def _():
        vb_ref[...] = acc.astype(BF16)
        heads_out(vp_ref, vs_ref)


def _proj_fox(h, w, w_fl_t, *, col0, rows_first, tm):
    T, D = h.shape
    Tp, Ts = rows_first, T - rows_first
    n_first = Tp // tm
    kv_p, kv_s = _two_streams(tm * N_HEADS, HEAD_DIM, n_first)
    kv_shape = lambda rows: jax.ShapeDtypeStruct((rows * N_HEADS, HEAD_DIM), F32)
    tok = pl.BlockSpec((tm, WIDTH), lambda i, j: (i, 0))
    tok_shape = jax.ShapeDtypeStruct((T, WIDTH), BF16)
    return pl.pallas_call(
        functools.partial(_proj_fox_kernel, n_first=n_first),
        grid=(T // tm, 3),
        in_specs=[pl.BlockSpec((tm, D), lambda i, j: (i, 0)),
                  pl.BlockSpec((D, WIDTH), lambda i, j: (0, col0 // WIDTH + j)),
                  pl.BlockSpec((N_HEADS, D), lambda i, j: (0, 0))],
        out_specs=[tok, tok, tok, kv_p, kv_s, kv_p, kv_s,
                   pl.BlockSpec((N_HEADS, tm), lambda i, j: (0, i))],
        out_shape=[tok_shape, tok_shape, tok_shape,
                   kv_shape(Tp), kv_shape(Ts), kv_shape(Tp), kv_shape(Ts),
                   jax.ShapeDtypeStruct((N_HEADS, T), F32)],
        compiler_params=_params("arbitrary", "arbitrary"),
        name="proj_fox",
    )(h, w, w_fl_t)


def _lane_cumsum(x, seg):
    pos = lax.broadcasted_iota(jnp.int32, x.shape, x.ndim - 1) & (seg - 1)
    shift = 1
    while shift < seg:
        x = x + jnp.where(pos >= shift, pltpu.roll(x, shift, x.ndim - 1), 0.0)
        shift *= 2
    return x


def _gate_kernel(fl_ref, bias_ref, lf_ref, c_ref, *, seg):
    z = fl_ref[...] + bias_ref[...]
    lf = jnp.minimum(z, 0.0) - jnp.log(1.0 + jnp.exp(-jnp.abs(z)))
    lf_ref[...] = lf
    c = _lane_cumsum(lf, seg) * LOG2E
    c_ref[...] = jnp.zeros(c_ref.shape, F32)
    c_ref[0:c.shape[0], :] = c


def _fox_gates(fl_t, bias, *, col0, cols, seg, block, pad_rows):
    H = fl_t.shape[0]
    nb = cols // block
    base = col0 // block
    return pl.pallas_call(
        functools.partial(_gate_kernel, seg=seg),
        grid=(nb,),
        in_specs=[pl.BlockSpec((H, block), lambda i: (0, base + i)),
                  pl.BlockSpec((H, 1), lambda i: (0, 0))],
        out_specs=[pl.BlockSpec((H, block), lambda i: (0, i)),
                   pl.BlockSpec((None, pad_rows, block), lambda i: (i, 0, 0))],
        out_shape=[jax.ShapeDtypeStruct((H, cols), F32),
                   jax.ShapeDtypeStruct((nb, pad_rows, block), F32)],
        compiler_params=_params("parallel"),
        name="fox_gates",
    )(fl_t, bias)


def _suffix_kernel(lf_ref, o_ref):
    lf = lf_ref[...]
    c = _lane_cumsum(lf, lf.shape[-1])
    o_ref[...] = (c[:, -1:] - c) * LOG2E


def _cache_suffix(clf_t):
    B, H, P = clf_t.shape
    spec = pl.BlockSpec((None, H, P), lambda b: (b, 0, 0))
    return pl.pallas_call(
        _suffix_kernel,
        grid=(B,),
        in_specs=[spec],
        out_specs=spec,
        out_shape=jax.ShapeDtypeStruct((B, H, P), F32),
        compiler_params=_params("parallel"),
        name="fox_cache_suffix",
    )(clf_t)


SUB = 16


def _gla_kernel(aq_ref, af_ref, ai_ref, ag_ref, lbl_ref, gn_ref, s0_ref, tri_ref,
                o_ref, s_out_ref, st_ref, *, chunk, n_chunks):
    t = pl.program_id(1)

    @pl.when(t == 0)
    def _():
        for h in range(N_HEADS):
            st_ref[h] = s0_ref[h].T

    lbl = lbl_ref[...]
    e = jnp.exp(lbl - jnp.max(lbl, axis=0, keepdims=True))
    lb = e[0:1] / jnp.sum(e, axis=0, keepdims=True)
    gn = gn_ref[...]
    tri = tri_ref[...]
    n_sub = chunk // SUB

    def one_chunk(ci, carry):
        r = pl.ds(pl.multiple_of(ci * chunk, chunk), chunk)
        fa = lb + (1.0 - lb) * _sigmoid(af_ref[r, :])
        lf = jnp.log(fa)
        ka = 1.0 - fa
        aq = aq_ref[r, :]
        qa = aq * _sigmoid(aq)
        v = ai_ref[r, :].astype(BF16)
        ag = ag_ref[r, :]
        gate = ag * _sigmoid(ag)

        hi = lf.astype(BF16)
        r1 = lf - hi.astype(F32)
        mid = r1.astype(BF16)
        lo = (r1 - mid.astype(F32)).astype(BF16)
        b = (jnp.dot(tri, hi, preferred_element_type=F32)
             + jnp.dot(tri, mid, preferred_element_type=F32)
             + jnp.dot(tri, lo, preferred_element_type=F32))
        b_end = b[chunk - 1:chunk, :]

        q_in = (qa * jnp.exp(b)).astype(BF16)
        k_out = (ka * jnp.exp(b_end - b)).astype(BF16)
        decay = jnp.exp(b_end)
        qp, kp = [], []
        for i in range(n_sub):
            r0, n = i * SUB, (i + 1) * SUB
            m_i = b[r0 + SUB // 2:r0 + SUB // 2 + 1, :]
            qp.append((qa[r0:n] * jnp.exp(b[r0:n] - m_i)).astype(BF16))
            kp.append((ka[:n] * jnp.exp(m_i - b[:n])).astype(BF16))

        st = [st_ref[h] for h in range(N_HEADS)]
        inter = [lax.dot_general(q_in[:, _head(h)], st[h].astype(BF16), NT_DIMS,
                                 preferred_element_type=F32) for h in range(N_HEADS)]
        att = [[lax.dot_general(qp[i][:, _head(h)], kp[i][:, _head(h)], NT_DIMS,
                                preferred_element_type=F32) for h in range(N_HEADS)]
               for i in range(n_sub)]
        for h in range(N_HEADS):
            st_ref[h] = st[h] * decay[:, _head(h)] + lax.dot_general(
                v[:, _head(h)], k_out[:, _head(h)], TN_DIMS, preferred_element_type=F32)
        for i in range(n_sub):
            r0, n = i * SUB, (i + 1) * SUB
            causal = (lax.broadcasted_iota(jnp.int32, (SUB, n), 1)
                      <= lax.broadcasted_iota(jnp.int32, (SUB, n), 0) + r0)
            att[i] = [jnp.where(causal, a, 0.0).astype(BF16) for a in att[i]]
        for h in range(N_HEADS):
            o = jnp.concatenate(
                [inter[h][i * SUB:(i + 1) * SUB]
                 + jnp.dot(att[i][h], v[:(i + 1) * SUB, _head(h)], preferred_element_type=F32)
                 for i in range(n_sub)], axis=0)
            ms = jnp.mean(o * o, axis=-1, keepdims=True)
            o_ref[r, _head(h)] = (o * lax.rsqrt(ms + EPS) * gn * gate[:, _head(h)]).astype(o_ref.dtype)
        return carry

    lax.fori_loop(0, n_chunks, one_chunk, 0)

    @pl.when(t == pl.num_programs(1) - 1)
    def _():
        for h in range(N_HEADS):
            s_out_ref[h] = st_ref[h].T


def _hgrn2(a_proj, lb_logits, gnorm, s0, *, batch, length, row0, chunk, n_chunks):
    step = chunk * n_chunks
    nt = length // step
    base = row0 // step

    def col(g):
        return pl.BlockSpec((step, WIDTH), lambda b, t: (base + b * nt + t, g))

    st_spec = pl.BlockSpec((None, N_HEADS, HEAD_DIM, HEAD_DIM), lambda b, t: (b, 0, 0, 0))
    tri = jnp.tril(jnp.ones((chunk, chunk), BF16))
    return pl.pallas_call(
        functools.partial(_gla_kernel, chunk=chunk, n_chunks=n_chunks),
        grid=(batch, nt),
        in_specs=[col(0), col(1), col(2), col(3),
                  pl.BlockSpec(lb_logits.shape, lambda b, t: (0, 0)),
                  pl.BlockSpec((1, HEAD_DIM), lambda b, t: (0, 0)),
                  st_spec,
                  pl.BlockSpec((chunk, chunk), lambda b, t: (0, 0))],
        out_specs=[pl.BlockSpec((step, WIDTH), lambda b, t: (b * nt + t, 0)), st_spec],
        out_shape=[jax.ShapeDtypeStruct((batch * length, WIDTH), BF16),
                   jax.ShapeDtypeStruct((batch, N_HEADS, HEAD_DIM, HEAD_DIM), F32)],
        scratch_shapes=[pltpu.VMEM((N_HEADS, HEAD_DIM, HEAD_DIM), F32)],
        compiler_params=_params("parallel", "arbitrary"),
        name="hgrn2",
    )(a_proj, a_proj, a_proj, a_proj, lb_logits, gnorm, s0, tri)


def _attn_prompt_kernel(q_ref, k_ref, v_ref, c_ref, o_ref,
                        vt_ref, cb_ref, s0_ref, s1_ref, m_ref, l_ref, acc_ref, *, blk):
    h, iq = pl.program_id(1), pl.program_id(2)
    length = k_ref.shape[0]
    halves = (slice(0, blk // 2), slice(blk // 2, blk))

    @pl.when(iq == 0)
    def _():
        for r0 in range(0, length, blk):
            rs = slice(r0, r0 + blk)
            vt_ref[:, rs] = v_ref[rs, :].astype(F32).T.astype(BF16)
            c_cols = c_ref[:, rs].T
            onehot = lax.broadcasted_iota(jnp.int32, c_cols.shape, 1) == h
            cb_ref[rs, :] = jnp.broadcast_to(
                jnp.sum(jnp.where(onehot, c_cols, 0.0), axis=1, keepdims=True), (blk, LANES))

    q = q_ref[...]
    m_ref[...] = jnp.full(m_ref.shape, -jnp.inf, F32)
    l_ref[...] = jnp.zeros(l_ref.shape, F32)
    acc_ref[...] = jnp.zeros(acc_ref.shape, F32)

    def scores(j, s_ref):
        ks = pl.ds(pl.multiple_of(j * blk, blk), blk)
        kb = k_ref[ks, :]
        cb = cb_ref[ks, :]
        for hv in halves:
            x = lax.dot_general(kb, q[hv, :], NT_DIMS, preferred_element_type=F32)
            s_ref[:, hv] = x - jnp.concatenate([cb] * (x.shape[1] // LANES), axis=1)

    def consume(j, s_ref, masked):
        vt = vt_ref[:, pl.ds(pl.multiple_of(j * blk, blk), blk)]
        p = []
        for hv in halves:
            x = s_ref[:, hv]
            if masked:
                key = lax.broadcasted_iota(jnp.int32, x.shape, 0)
                qry = lax.broadcasted_iota(jnp.int32, x.shape, 1) + hv.start
                x = jnp.where(key <= qry, x, -jnp.inf)
            m_prev = m_ref[:, hv]
            m_new = jnp.maximum(m_prev, jnp.max(x, axis=0, keepdims=True))
            alpha = jnp.exp2(m_prev - m_new)
            e = jnp.exp2(x - m_new)
            l_ref[:, hv] = alpha * l_ref[:, hv] + jnp.sum(e, axis=0, keepdims=True)
            m_ref[:, hv] = m_new
            acc_ref[:, hv] = alpha * acc_ref[:, hv]
            p.append(e.astype(BF16))
        for e, hv in zip(p, halves):
            acc_ref[:, hv] += jnp.dot(vt, e, preferred_element_type=F32)

    scores(0, s0_ref)

    def pair(p, carry):
        j = 2 * p
        scores(j + 1, s1_ref)
        consume(j, s0_ref, False)
        scores(j + 2, s0_ref)
        consume(j + 1, s1_ref, False)
        return carry

    lax.fori_loop(0, iq // 2, pair, 0)

    @pl.when(iq % 2 == 0)
    def _():
        consume(iq, s0_ref, True)

    @pl.when(iq % 2 == 1)
    def _():
        scores(iq, s1_ref)
        consume(iq - 1, s0_ref, False)
        consume(iq, s1_ref, True)

    o_ref[...] = (acc_ref[...] / l_ref[...]).T.astype(o_ref.dtype)


def _fox_prompt(q, k, v, c, *, batch, length, blk):
    nq = length // blk
    qspec = pl.BlockSpec((blk, HEAD_DIM), lambda b, h, i: (b * nq + i, h))
    kvspec = pl.BlockSpec((length, HEAD_DIM), lambda b, h, i: (b, h))
    return pl.pallas_call(
        functools.partial(_attn_prompt_kernel, blk=blk),
        grid=(batch, N_HEADS, nq),
        in_specs=[qspec, kvspec, kvspec,
                  pl.BlockSpec((None,) + c.shape[1:], lambda b, h, i: (b, 0, 0))],
        out_specs=qspec,
        out_shape=jax.ShapeDtypeStruct((batch * length, WIDTH), BF16),
        scratch_shapes=[pltpu.VMEM((HEAD_DIM, length), BF16),
                        pltpu.VMEM((length, LANES), F32),
                        pltpu.VMEM((blk, blk), F32), pltpu.VMEM((blk, blk), F32),
                        pltpu.VMEM((1, blk), F32), pltpu.VMEM((1, blk), F32),
                        pltpu.VMEM((HEAD_DIM, blk), F32)],
        compiler_params=_params("parallel", "parallel", "arbitrary"),
        name="fox_prompt",
    )(q, k, v, c)


def _attend_heads(score_fn, value_fn, m_ref, l_ref, acc_ref):
    s = [score_fn(h) for h in range(N_HEADS)]
    stats = []
    for h in range(N_HEADS):
        m_prev = m_ref[h]
        m_new = jnp.maximum(m_prev, jnp.max(s[h], axis=-1, keepdims=True))
        stats.append((jnp.exp2(m_prev - m_new), m_new))
    p = [jnp.exp2(s[h] - stats[h][1]) for h in range(N_HEADS)]
    for h in range(N_HEADS):
        alpha, m_new = stats[h]
        l_ref[h] = alpha * l_ref[h] + jnp.sum(p[h], axis=-1, keepdims=True)
        m_ref[h] = m_new
        acc_ref[:, _head(h)] = alpha * acc_ref[:, _head(h)] + jnp.dot(
            p[h].astype(BF16), value_fn(h), preferred_element_type=F32)


def _attn_sample_kernel(q_ref, kc_hbm, vc_hbm, bc_ref, kn_ref, vn_ref, bn_ref, o_ref,
                        kbuf_ref, vbuf_ref, sem_ref, m_ref, l_ref, acc_ref):
    b, j = pl.program_id(0), pl.program_id(1)
    nj = pl.num_programs(1)
    tk = bc_ref.shape[1]
    step = b * nj + j
    slot = step % 2

    def cache_copies(stream, blk, into):
        return [pltpu.make_async_copy(hbm.at[stream, pl.ds(blk * tk, tk), h, :],
                                      buf.at[into, h], sem_ref.at[t, into])
                for t, (hbm, buf) in enumerate(((kc_hbm, kbuf_ref), (vc_hbm, vbuf_ref)))
                for h in range(N_HEADS)]

    def start_all(copies):
        for n, cp in enumerate(copies):
            cp.start(priority=n % 2)

    @pl.when(step == 0)
    def _():
        start_all(cache_copies(b, j, slot))

    @pl.when(step + 1 < pl.num_programs(0) * nj)
    def _():
        start_all(cache_copies((step + 1) // nj, (step + 1) % nj, 1 - slot))

    @pl.when(j == 0)
    def _():
        m_ref[...] = jnp.full(m_ref.shape, -jnp.inf, F32)
        l_ref[...] = jnp.zeros(l_ref.shape, F32)
        acc_ref[...] = jnp.zeros(acc_ref.shape, F32)

    for cp in cache_copies(b, j, slot):
        cp.wait()
    _attend_heads(
        lambda h: lax.dot_general(q_ref[:, _head(h)], kbuf_ref[slot, h].astype(BF16),
                                  NT_DIMS, preferred_element_type=F32) + bc_ref[h:h + 1, :],
        lambda h: vbuf_ref[slot, h].astype(BF16), m_ref, l_ref, acc_ref)

    @pl.when(j == pl.num_programs(1) - 1)
    def _():
        tq = q_ref.shape[0]
        causal = (lax.broadcasted_iota(jnp.int32, (tq, tq), 1)
                  <= lax.broadcasted_iota(jnp.int32, (tq, tq), 0))
        _attend_heads(
            lambda h: jnp.where(causal,
                                lax.dot_general(q_ref[:, _head(h)], kn_ref[:, _head(h)], NT_DIMS,
                                                preferred_element_type=F32) - bn_ref[h:h + 1, :],
                                -jnp.inf),
            lambda h: vn_ref[:, _head(h)], m_ref, l_ref, acc_ref)
        for h in range(N_HEADS):
            o_ref[:, _head(h)] = (acc_ref[:, _head(h)] / l_ref[h]).astype(o_ref.dtype)


def _fox_sample(q, k_new, v_new, c_new, cache_k, cache_v, cache_bias, *, row0, tk):
    B, H, P = cache_bias.shape
    Ls = c_new.shape[2]
    base = row0 // Ls
    new = pl.BlockSpec((Ls, WIDTH), lambda b, j: (base + b, 0))
    cache = pl.BlockSpec(memory_space=pl.ANY)
    cache_buf = pltpu.VMEM((2, H, tk, HEAD_DIM), F32)
    return pl.pallas_call(
        _attn_sample_kernel,
        grid=(B, P // tk),
        in_specs=[new, cache, cache,
                  pl.BlockSpec((None, N_HEADS, tk), lambda b, j: (b, 0, j)),
                  new, new,
                  pl.BlockSpec((None, N_HEADS, Ls), lambda b, j: (b, 0, 0))],
        out_specs=pl.BlockSpec((Ls, WIDTH), lambda b, j: (b, 0)),
        out_shape=jax.ShapeDtypeStruct((B * Ls, WIDTH), BF16),
        scratch_shapes=[cache_buf, cache_buf, pltpu.SemaphoreType.DMA((2, 2)),
                        pltpu.VMEM((N_HEADS, Ls, 1), F32), pltpu.VMEM((N_HEADS, Ls, 1), F32),
                        pltpu.VMEM((Ls, WIDTH), F32)],
        compiler_params=_params("arbitrary", "arbitrary"),
        name="fox_sample",
    )(q, cache_k, cache_v, cache_bias, k_new, v_new, c_new)


MERGE_COLS = 512


def _merge_kernel(xp_ref, xs_ref, oap_ref, oas_ref, obp_ref, obs_ref, g_ref,
                  wpa_ref, wpb_ref, wo_ref, o_ref, mg_ref, *, n_first):
    first = pl.program_id(0) < n_first
    D = xp_ref.shape[1]
    oa = jnp.where(first, oap_ref[...], oas_ref[...])
    ob = jnp.where(first, obp_ref[...], obs_ref[...])
    cols = min(MERGE_COLS, D)
    for n0 in range(0, D, cols):
        ns = slice(n0, n0 + cols)
        pa = jnp.dot(oa, wpa_ref[:, ns], preferred_element_type=F32)
        pb = jnp.dot(ob, wpb_ref[:, ns], preferred_element_type=F32)
        ga = g_ref[:, ns].astype(F32)
        gb = g_ref[:, D + n0:D + n0 + cols].astype(F32)
        mg_ref[:, ns] = (ga * pa + gb * pb).astype(BF16)
    x = jnp.where(first, xp_ref[...], xs_ref[...])
    o_ref[...] = x + jnp.dot(mg_ref[...], wo_ref[...], preferred_element_type=F32)


def _merge_out(xp, xs, oa_p, oa_s, ob_p, ob_s, g, w_pa, w_pb, w_o, *, tm):
    D = xp.shape[1]
    T = xp.shape[0] + xs.shape[0]
    n_first = xp.shape[0] // tm
    fixed = lambda i: (0, 0)
    return pl.pallas_call(
        functools.partial(_merge_kernel, n_first=n_first),
        grid=(T // tm,),
        in_specs=[*_two_streams(tm, D, n_first), *_two_streams(tm, WIDTH, n_first),
                  *_two_streams(tm, WIDTH, n_first),
                  pl.BlockSpec((tm, 2 * D), lambda i: (i, 0)),
                  *[pl.BlockSpec(w.shape, fixed, pipeline_mode=pl.Buffered(1))
                    for w in (w_pa, w_pb, w_o)]],
        out_specs=pl.BlockSpec((tm, D), lambda i: (i, 0)),
        out_shape=jax.ShapeDtypeStruct((T, D), F32),
        scratch_shapes=[pltpu.VMEM((tm, D), BF16)],
        compiler_params=_params("parallel"),
        name="merge_out",
    )(xp, xs, oa_p, oa_s, ob_p, ob_s, g, w_pa, w_pb, w_o)


def _ffn_kernel(x_ref, g2_ref, w1_ref, w2_ref, gf_ref, yp_ref, ys_ref, h_ref, acc_ref, *, n_first):
    i, f = pl.program_id(0), pl.program_id(1)

    @pl.when(f == 0)
    def _():
        h_ref[...] = _normed(x_ref[...], g2_ref)
        acc_ref[...] = x_ref[...]

    u = jnp.maximum(jnp.dot(h_ref[...], w1_ref[...], preferred_element_type=F32), 0.0)
    acc_ref[...] += jnp.dot((u * u).astype(BF16), w2_ref[...], preferred_element_type=F32)

    @pl.when(f == pl.num_programs(1) - 1)
    def _():
        x = acc_ref[...]
        ms = jnp.mean(x * x, axis=-1, keepdims=True)
        y = x * lax.rsqrt(ms + EPS) * gf_ref[...]

        @pl.when(i < n_first)
        def _():
            yp_ref[...] = y

        @pl.when(i >= n_first)
        def _():
            ys_ref[...] = y


def _ffn(x, g2, w1, w2, gf, *, rows_first, tm, tf):
    T, D = x.shape
    F = w1.shape[1]
    n_first = rows_first // tm
    vec = pl.BlockSpec((1, D), lambda i, f: (0, 0))
    return pl.pallas_call(
        functools.partial(_ffn_kernel, n_first=n_first),
        grid=(T // tm, F // tf),
        in_specs=[pl.BlockSpec((tm, D), lambda i, f: (i, 0)), vec,
                  pl.BlockSpec((D, tf), lambda i, f: (0, f)),
                  pl.BlockSpec((tf, D), lambda i, f: (f, 0)), vec],
        out_specs=list(_two_streams(tm, D, n_first)),
        out_shape=[jax.ShapeDtypeStruct((rows_first, D), F32),
                   jax.ShapeDtypeStruct((T - rows_first, D), F32)],
        scratch_shapes=[pltpu.VMEM((tm, D), BF16), pltpu.VMEM((tm, D), F32)],
        compiler_params=_params("arbitrary", "arbitrary"),
        name="ffn",
    )(x, g2, w1, w2, gf)


def _tile(n, pref):
    t = min(pref, n)
    while n % t:
        t //= 2
    return t


def kernel(x_prompt, x_sample, cache_fox_k, cache_fox_v, cache_fox_logf, state_hgrn, norm1, w_in,
           b_fox_f, lb_logits, gnorm_a, w_pa, w_pb, w_o, norm2, w1, w2, norm_f):
    B, L, D = x_prompt.shape
    Bs, Ls, _ = x_sample.shape
    depth, _, P, H, DH = cache_fox_k.shape
    assert depth == 1 and H == N_HEADS and DH == HEAD_DIM
    Tp, Ts = B * L, Bs * Ls
    T = Tp + Ts
    W = WIDTH

    xp, xs = x_prompt.reshape(Tp, D), x_sample.reshape(Ts, D)
    w_inb = w_in[0].astype(BF16)
    w_fl_t = w_inb[:, 7 * W:7 * W + H].T
    w_g = w_inb[:, 7 * W + H:]
    g1 = norm1[0].reshape(1, D)

    tm = _tile(math.gcd(Tp, Ts), 512)
    tm1 = _tile(math.gcd(Tp, Ts), 1024)
    h1 = _rmsnorm_rows(xp, xs, g1, tm=_tile(tm, 256))
    a_proj = _proj(h1, w_inb, n_cols=4 * W, act=None, out_dtype=F32, tm=tm1, tn=W)
    gates = _proj(h1, w_g, n_cols=2 * D, act="sigmoid", out_dtype=BF16, tm=tm1, tn=_tile(2 * D, W))
    q_b, k_b, v_b, k_p, k_s, v_p, v_s, fl_t = _proj_fox(h1, w_inb, w_fl_t, col0=4 * W,
                                                        rows_first=Tp, tm=_tile(tm, 512))

    bias = b_fox_f[0].reshape(H, 1)
    lf_p, c_p = _fox_gates(fl_t, bias, col0=0, cols=Tp, seg=L, block=L, pad_rows=LANES)
    lf_s, c_s = _fox_gates(fl_t, bias, col0=Tp, cols=Ts, seg=Ls, block=_tile(Ts, 1024), pad_rows=H)
    cache_bias = _cache_suffix(jnp.transpose(cache_fox_logf[0], (0, 2, 1)))

    zeros_state = jnp.zeros((B, H, DH, DH), F32)
    chunk_p = _tile(L, 64)
    oa_p, s_p = _hgrn2(a_proj, lb_logits, gnorm_a, zeros_state, batch=B, length=L, row0=0,
                       chunk=chunk_p, n_chunks=_tile(L // chunk_p, 4))
    oa_s, s_s = _hgrn2(a_proj, lb_logits, gnorm_a, state_hgrn[0], batch=Bs, length=Ls, row0=Tp,
                       chunk=_tile(Ls, 64), n_chunks=1)

    ob_p = _fox_prompt(q_b, k_b, v_b, c_p, batch=B, length=L, blk=_tile(L, 512))
    c_s3 = c_s.transpose(1, 0, 2).reshape(H, Bs, Ls).transpose(1, 0, 2)
    ob_s = _fox_sample(q_b, k_b, v_b, c_s3, cache_fox_k[0], cache_fox_v[0], cache_bias,
                       row0=Tp, tk=_tile(P, 1024))

    tm5 = _tile(tm, 512)
    x1 = _merge_out(xp, xs, oa_p, oa_s, ob_p, ob_s, gates, w_pa[0].astype(BF16),
                    w_pb[0].astype(BF16), w_o[0].astype(BF16), tm=_tile(tm, 256))
    y_p, y_s = _ffn(x1, norm2[0].reshape(1, D), w1[0].astype(BF16), w2[0].astype(BF16),
                    norm_f.reshape(1, D), rows_first=Tp, tm=tm5, tf=_tile(w1.shape[-1], 1024))

    return (y_p.reshape(B, L, D), y_s.reshape(Bs, Ls, D),
            k_p.reshape(1, B, L, H, DH), v_p.reshape(1, B, L, H, DH),
            lf_p.T.reshape(1, B, L, H), s_p[None],
            k_s.reshape(1, Bs, Ls, H, DH), v_s.reshape(1, Bs, Ls, H, DH),
            lf_s.T.reshape(1, Bs, Ls, H), s_s[None])
```

```python
import functools
import math

import jax
import jax.numpy as jnp
from jax import lax
from jax.experimental import pallas as pl
from jax.experimental.pallas import tpu as pltpu

F32 = jnp.float32
BF16 = jnp.bfloat16

EPS = 1e-6
N_HEADS = 8
HEAD_DIM = 128
WIDTH = N_HEADS * HEAD_DIM
LANES = 128
VMEM_LIMIT = 56 * 1024 * 1024
LOG2E = math.log2(math.e)
QK_SCALE = HEAD_DIM ** -0.5 * LOG2E

NT_DIMS = (((1,), (1,)), ((), ()))
TN_DIMS = (((0,), (0,)), ((), ()))


def _params(*sem):
    return pltpu.CompilerParams(dimension_semantics=sem, vmem_limit_bytes=VMEM_LIMIT)


def _sigmoid(x):
    return 1.0 / (1.0 + jnp.exp(-x))


def _head(h):
    return slice(h * HEAD_DIM, (h + 1) * HEAD_DIM)


def _two_streams(rows, width, n_first, n_second=None):
    mode = {"pipeline_mode": pl.Buffered(1)} if n_second == 1 else {}
    return (pl.BlockSpec((rows, width), lambda i, *_: (jnp.minimum(i, n_first - 1), 0)),
            pl.BlockSpec((rows, width), lambda i, *_: (jnp.maximum(i - n_first, 0), 0), **mode))


def _head_rows(h, rows):
    return pl.ds(h, rows, stride=N_HEADS)


def _normed(x, g_ref):
    ms = jnp.mean(x * x, axis=-1, keepdims=True)
    return (x * lax.rsqrt(ms + EPS) * g_ref[...]).astype(BF16)


def _rmsnorm_kernel(xp_ref, xs_ref, g_ref, o_ref, *, n_first):
    @pl.when(pl.program_id(0) < n_first)
    def _():
        o_ref[...] = _normed(xp_ref[...], g_ref)

    @pl.when(pl.program_id(0) >= n_first)
    def _():
        o_ref[...] = _normed(xs_ref[...], g_ref)


def _rmsnorm_rows(xp, xs, g, *, tm):
    D = xp.shape[1]
    n_first = xp.shape[0] // tm
    T = xp.shape[0] + xs.shape[0]
    return pl.pallas_call(
        functools.partial(_rmsnorm_kernel, n_first=n_first),
        grid=(T // tm,),
        in_specs=[*_two_streams(tm, D, n_first), pl.BlockSpec((1, D), lambda i: (0, 0))],
        out_specs=pl.BlockSpec((tm, D), lambda i: (i, 0)),
        out_shape=jax.ShapeDtypeStruct((T, D), BF16),
        compiler_params=_params("parallel"),
        name="rmsnorm_rows",
    )(xp, xs, g)


def _proj_kernel(h_ref, w_ref, o_ref, *, act):
    acc = jnp.dot(h_ref[...], w_ref[...], preferred_element_type=F32)
    if act == "sigmoid":
        acc = _sigmoid(acc)
    o_ref[...] = acc.astype(o_ref.dtype)


def _proj(h, w, *, n_cols, act, out_dtype, tm, tn):
    T, D = h.shape
    return pl.pallas_call(
        functools.partial(_proj_kernel, act=act),
        grid=(T // tm, n_cols // tn),
        in_specs=[pl.BlockSpec((tm, D), lambda i, j: (i, 0)),
                  pl.BlockSpec((D, tn), lambda i, j: (0, j))],
        out_specs=pl.BlockSpec((tm, tn), lambda i, j: (i, j)),
        out_shape=jax.ShapeDtypeStruct((T, n_cols), out_dtype),
        compiler_params=_params("parallel", "arbitrary"),
        name="proj_" + (act or "lin"),
    )(h, w)


def _proj_fox_kernel(h_ref, w_ref, wfl_ref,
                     q_ref, kb_ref, vb_ref, kp_ref, ks_ref, vp_ref, vs_ref, fl_ref, *, n_first):
    i, j = pl.program_id(0), pl.program_id(1)

    @pl.when(j == 0)
    def _():
        fl_ref[...] = lax.dot_general(wfl_ref[...], h_ref[...], NT_DIMS,
                                      preferred_element_type=F32)

    h = h_ref[...]
    tm = h.shape[0]
    pair = 2 * HEAD_DIM

    def column_pairs():
        for c in range(WIDTH // pair):
            cs = slice(c * pair, (c + 1) * pair)
            yield c, cs, jnp.dot(h, w_ref[:, cs], preferred_element_type=F32)

    @pl.when(j == 0)
    def _():
        for _, cs, acc in column_pairs():
            q_ref[:, cs] = (acc * QK_SCALE).astype(BF16)

    def key_or_value(bf_ref, heads_ref):
        for c, cs, acc in column_pairs():
            bf_ref[:, cs] = acc.astype(BF16)
            for hh in range(2):
                heads_ref[_head_rows(2 * c + hh, tm), :] = acc[:, _head(hh)]

    for jj, bf_ref, first_ref, second_ref in ((1, kb_ref, kp_ref, ks_ref),
                                              (2, vb_ref, vp_ref, vs_ref)):
        pl.when((j == jj) & (i < n_first))(
            functools.partial(key_or_value, bf_ref, first_ref))
        pl.when((j == jj) & (i >= n_first))(
            functools.partial(key_or_value, bf_ref, second_ref))


def _proj_fox(h, w, w_fl_t, *, col0, rows_first, tm):
    T, D = h.shape
    Tp, Ts = rows_first, T - rows_first
    n_first = Tp // tm
    kv_p, kv_s = _two_streams(tm * N_HEADS, HEAD_DIM, n_first)
    kv_shape = lambda rows: jax.ShapeDtypeStruct((rows * N_HEADS, HEAD_DIM), F32)
    tok = pl.BlockSpec((tm, WIDTH), lambda i, j: (i, 0))
    tok_shape = jax.ShapeDtypeStruct((T, WIDTH), BF16)
    return pl.pallas_call(
        functools.partial(_proj_fox_kernel, n_first=n_first),
        grid=(T // tm, 3),
        in_specs=[pl.BlockSpec((tm, D), lambda i, j: (i, 0)),
                  pl.BlockSpec((D, WIDTH), lambda i, j: (0, col0 // WIDTH + j)),
                  pl.BlockSpec((N_HEADS, D), lambda i, j: (0, 0))],
        out_specs=[tok, tok, tok, kv_p, kv_s, kv_p, kv_s,
                   pl.BlockSpec((N_HEADS, tm), lambda i, j: (0, i))],
        out_shape=[tok_shape, tok_shape, tok_shape,
                   kv_shape(Tp), kv_shape(Ts), kv_shape(Tp), kv_shape(Ts),
                   jax.ShapeDtypeStruct((N_HEADS, T), F32)],
        compiler_params=_params("arbitrary", "arbitrary"),
        name="proj_fox",
    )(h, w, w_fl_t)


def _lane_cumsum(x, seg):
    pos = lax.broadcasted_iota(jnp.int32, x.shape, x.ndim - 1) & (seg - 1)
    shift = 1
    while shift < seg:
        x = x + jnp.where(pos >= shift, pltpu.roll(x, shift, x.ndim - 1), 0.0)
        shift *= 2
    return x


def _gate_kernel(fl_ref, bias_ref, lf_ref, c_ref, *, seg):
    z = fl_ref[...] + bias_ref[...]
    lf = jnp.minimum(z, 0.0) - jnp.log(1.0 + jnp.exp(-jnp.abs(z)))
    lf_ref[...] = lf
    c = _lane_cumsum(lf, seg) * LOG2E
    c_ref[...] = jnp.zeros(c_ref.shape, F32)
    c_ref[0:c.shape[0], :] = c


def _fox_gates(fl_t, bias, *, col0, cols, seg, block, pad_rows):
    H = fl_t.shape[0]
    nb = cols // block
    base = col0 // block
    return pl.pallas_call(
        functools.partial(_gate_kernel, seg=seg),
        grid=(nb,),
        in_specs=[pl.BlockSpec((H, block), lambda i: (0, base + i)),
                  pl.BlockSpec((H, 1), lambda i: (0, 0))],
        out_specs=[pl.BlockSpec((H, block), lambda i: (0, i)),
                   pl.BlockSpec((None, pad_rows, block), lambda i: (i, 0, 0))],
        out_shape=[jax.ShapeDtypeStruct((H, cols), F32),
                   jax.ShapeDtypeStruct((nb, pad_rows, block), F32)],
        compiler_params=_params("parallel"),
        name="fox_gates",
    )(fl_t, bias)


def _suffix_kernel(lf_ref, o_ref):
    lf = lf_ref[...]
    c = _lane_cumsum(lf, lf.shape[-1])
    o_ref[...] = (c[:, -1:] - c) * LOG2E


def _cache_suffix(clf_t):
    B, H, P = clf_t.shape
    spec = pl.BlockSpec((None, H, P), lambda b: (b, 0, 0))
    return pl.pallas_call(
        _suffix_kernel,
        grid=(B,),
        in_specs=[spec],
        out_specs=spec,
        out_shape=jax.ShapeDtypeStruct((B, H, P), F32),
        compiler_params=_params("parallel"),
        name="fox_cache_suffix",
    )(clf_t)


SUB = 16


def _gla_kernel(aq_ref, af_ref, ai_ref, ag_ref, lbl_ref, gn_ref, s0_ref, tri_ref,
                o_ref, s_out_ref, st_ref, *, chunk, n_chunks):
    t = pl.program_id(1)

    @pl.when(t == 0)
    def _():
        for h in range(N_HEADS):
            st_ref[h] = s0_ref[h].T

    lbl = lbl_ref[...]
    e = jnp.exp(lbl - jnp.max(lbl, axis=0, keepdims=True))
    lb = e[0:1] / jnp.sum(e, axis=0, keepdims=True)
    gn = gn_ref[...]
    tri = tri_ref[...]
    n_sub = chunk // SUB

    def one_chunk(ci, carry):
        r = pl.ds(pl.multiple_of(ci * chunk, chunk), chunk)
        fa = lb + (1.0 - lb) * _sigmoid(af_ref[r, :])
        lf = jnp.log(fa)
        ka = 1.0 - fa
        aq = aq_ref[r, :]
        qa = aq * _sigmoid(aq)
        v = ai_ref[r, :].astype(BF16)
        ag = ag_ref[r, :]
        gate = ag * _sigmoid(ag)

        hi = lf.astype(BF16)
        r1 = lf - hi.astype(F32)
        mid = r1.astype(BF16)
        lo = (r1 - mid.astype(F32)).astype(BF16)
        b = (jnp.dot(tri, hi, preferred_element_type=F32)
             + jnp.dot(tri, mid, preferred_element_type=F32)
             + jnp.dot(tri, lo, preferred_element_type=F32))
        b_end = b[chunk - 1:chunk, :]

        q_in = (qa * jnp.exp(b)).astype(BF16)
        k_out = (ka * jnp.exp(b_end - b)).astype(BF16)
        decay = jnp.exp(b_end)
        qp, kp = [], []
        for i in range(n_sub):
            r0, n = i * SUB, (i + 1) * SUB
            m_i = b[r0 + SUB // 2:r0 + SUB // 2 + 1, :]
            qp.append((qa[r0:n] * jnp.exp(b[r0:n] - m_i)).astype(BF16))
            kp.append((ka[:n] * jnp.exp(m_i - b[:n])).astype(BF16))

        st = [st_ref[h] for h in range(N_HEADS)]
        inter = [lax.dot_general(q_in[:, _head(h)], st[h].astype(BF16), NT_DIMS,
                                 preferred_element_type=F32) for h in range(N_HEADS)]
        att = [[lax.dot_general(qp[i][:, _head(h)], kp[i][:, _head(h)], NT_DIMS,
                                preferred_element_type=F32) for h in range(N_HEADS)]
               for i in range(n_sub)]
        for h in range(N_HEADS):
            st_ref[h] = st[h] * decay[:, _head(h)] + lax.dot_general(
                v[:, _head(h)], k_out[:, _head(h)], TN_DIMS, preferred_element_type=F32)
        for i in range(n_sub):
            r0, n = i * SUB, (i + 1) * SUB
            causal = (lax.broadcasted_iota(jnp.int32, (SUB, n), 1)
                      <= lax.broadcasted_iota(jnp.int32, (SUB, n), 0) + r0)
            att[i] = [jnp.where(causal, a, 0.0).astype(BF16) for a in att[i]]
        for h in range(N_HEADS):
            o = jnp.concatenate(
                [inter[h][i * SUB:(i + 1) * SUB]
                 + jnp.dot(att[i][h], v[:(i + 1) * SUB, _head(h)], preferred_element_type=F32)
                 for i in range(n_sub)], axis=0)
            ms = jnp.mean(o * o, axis=-1, keepdims=True)
            o_ref[r, _head(h)] = (o * lax.rsqrt(ms + EPS) * gn * gate[:, _head(h)]).astype(o_ref.dtype)
        return carry

    lax.fori_loop(0, n_chunks, one_chunk, 0)

    @pl.when(t == pl.num_programs(1) - 1)
    def _():
        for h in range(N_HEADS):
            s_out_ref[h] = st_ref[h].T


def _hgrn2(a_proj, lb_logits, gnorm, s0, *, batch, length, row0, chunk, n_chunks):
    step = chunk * n_chunks
    nt = length // step
    base = row0 // step

    def col(g):
        return pl.BlockSpec((step, WIDTH), lambda b, t: (base + b * nt + t, g))

    st_spec = pl.BlockSpec((None, N_HEADS, HEAD_DIM, HEAD_DIM), lambda b, t: (b, 0, 0, 0))
    tri = jnp.tril(jnp.ones((chunk, chunk), BF16))
    return pl.pallas_call(
        functools.partial(_gla_kernel, chunk=chunk, n_chunks=n_chunks),
        grid=(batch, nt),
        in_specs=[col(0), col(1), col(2), col(3),
                  pl.BlockSpec(lb_logits.shape, lambda b, t: (0, 0)),
                  pl.BlockSpec((1, HEAD_DIM), lambda b, t: (0, 0)),
                  st_spec,
                  pl.BlockSpec((chunk, chunk), lambda b, t: (0, 0))],
        out_specs=[pl.BlockSpec((step, WIDTH), lambda b, t: (b * nt + t, 0)), st_spec],
        out_shape=[jax.ShapeDtypeStruct((batch * length, WIDTH), BF16),
                   jax.ShapeDtypeStruct((batch, N_HEADS, HEAD_DIM, HEAD_DIM), F32)],
        scratch_shapes=[pltpu.VMEM((N_HEADS, HEAD_DIM, HEAD_DIM), F32)],
        compiler_params=_params("parallel", "arbitrary"),
        name="hgrn2",
    )(a_proj, a_proj, a_proj, a_proj, lb_logits, gnorm, s0, tri)


def _attn_prompt_kernel(q_ref, k_ref, v_ref, c_ref, o_ref,
                        vt_ref, cb_ref, s0_ref, s1_ref, m_ref, l_ref, acc_ref, *, blk):
    h, iq = pl.program_id(1), pl.program_id(2)
    length = k_ref.shape[0]
    halves = (slice(0, blk // 2), slice(blk // 2, blk))

    @pl.when(iq == 0)
    def _():
        for r0 in range(0, length, blk):
            rs = slice(r0, r0 + blk)
            vt_ref[:, rs] = v_ref[rs, :].astype(F32).T.astype(BF16)
            c_cols = c_ref[:, rs].T
            onehot = lax.broadcasted_iota(jnp.int32, c_cols.shape, 1) == h
            cb_ref[rs, :] = jnp.broadcast_to(
                jnp.sum(jnp.where(onehot, c_cols, 0.0), axis=1, keepdims=True), (blk, LANES))

    q = q_ref[...]
    m_ref[...] = jnp.full(m_ref.shape, -jnp.inf, F32)
    l_ref[...] = jnp.zeros(l_ref.shape, F32)
    acc_ref[...] = jnp.zeros(acc_ref.shape, F32)

    def scores(j, s_ref):
        ks = pl.ds(pl.multiple_of(j * blk, blk), blk)
        kb = k_ref[ks, :]
        cb = cb_ref[ks, :]
        for hv in halves:
            x = lax.dot_general(kb, q[hv, :], NT_DIMS, preferred_element_type=F32)
            s_ref[:, hv] = x - jnp.concatenate([cb] * (x.shape[1] // LANES), axis=1)

    def consume(j, s_ref, masked):
        vt = vt_ref[:, pl.ds(pl.multiple_of(j * blk, blk), blk)]
        p = []
        for hv in halves:
            keys = hv.stop if masked else blk
            x = s_ref[:keys, hv]
            if masked:
                key = lax.broadcasted_iota(jnp.int32, x.shape, 0)
                qry = lax.broadcasted_iota(jnp.int32, x.shape, 1) + hv.start
                x = jnp.where(key <= qry, x, -jnp.inf)
            m_prev = m_ref[:, hv]
            m_new = jnp.maximum(m_prev, jnp.max(x, axis=0, keepdims=True))
            alpha = jnp.exp2(m_prev - m_new)
            e = jnp.exp2(x - m_new)
            l_ref[:, hv] = alpha * l_ref[:, hv] + jnp.sum(e, axis=0, keepdims=True)
            m_ref[:, hv] = m_new
            acc_ref[:, hv] = alpha * acc_ref[:, hv]
            p.append(e.astype(BF16))
        for e, hv in zip(p, halves):
            acc_ref[:, hv] += jnp.dot(vt[:, :e.shape[0]], e, preferred_element_type=F32)

    scores(0, s0_ref)

    def pair(p, carry):
        j = 2 * p
        scores(j + 1, s1_ref)
        consume(j, s0_ref, False)
        scores(j + 2, s0_ref)
        consume(j + 1, s1_ref, False)
        return carry

    lax.fori_loop(0, iq // 2, pair, 0)

    @pl.when(iq % 2 == 0)
    def _():
        consume(iq, s0_ref, True)

    @pl.when(iq % 2 == 1)
    def _():
        scores(iq, s1_ref)
        consume(iq - 1, s0_ref, False)
        consume(iq, s1_ref, True)

    o_ref[...] = (acc_ref[...] / l_ref[...]).T.astype(o_ref.dtype)


def _fox_prompt(q, k, v, c, *, batch, length, blk):
    nq = length // blk
    qspec = pl.BlockSpec((blk, HEAD_DIM), lambda b, h, i: (b * nq + i, h))
    kvspec = pl.BlockSpec((length, HEAD_DIM), lambda b, h, i: (b, h))
    return pl.pallas_call(
        functools.partial(_attn_prompt_kernel, blk=blk),
        grid=(batch, N_HEADS, nq),
        in_specs=[qspec, kvspec, kvspec,
                  pl.BlockSpec((None,) + c.shape[1:], lambda b, h, i: (b, 0, 0))],
        out_specs=qspec,
        out_shape=jax.ShapeDtypeStruct((batch * length, WIDTH), BF16),
        scratch_shapes=[pltpu.VMEM((HEAD_DIM, length), BF16),
                        pltpu.VMEM((length, LANES), F32),
                        pltpu.VMEM((blk, blk), F32), pltpu.VMEM((blk, blk), F32),
                        pltpu.VMEM((1, blk), F32), pltpu.VMEM((1, blk), F32),
                        pltpu.VMEM((HEAD_DIM, blk), F32)],
        compiler_params=_params("parallel", "parallel", "arbitrary"),
        name="fox_prompt",
    )(q, k, v, c)


def _attend_heads(score_fn, value_fn, m_ref, l_ref, acc_ref):
    s = [score_fn(h) for h in range(N_HEADS)]
    stats = []
    for h in range(N_HEADS):
        m_prev = m_ref[h]
        m_new = jnp.maximum(m_prev, jnp.max(s[h], axis=-1, keepdims=True))
        stats.append((jnp.exp2(m_prev - m_new), m_new))
    p = [jnp.exp2(s[h] - stats[h][1]) for h in range(N_HEADS)]
    for h in range(N_HEADS):
        alpha, m_new = stats[h]
        l_ref[h] = alpha * l_ref[h] + jnp.sum(p[h], axis=-1, keepdims=True)
        m_ref[h] = m_new
        acc_ref[:, _head(h)] = alpha * acc_ref[:, _head(h)] + jnp.dot(
            p[h].astype(BF16), value_fn(h), preferred_element_type=F32)


def _attn_sample_kernel(q_ref, kc_hbm, vc_hbm, bc_ref, kn_ref, vn_ref, bn_ref, o_ref,
                        kbuf_ref, vbuf_ref, sem_ref, m_ref, l_ref, acc_ref):
    b, j = pl.program_id(0), pl.program_id(1)
    nj = pl.num_programs(1)
    tk = bc_ref.shape[1]
    step = b * nj + j
    slot = step % 2

    def cache_copies(stream, blk, into):
        return [pltpu.make_async_copy(hbm.at[stream, pl.ds(blk * tk, tk), h, :],
                                      buf.at[into, h], sem_ref.at[t, into])
                for t, (hbm, buf) in enumerate(((kc_hbm, kbuf_ref), (vc_hbm, vbuf_ref)))
                for h in range(N_HEADS)]

    def start_all(copies):
        for cp in copies:
            cp.start()

    @pl.when(step == 0)
    def _():
        start_all(cache_copies(b, j, slot))

    @pl.when(step + 1 < pl.num_programs(0) * nj)
    def _():
        start_all(cache_copies((step + 1) // nj, (step + 1) % nj, 1 - slot))

    @pl.when(j == 0)
    def _():
        m_ref[...] = jnp.full(m_ref.shape, -jnp.inf, F32)
        l_ref[...] = jnp.zeros(l_ref.shape, F32)
        acc_ref[...] = jnp.zeros(acc_ref.shape, F32)

    for cp in cache_copies(b, j, slot):
        cp.wait()
    _attend_heads(
        lambda h: lax.dot_general(q_ref[:, _head(h)], kbuf_ref[slot, h].astype(BF16),
                                  NT_DIMS, preferred_element_type=F32) + bc_ref[h:h + 1, :],
        lambda h: vbuf_ref[slot, h].astype(BF16), m_ref, l_ref, acc_ref)

    @pl.when(j == pl.num_programs(1) - 1)
    def _():
        tq = q_ref.shape[0]
        causal = (lax.broadcasted_iota(jnp.int32, (tq, tq), 1)
                  <= lax.broadcasted_iota(jnp.int32, (tq, tq), 0))
        _attend_heads(
            lambda h: jnp.where(causal,
                                lax.dot_general(q_ref[:, _head(h)], kn_ref[:, _head(h)], NT_DIMS,
                                                preferred_element_type=F32) - bn_ref[h:h + 1, :],
                                -jnp.inf),
            lambda h: vn_ref[:, _head(h)], m_ref, l_ref, acc_ref)
        for h in range(N_HEADS):
            o_ref[:, _head(h)] = (acc_ref[:, _head(h)] / l_ref[h]).astype(o_ref.dtype)


def _fox_sample(q, k_new, v_new, c_new, cache_k, cache_v, cache_bias, *, row0, tk):
    B, H, P = cache_bias.shape
    Ls = c_new.shape[2]
    base = row0 // Ls
    new = pl.BlockSpec((Ls, WIDTH), lambda b, j: (base + b, 0))
    cache = pl.BlockSpec(memory_space=pl.ANY)
    cache_buf = pltpu.VMEM((2, H, tk, HEAD_DIM), F32)
    return pl.pallas_call(
        _attn_sample_kernel,
        grid=(B, P // tk),
        in_specs=[new, cache, cache,
                  pl.BlockSpec((None, N_HEADS, tk), lambda b, j: (b, 0, j)),
                  new, new,
                  pl.BlockSpec((None, N_HEADS, Ls), lambda b, j: (b, 0, 0))],
        out_specs=pl.BlockSpec((Ls, WIDTH), lambda b, j: (b, 0)),
        out_shape=jax.ShapeDtypeStruct((B * Ls, WIDTH), BF16),
        scratch_shapes=[cache_buf, cache_buf, pltpu.SemaphoreType.DMA((2, 2)),
                        pltpu.VMEM((N_HEADS, Ls, 1), F32), pltpu.VMEM((N_HEADS, Ls, 1), F32),
                        pltpu.VMEM((Ls, WIDTH), F32)],
        compiler_params=_params("arbitrary", "arbitrary"),
        name="fox_sample",
    )(q, cache_k, cache_v, cache_bias, k_new, v_new, c_new)


MERGE_COLS = 512


def _merge_kernel(xp_ref, xs_ref, oap_ref, oas_ref, obp_ref, obs_ref, g_ref,
                  wpa_ref, wpb_ref, wo_ref, o_ref, mg_ref, *, n_first):
    first = pl.program_id(0) < n_first
    D = xp_ref.shape[1]
    oa = jnp.where(first, oap_ref[...], oas_ref[...])
    ob = jnp.where(first, obp_ref[...], obs_ref[...])
    cols = min(MERGE_COLS, D)
    for n0 in range(0, D, cols):
        ns = slice(n0, n0 + cols)
        pa = jnp.dot(oa, wpa_ref[:, ns], preferred_element_type=F32)
        pb = jnp.dot(ob, wpb_ref[:, ns], preferred_element_type=F32)
        ga = g_ref[:, ns].astype(F32)
        gb = g_ref[:, D + n0:D + n0 + cols].astype(F32)
        mg_ref[:, ns] = (ga * pa + gb * pb).astype(BF16)
    x = jnp.where(first, xp_ref[...], xs_ref[...])
    o_ref[...] = x + jnp.dot(mg_ref[...], wo_ref[...], preferred_element_type=F32)


def _merge_out(xp, xs, oa_p, oa_s, ob_p, ob_s, g, w_pa, w_pb, w_o, *, tm):
    D = xp.shape[1]
    T = xp.shape[0] + xs.shape[0]
    n_first = xp.shape[0] // tm
    fixed = lambda i: (0, 0)
    return pl.pallas_call(
        functools.partial(_merge_kernel, n_first=n_first),
        grid=(T // tm,),
        in_specs=[*_two_streams(tm, D, n_first), *_two_streams(tm, WIDTH, n_first),
                  *_two_streams(tm, WIDTH, n_first),
                  pl.BlockSpec((tm, 2 * D), lambda i: (i, 0)),
                  *[pl.BlockSpec(w.shape, fixed, pipeline_mode=pl.Buffered(1))
                    for w in (w_pa, w_pb, w_o)]],
        out_specs=pl.BlockSpec((tm, D), lambda i: (i, 0)),
        out_shape=jax.ShapeDtypeStruct((T, D), F32),
        scratch_shapes=[pltpu.VMEM((tm, D), BF16)],
        compiler_params=_params("parallel"),
        name="merge_out",
    )(xp, xs, oa_p, oa_s, ob_p, ob_s, g, w_pa, w_pb, w_o)


def _ffn_kernel(x_ref, g2_ref, w1_ref, w2_ref, gf_ref, yp_ref, ys_ref, h_ref, acc_ref, *, n_first):
    i, f = pl.program_id(0), pl.program_id(1)

    @pl.when(f == 0)
    def _():
        h_ref[...] = _normed(x_ref[...], g2_ref)
        acc_ref[...] = x_ref[...]

    u = jnp.maximum(jnp.dot(h_ref[...], w1_ref[...], preferred_element_type=F32), 0.0)
    acc_ref[...] += jnp.dot((u * u).astype(BF16), w2_ref[...], preferred_element_type=F32)

    @pl.when(f == pl.num_programs(1) - 1)
    def _():
        x = acc_ref[...]
        ms = jnp.mean(x * x, axis=-1, keepdims=True)
        y = x * lax.rsqrt(ms + EPS) * gf_ref[...]

        @pl.when(i < n_first)
        def _():
            yp_ref[...] = y

        @pl.when(i >= n_first)
        def _():
            ys_ref[...] = y


def _ffn(x, g2, w1, w2, gf, *, rows_first, tm, tf):
    T, D = x.shape
    F = w1.shape[1]
    n_first = rows_first // tm
    vec = pl.BlockSpec((1, D), lambda i, f: (0, 0))
    return pl.pallas_call(
        functools.partial(_ffn_kernel, n_first=n_first),
        grid=(T // tm, F // tf),
        in_specs=[pl.BlockSpec((tm, D), lambda i, f: (i, 0)), vec,
                  pl.BlockSpec((D, tf), lambda i, f: (0, f)),
                  pl.BlockSpec((tf, D), lambda i, f: (f, 0)), vec],
        out_specs=list(_two_streams(tm, D, n_first)),
        out_shape=[jax.ShapeDtypeStruct((rows_first, D), F32),
                   jax.ShapeDtypeStruct((T - rows_first, D), F32)],
        scratch_shapes=[pltpu.VMEM((tm, D), BF16), pltpu.VMEM((tm, D), F32)],
        compiler_params=_params("arbitrary", "arbitrary"),
        name="ffn",
    )(x, g2, w1, w2, gf)


def _tile(n, pref):
    t = min(pref, n)
    while n % t:
        t //= 2
    return t


def kernel(x_prompt, x_sample, cache_fox_k, cache_fox_v, cache_fox_logf, state_hgrn, norm1, w_in,
           b_fox_f, lb_logits, gnorm_a, w_pa, w_pb, w_o, norm2, w1, w2, norm_f):
    B, L, D = x_prompt.shape
    Bs, Ls, _ = x_sample.shape
    depth, _, P, H, DH = cache_fox_k.shape
    assert depth == 1 and H == N_HEADS and DH == HEAD_DIM
    Tp, Ts = B * L, Bs * Ls
    T = Tp + Ts
    W = WIDTH

    xp, xs = x_prompt.reshape(Tp, D), x_sample.reshape(Ts, D)
    w_inb = w_in[0].astype(BF16)
    w_fl_t = w_inb[:, 7 * W:7 * W + H].T
    w_g = w_inb[:, 7 * W + H:]
    g1 = norm1[0].reshape(1, D)

    tm = _tile(math.gcd(Tp, Ts), 512)
    tm1 = _tile(math.gcd(Tp, Ts), 1024)
    h1 = _rmsnorm_rows(xp, xs, g1, tm=_tile(tm, 256))
    a_proj = _proj(h1, w_inb, n_cols=4 * W, act=None, out_dtype=F32, tm=tm1, tn=W)
    gates = _proj(h1, w_g, n_cols=2 * D, act="sigmoid", out_dtype=BF16, tm=tm1, tn=_tile(2 * D, W))
    q_b, k_b, v_b, k_p, k_s, v_p, v_s, fl_t = _proj_fox(h1, w_inb, w_fl_t, col0=4 * W,
                                                        rows_first=Tp, tm=_tile(tm, 512))

    bias = b_fox_f[0].reshape(H, 1)
    lf_p, c_p = _fox_gates(fl_t, bias, col0=0, cols=Tp, seg=L, block=L, pad_rows=LANES)
    lf_s, c_s = _fox_gates(fl_t, bias, col0=Tp, cols=Ts, seg=Ls, block=_tile(Ts, 1024), pad_rows=H)
    cache_bias = _cache_suffix(jnp.transpose(cache_fox_logf[0], (0, 2, 1)))

    zeros_state = jnp.zeros((B, H, DH, DH), F32)
    chunk_p = _tile(L, 64)
    oa_p, s_p = _hgrn2(a_proj, lb_logits, gnorm_a, zeros_state, batch=B, length=L, row0=0,
                       chunk=chunk_p, n_chunks=_tile(L // chunk_p, 4))
    oa_s, s_s = _hgrn2(a_proj, lb_logits, gnorm_a, state_hgrn[0], batch=Bs, length=Ls, row0=Tp,
                       chunk=_tile(Ls, 64), n_chunks=1)

    ob_p = _fox_prompt(q_b, k_b, v_b, c_p, batch=B, length=L, blk=_tile(L, 512))
    c_s3 = c_s.transpose(1, 0, 2).reshape(H, Bs, Ls).transpose(1, 0, 2)
    ob_s = _fox_sample(q_b, k_b, v_b, c_s3, cache_fox_k[0], cache_fox_v[0], cache_bias,
                       row0=Tp, tk=_tile(P, 1024))

    tm5 = _tile(tm, 512)
    x1 = _merge_out(xp, xs, oa_p, oa_s, ob_p, ob_s, gates, w_pa[0].astype(BF16),
                    w_pb[0].astype(BF16), w_o[0].astype(BF16), tm=_tile(tm, 256))
    y_p, y_s = _ffn(x1, norm2[0].reshape(1, D), w1[0].astype(BF16), w2[0].astype(BF16),
                    norm_f.reshape(1, D), rows_first=Tp, tm=tm5, tf=_tile(w1.shape[-1], 1024))

    return (y_p.reshape(B, L, D), y_s.reshape(Bs, Ls, D),
            k_p.reshape(1, B, L, H, DH), v_p.reshape(1, B, L, H, DH),
            lf_p.T.reshape(1, B, L, H), s_p[None],
            k_s.reshape(1, Bs, Ls, H, DH), v_s.reshape(1, Bs, Ls, H, DH),
            lf_s.T.reshape(1, Bs, Ls, H), s_s[None])
```

```python
import functools
import math

import jax
import jax.numpy as jnp
from jax import lax
from jax.experimental import pallas as pl
from jax.experimental.pallas import tpu as pltpu

F32 = jnp.float32
BF16 = jnp.bfloat16

EPS = 1e-6
N_HEADS = 8
HEAD_DIM = 128
WIDTH = N_HEADS * HEAD_DIM
LANES = 128
VMEM_LIMIT = 56 * 1024 * 1024
LOG2E = math.log2(math.e)
QK_SCALE = HEAD_DIM ** -0.5 * LOG2E

NT_DIMS = (((1,), (1,)), ((), ()))
TN_DIMS = (((0,), (0,)), ((), ()))


def _params(*sem):
    return pltpu.CompilerParams(dimension_semantics=sem, vmem_limit_bytes=VMEM_LIMIT)


def _sigmoid(x):
    return 1.0 / (1.0 + jnp.exp(-x))


def _head(h):
    return slice(h * HEAD_DIM, (h + 1) * HEAD_DIM)


def _two_streams(rows, width, n_first, n_second=None):
    mode = {"pipeline_mode": pl.Buffered(1)} if n_second == 1 else {}
    return (pl.BlockSpec((rows, width), lambda i, *_: (jnp.minimum(i, n_first - 1), 0)),
            pl.BlockSpec((rows, width), lambda i, *_: (jnp.maximum(i - n_first, 0), 0), **mode))


def _head_rows(h, rows):
    return pl.ds(h, rows, stride=N_HEADS)


def _normed(x, g_ref):
    ms = jnp.mean(x * x, axis=-1, keepdims=True)
    return (x * lax.rsqrt(ms + EPS) * g_ref[...]).astype(BF16)


def _rmsnorm_kernel(xp_ref, xs_ref, g_ref, o_ref, *, n_first):
    @pl.when(pl.program_id(0) < n_first)
    def _():
        o_ref[...] = _normed(xp_ref[...], g_ref)

    @pl.when(pl.program_id(0) >= n_first)
    def _():
        o_ref[...] = _normed(xs_ref[...], g_ref)


def _rmsnorm_rows(xp, xs, g, *, tm):
    D = xp.shape[1]
    n_first = xp.shape[0] // tm
    T = xp.shape[0] + xs.shape[0]
    return pl.pallas_call(
        functools.partial(_rmsnorm_kernel, n_first=n_first),
        grid=(T // tm,),
        in_specs=[*_two_streams(tm, D, n_first), pl.BlockSpec((1, D), lambda i: (0, 0))],
        out_specs=pl.BlockSpec((tm, D), lambda i: (i, 0)),
        out_shape=jax.ShapeDtypeStruct((T, D), BF16),
        compiler_params=_params("parallel"),
        name="rmsnorm_rows",
    )(xp, xs, g)


def _proj_kernel(h_ref, w_ref, o_ref, *, act):
    acc = jnp.dot(h_ref[...], w_ref[...], preferred_element_type=F32)
    if act == "sigmoid":
        acc = _sigmoid(acc)
    o_ref[...] = acc.astype(o_ref.dtype)


def _proj(h, w, *, n_cols, act, out_dtype, tm, tn):
    T, D = h.shape
    return pl.pallas_call(
        functools.partial(_proj_kernel, act=act),
        grid=(T // tm, n_cols // tn),
        in_specs=[pl.BlockSpec((tm, D), lambda i, j: (i, 0)),
                  pl.BlockSpec((D, tn), lambda i, j: (0, j))],
        out_specs=pl.BlockSpec((tm, tn), lambda i, j: (i, j)),
        out_shape=jax.ShapeDtypeStruct((T, n_cols), out_dtype),
        compiler_params=_params("parallel", "arbitrary"),
        name="proj_" + (act or "lin"),
    )(h, w)


def _proj_fox_kernel(h_ref, w_ref, wfl_ref,
                     q_ref, kb_ref, vb_ref, kp_ref, ks_ref, vp_ref, vs_ref, fl_ref, *, n_first):
    i, j = pl.program_id(0), pl.program_id(1)

    @pl.when(j == 0)
    def _():
        fl_ref[...] = lax.dot_general(wfl_ref[...], h_ref[...], NT_DIMS,
                                      preferred_element_type=F32)

    h = h_ref[...]
    tm = h.shape[0]
    pair = 2 * HEAD_DIM

    def column_pairs():
        for c in range(WIDTH // pair):
            cs = slice(c * pair, (c + 1) * pair)
            yield c, cs, jnp.dot(h, w_ref[:, cs], preferred_element_type=F32)

    @pl.when(j == 0)
    def _():
        for _, cs, acc in column_pairs():
            q_ref[:, cs] = (acc * QK_SCALE).astype(BF16)

    def key_or_value(bf_ref, heads_ref):
        for c, cs, acc in column_pairs():
            bf_ref[:, cs] = acc.astype(BF16)
            for hh in range(2):
                heads_ref[_head_rows(2 * c + hh, tm), :] = acc[:, _head(hh)]

    for jj, bf_ref, first_ref, second_ref in ((1, kb_ref, kp_ref, ks_ref),
                                              (2, vb_ref, vp_ref, vs_ref)):
        pl.when((j == jj) & (i < n_first))(
            functools.partial(key_or_value, bf_ref, first_ref))
        pl.when((j == jj) & (i >= n_first))(
            functools.partial(key_or_value, bf_ref, second_ref))


def _proj_fox(h, w, w_fl_t, *, col0, rows_first, tm):
    T, D = h.shape
    Tp, Ts = rows_first, T - rows_first
    n_first = Tp // tm
    kv_p, kv_s = _two_streams(tm * N_HEADS, HEAD_DIM, n_first)
    kv_shape = lambda rows: jax.ShapeDtypeStruct((rows * N_HEADS, HEAD_DIM), F32)
    tok = pl.BlockSpec((tm, WIDTH), lambda i, j: (i, 0))
    tok_shape = jax.ShapeDtypeStruct((T, WIDTH), BF16)
    return pl.pallas_call(
        functools.partial(_proj_fox_kernel, n_first=n_first),
        grid=(T // tm, 3),
        in_specs=[pl.BlockSpec((tm, D), lambda i, j: (i, 0)),
                  pl.BlockSpec((D, WIDTH), lambda i, j: (0, col0 // WIDTH + j)),
                  pl.BlockSpec((N_HEADS, D), lambda i, j: (0, 0))],
        out_specs=[tok, tok, tok, kv_p, kv_s, kv_p, kv_s,
                   pl.BlockSpec((N_HEADS, tm), lambda i, j: (0, i))],
        out_shape=[tok_shape, tok_shape, tok_shape,
                   kv_shape(Tp), kv_shape(Ts), kv_shape(Tp), kv_shape(Ts),
                   jax.ShapeDtypeStruct((N_HEADS, T), F32)],
        compiler_params=_params("arbitrary", "arbitrary"),
        name="proj_fox",
    )(h, w, w_fl_t)


def _lane_cumsum(x, seg):
    pos = lax.broadcasted_iota(jnp.int32, x.shape, x.ndim - 1) & (seg - 1)
    shift = 1
    while shift < seg:
        x = x + jnp.where(pos >= shift, pltpu.roll(x, shift, x.ndim - 1), 0.0)
        shift *= 2
    return x


def _gate_kernel(fl_ref, bias_ref, lf_ref, c_ref, *, seg):
    z = fl_ref[...] + bias_ref[...]
    lf = jnp.minimum(z, 0.0) - jnp.log(1.0 + jnp.exp(-jnp.abs(z)))
    lf_ref[...] = lf
    c = _lane_cumsum(lf, seg) * LOG2E
    c_ref[...] = jnp.zeros(c_ref.shape, F32)
    c_ref[0:c.shape[0], :] = c


def _fox_gates(fl_t, bias, *, col0, cols, seg, block, pad_rows):
    H = fl_t.shape[0]
    nb = cols // block
    base = col0 // block
    return pl.pallas_call(
        functools.partial(_gate_kernel, seg=seg),
        grid=(nb,),
        in_specs=[pl.BlockSpec((H, block), lambda i: (0, base + i)),
                  pl.BlockSpec((H, 1), lambda i: (0, 0))],
        out_specs=[pl.BlockSpec((H, block), lambda i: (0, i)),
                   pl.BlockSpec((None, pad_rows, block), lambda i: (i, 0, 0))],
        out_shape=[jax.ShapeDtypeStruct((H, cols), F32),
                   jax.ShapeDtypeStruct((nb, pad_rows, block), F32)],
        compiler_params=_params("parallel"),
        name="fox_gates",
    )(fl_t, bias)


def _suffix_kernel(lf_ref, o_ref):
    lf = lf_ref[...]
    c = _lane_cumsum(lf, lf.shape[-1])
    o_ref[...] = (c[:, -1:] - c) * LOG2E


def _cache_suffix(clf_t):
    B, H, P = clf_t.shape
    spec = pl.BlockSpec((None, H, P), lambda b: (b, 0, 0))
    return pl.pallas_call(
        _suffix_kernel,
        grid=(B,),
        in_specs=[spec],
        out_specs=spec,
        out_shape=jax.ShapeDtypeStruct((B, H, P), F32),
        compiler_params=_params("parallel"),
        name="fox_cache_suffix",
    )(clf_t)


SUB = 16


def _gla_kernel(aq_ref, af_ref, ai_ref, ag_ref, lbl_ref, gn_ref, s0_ref, tri_ref,
                o_ref, s_out_ref, st_ref, *, chunk, n_chunks):
    t = pl.program_id(1)

    @pl.when(t == 0)
    def _():
        for h in range(N_HEADS):
            st_ref[h] = s0_ref[h].T

    lbl = lbl_ref[...]
    e = jnp.exp(lbl - jnp.max(lbl, axis=0, keepdims=True))
    lb = e[0:1] / jnp.sum(e, axis=0, keepdims=True)
    gn = gn_ref[...]
    tri = tri_ref[...]
    n_sub = chunk // SUB

    def one_chunk(ci, carry):
        r = pl.ds(pl.multiple_of(ci * chunk, chunk), chunk)
        fa = lb + (1.0 - lb) * _sigmoid(af_ref[r, :])
        lf = jnp.log(fa)
        ka = 1.0 - fa
        aq = aq_ref[r, :]
        qa = aq * _sigmoid(aq)
        v = ai_ref[r, :].astype(BF16)
        ag = ag_ref[r, :]
        gate = ag * _sigmoid(ag)

        hi = lf.astype(BF16)
        r1 = lf - hi.astype(F32)
        mid = r1.astype(BF16)
        lo = (r1 - mid.astype(F32)).astype(BF16)
        b = (jnp.dot(tri, hi, preferred_element_type=F32)
             + jnp.dot(tri, mid, preferred_element_type=F32)
             + jnp.dot(tri, lo, preferred_element_type=F32))
        b_end = b[chunk - 1:chunk, :]

        q_in = (qa * jnp.exp(b)).astype(BF16)
        k_out = (ka * jnp.exp(b_end - b)).astype(BF16)
        decay = jnp.exp(b_end)
        qp, kp = [], []
        for i in range(n_sub):
            r0, n = i * SUB, (i + 1) * SUB
            m_i = b[r0 + SUB // 2:r0 + SUB // 2 + 1, :]
            qp.append((qa[r0:n] * jnp.exp(b[r0:n] - m_i)).astype(BF16))
            kp.append((ka[:n] * jnp.exp(m_i - b[:n])).astype(BF16))

        st = [st_ref[h] for h in range(N_HEADS)]
        inter = [lax.dot_general(q_in[:, _head(h)], st[h].astype(BF16), NT_DIMS,
                                 preferred_element_type=F32) for h in range(N_HEADS)]
        att = [[lax.dot_general(qp[i][:, _head(h)], kp[i][:, _head(h)], NT_DIMS,
                                preferred_element_type=F32) for h in range(N_HEADS)]
               for i in range(n_sub)]
        for h in range(N_HEADS):
            st_ref[h] = st[h] * decay[:, _head(h)] + lax.dot_general(
                v[:, _head(h)], k_out[:, _head(h)], TN_DIMS, preferred_element_type=F32)
        for i in range(n_sub):
            r0, n = i * SUB, (i + 1) * SUB
            causal = (lax.broadcasted_iota(jnp.int32, (SUB, n), 1)
                      <= lax.broadcasted_iota(jnp.int32, (SUB, n), 0) + r0)
            att[i] = [jnp.where(causal, a, 0.0).astype(BF16) for a in att[i]]
        for h in range(N_HEADS):
            o = jnp.concatenate(
                [inter[h][i * SUB:(i + 1) * SUB]
                 + jnp.dot(att[i][h], v[:(i + 1) * SUB, _head(h)], preferred_element_type=F32)
                 for i in range(n_sub)], axis=0)
            ms = jnp.mean(o * o, axis=-1, keepdims=True)
            o_ref[r, _head(h)] = (o * lax.rsqrt(ms + EPS) * gn * gate[:, _head(h)]).astype(o_ref.dtype)
        return carry

    lax.fori_loop(0, n_chunks, one_chunk, 0)

    @pl.when(t == pl.num_programs(1) - 1)
    def _():
        for h in range(N_HEADS):
            s_out_ref[h] = st_ref[h].T


def _hgrn2(a_proj, lb_logits, gnorm, s0, *, batch, length, row0, chunk, n_chunks):
    step = chunk * n_chunks
    nt = length // step
    base = row0 // step

    def col(g):
        return pl.BlockSpec((step, WIDTH), lambda b, t: (base + b * nt + t, g))

    st_spec = pl.BlockSpec((None, N_HEADS, HEAD_DIM, HEAD_DIM), lambda b, t: (b, 0, 0, 0))
    tri = jnp.tril(jnp.ones((chunk, chunk), BF16))
    return pl.pallas_call(
        functools.partial(_gla_kernel, chunk=chunk, n_chunks=n_chunks),
        grid=(batch, nt),
        in_specs=[col(0), col(1), col(2), col(3),
                  pl.BlockSpec(lb_logits.shape, lambda b, t: (0, 0)),
                  pl.BlockSpec((1, HEAD_DIM), lambda b, t: (0, 0)),
                  st_spec,
                  pl.BlockSpec((chunk, chunk), lambda b, t: (0, 0))],
        out_specs=[pl.BlockSpec((step, WIDTH), lambda b, t: (b * nt + t, 0)), st_spec],
        out_shape=[jax.ShapeDtypeStruct((batch * length, WIDTH), BF16),
                   jax.ShapeDtypeStruct((batch, N_HEADS, HEAD_DIM, HEAD_DIM), F32)],
        scratch_shapes=[pltpu.VMEM((N_HEADS, HEAD_DIM, HEAD_DIM), F32)],
        compiler_params=_params("parallel", "arbitrary"),
        name="hgrn2",
    )(a_proj, a_proj, a_proj, a_proj, lb_logits, gnorm, s0, tri)


def _attn_prompt_kernel(q_ref, k_ref, v_ref, c_ref, o_ref,
                        vt_ref, cb_ref, s_ref, m_ref, l_ref, acc_ref, *, blk):
    h = pl.program_id(1)
    length = k_ref.shape[0]
    halves = (slice(0, blk // 2), slice(blk // 2, blk))

    for r0 in range(0, length, blk):
        rs = slice(r0, r0 + blk)
        vt_ref[:, rs] = v_ref[rs, :].astype(F32).T.astype(BF16)
        c_cols = c_ref[:, rs].T
        onehot = lax.broadcasted_iota(jnp.int32, c_cols.shape, 1) == h
        cb_ref[rs, :] = jnp.broadcast_to(
            jnp.sum(jnp.where(onehot, c_cols, 0.0), axis=1, keepdims=True), (blk, LANES))

    def scores(iq, j, buf):
        ks = slice(j * blk, (j + 1) * blk)
        kb = k_ref[ks, :]
        cb = cb_ref[ks, :]
        for hv in halves:
            x = lax.dot_general(kb, q_ref[iq * blk + hv.start:iq * blk + hv.stop, :], NT_DIMS,
                                preferred_element_type=F32)
            s_ref[buf, :, hv] = x - jnp.concatenate([cb] * (x.shape[1] // LANES), axis=1)

    def consume(iq, j, buf):
        st = iq % 2
        masked = j == iq
        vt = vt_ref[:, j * blk:(j + 1) * blk]
        p = []
        for hv in halves:
            keys = hv.stop if masked else blk
            x = s_ref[buf, :keys, hv]
            if masked:
                key = lax.broadcasted_iota(jnp.int32, x.shape, 0)
                qry = lax.broadcasted_iota(jnp.int32, x.shape, 1) + hv.start
                x = jnp.where(key <= qry, x, -jnp.inf)
            if j == 0:
                m_new = jnp.max(x, axis=0, keepdims=True)
                e = jnp.exp2(x - m_new)
                l_ref[st, :, hv] = jnp.sum(e, axis=0, keepdims=True)
            else:
                m_prev = m_ref[st, :, hv]
                m_new = jnp.maximum(m_prev, jnp.max(x, axis=0, keepdims=True))
                alpha = jnp.exp2(m_prev - m_new)
                e = jnp.exp2(x - m_new)
                l_ref[st, :, hv] = alpha * l_ref[st, :, hv] + jnp.sum(e, axis=0, keepdims=True)
                acc_ref[st, :, hv] = alpha * acc_ref[st, :, hv]
            m_ref[st, :, hv] = m_new
            p.append(e.astype(BF16))
        for e, hv in zip(p, halves):
            pv = jnp.dot(vt[:, :e.shape[0]], e, preferred_element_type=F32)
            if j == 0:
                acc_ref[st, :, hv] = pv
            else:
                acc_ref[st, :, hv] += pv

    visits = [(iq, j) for iq in range(length // blk) for j in range(iq + 1)]
    scores(*visits[0], 0)
    for n, (iq, j) in enumerate(visits):
        if n + 1 < len(visits):
            scores(*visits[n + 1], (n + 1) % 2)
        consume(iq, j, n % 2)
        if j == iq:
            st = iq % 2
            o_ref[iq * blk:(iq + 1) * blk, :] = (acc_ref[st] / l_ref[st]).T.astype(o_ref.dtype)


def _fox_prompt(q, k, v, c, *, batch, length, blk):
    spec = pl.BlockSpec((length, HEAD_DIM), lambda b, h: (b, h))
    return pl.pallas_call(
        functools.partial(_attn_prompt_kernel, blk=blk),
        grid=(batch, N_HEADS),
        in_specs=[spec, spec, spec, pl.BlockSpec((None,) + c.shape[1:], lambda b, h: (b, 0, 0))],
        out_specs=spec,
        out_shape=jax.ShapeDtypeStruct((batch * length, WIDTH), BF16),
        scratch_shapes=[pltpu.VMEM((HEAD_DIM, length), BF16),
                        pltpu.VMEM((length, LANES), F32),
                        pltpu.VMEM((2, blk, blk), F32),
                        pltpu.VMEM((2, 1, blk), F32), pltpu.VMEM((2, 1, blk), F32),
                        pltpu.VMEM((2, HEAD_DIM, blk), F32)],
        compiler_params=_params("parallel", "parallel"),
        name="fox_prompt",
    )(q, k, v, c)


def _attend_heads(score_fn, value_fn, m_ref, l_ref, acc_ref):
    s = [score_fn(h) for h in range(N_HEADS)]
    stats = []
    for h in range(N_HEADS):
        m_prev = m_ref[h]
        m_new = jnp.maximum(m_prev, jnp.max(s[h], axis=-1, keepdims=True))
        stats.append((jnp.exp2(m_prev - m_new), m_new))
    p = [jnp.exp2(s[h] - stats[h][1]) for h in range(N_HEADS)]
    for h in range(N_HEADS):
        alpha, m_new = stats[h]
        l_ref[h] = alpha * l_ref[h] + jnp.sum(p[h], axis=-1, keepdims=True)
        m_ref[h] = m_new
        acc_ref[:, _head(h)] = alpha * acc_ref[:, _head(h)] + jnp.dot(
            p[h].astype(BF16), value_fn(h), preferred_element_type=F32)


def _attn_sample_kernel(q_ref, kc_hbm, vc_hbm, bc_ref, kn_ref, vn_ref, bn_ref, o_ref,
                        kbuf_ref, vbuf_ref, sem_ref, m_ref, l_ref, acc_ref):
    b, j = pl.program_id(0), pl.program_id(1)
    nj = pl.num_programs(1)
    tk = bc_ref.shape[1]
    step = b * nj + j
    slot = step % 2

    def cache_copies(stream, blk, into):
        return [pltpu.make_async_copy(hbm.at[stream, pl.ds(blk * tk, tk), h, :],
                                      buf.at[into, h], sem_ref.at[t, into])
                for t, (hbm, buf) in enumerate(((kc_hbm, kbuf_ref), (vc_hbm, vbuf_ref)))
                for h in range(N_HEADS)]

    def start_all(copies):
        for cp in copies:
            cp.start()

    @pl.when(step == 0)
    def _():
        start_all(cache_copies(b, j, slot))

    @pl.when(step + 1 < pl.num_programs(0) * nj)
    def _():
        start_all(cache_copies((step + 1) // nj, (step + 1) % nj, 1 - slot))

    @pl.when(j == 0)
    def _():
        m_ref[...] = jnp.full(m_ref.shape, -jnp.inf, F32)
        l_ref[...] = jnp.zeros(l_ref.shape, F32)
        acc_ref[...] = jnp.zeros(acc_ref.shape, F32)

    for cp in cache_copies(b, j, slot):
        cp.wait()
    _attend_heads(
        lambda h: lax.dot_general(q_ref[:, _head(h)], kbuf_ref[slot, h].astype(BF16),
                                  NT_DIMS, preferred_element_type=F32) + bc_ref[h:h + 1, :],
        lambda h: vbuf_ref[slot, h].astype(BF16), m_ref, l_ref, acc_ref)

    @pl.when(j == pl.num_programs(1) - 1)
    def _():
        tq = q_ref.shape[0]
        causal = (lax.broadcasted_iota(jnp.int32, (tq, tq), 1)
                  <= lax.broadcasted_iota(jnp.int32, (tq, tq), 0))
        _attend_heads(
            lambda h: jnp.where(causal,
                                lax.dot_general(q_ref[:, _head(h)], kn_ref[:, _head(h)], NT_DIMS,
                                                preferred_element_type=F32) - bn_ref[h:h + 1, :],
                                -jnp.inf),
            lambda h: vn_ref[:, _head(h)], m_ref, l_ref, acc_ref)
        for h in range(N_HEADS):
            o_ref[:, _head(h)] = (acc_ref[:, _head(h)] / l_ref[h]).astype(o_ref.dtype)


def _fox_sample(q, k_new, v_new, c_new, cache_k, cache_v, cache_bias, *, row0, tk):
    B, H, P = cache_bias.shape
    Ls = c_new.shape[2]
    base = row0 // Ls
    new = pl.BlockSpec((Ls, WIDTH), lambda b, j: (base + b, 0))
    cache = pl.BlockSpec(memory_space=pl.ANY)
    cache_buf = pltpu.VMEM((2, H, tk, HEAD_DIM), F32)
    return pl.pallas_call(
        _attn_sample_kernel,
        grid=(B, P // tk),
        in_specs=[new, cache, cache,
                  pl.BlockSpec((None, N_HEADS, tk), lambda b, j: (b, 0, j)),
                  new, new,
                  pl.BlockSpec((None, N_HEADS, Ls), lambda b, j: (b, 0, 0))],
        out_specs=pl.BlockSpec((Ls, WIDTH), lambda b, j: (b, 0)),
        out_shape=jax.ShapeDtypeStruct((B * Ls, WIDTH), BF16),
        scratch_shapes=[cache_buf, cache_buf, pltpu.SemaphoreType.DMA((2, 2)),
                        pltpu.VMEM((N_HEADS, Ls, 1), F32), pltpu.VMEM((N_HEADS, Ls, 1), F32),
                        pltpu.VMEM((Ls, WIDTH), F32)],
        compiler_params=_params("arbitrary", "arbitrary"),
        name="fox_sample",
    )(q, cache_k, cache_v, cache_bias, k_new, v_new, c_new)


MERGE_COLS = 512


def _merge_kernel(xp_ref, xs_ref, oap_ref, oas_ref, obp_ref, obs_ref, g_ref,
                  wpa_ref, wpb_ref, wo_ref, o_ref, mg_ref, *, n_first):
    first = pl.program_id(0) < n_first
    D = xp_ref.shape[1]
    oa = jnp.where(first, oap_ref[...], oas_ref[...])
    ob = jnp.where(first, obp_ref[...], obs_ref[...])
    cols = min(MERGE_COLS, D)
    for n0 in range(0, D, cols):
        ns = slice(n0, n0 + cols)
        pa = jnp.dot(oa, wpa_ref[:, ns], preferred_element_type=F32)
        pb = jnp.dot(ob, wpb_ref[:, ns], preferred_element_type=F32)
        ga = g_ref[:, ns].astype(F32)
        gb = g_ref[:, D + n0:D + n0 + cols].astype(F32)
        mg_ref[:, ns] = (ga * pa + gb * pb).astype(BF16)
    x = jnp.where(first, xp_ref[...], xs_ref[...])
    o_ref[...] = x + jnp.dot(mg_ref[...], wo_ref[...], preferred_element_type=F32)


def _merge_out(xp, xs, oa_p, oa_s, ob_p, ob_s, g, w_pa, w_pb, w_o, *, tm):
    D = xp.shape[1]
    T = xp.shape[0] + xs.shape[0]
    n_first = xp.shape[0] // tm
    fixed = lambda i: (0, 0)
    return pl.pallas_call(
        functools.partial(_merge_kernel, n_first=n_first),
        grid=(T // tm,),
        in_specs=[*_two_streams(tm, D, n_first), *_two_streams(tm, WIDTH, n_first),
                  *_two_streams(tm, WIDTH, n_first),
                  pl.BlockSpec((tm, 2 * D), lambda i: (i, 0)),
                  *[pl.BlockSpec(w.shape, fixed, pipeline_mode=pl.Buffered(1))
                    for w in (w_pa, w_pb, w_o)]],
        out_specs=pl.BlockSpec((tm, D), lambda i: (i, 0)),
        out_shape=jax.ShapeDtypeStruct((T, D), F32),
        scratch_shapes=[pltpu.VMEM((tm, D), BF16)],
        compiler_params=_params("parallel"),
        name="merge_out",
    )(xp, xs, oa_p, oa_s, ob_p, ob_s, g, w_pa, w_pb, w_o)


def _ffn_kernel(x_ref, g2_ref, w1_ref, w2_ref, gf_ref, yp_ref, ys_ref, h_ref, acc_ref, *, n_first):
    i, f = pl.program_id(0), pl.program_id(1)

    @pl.when(f == 0)
    def _():
        h_ref[...] = _normed(x_ref[...], g2_ref)
        acc_ref[...] = x_ref[...]

    u = jnp.maximum(jnp.dot(h_ref[...], w1_ref[...], preferred_element_type=F32), 0.0)
    acc_ref[...] += jnp.dot((u * u).astype(BF16), w2_ref[...], preferred_element_type=F32)

    @pl.when(f == pl.num_programs(1) - 1)
    def _():
        x = acc_ref[...]
        ms = jnp.mean(x * x, axis=-1, keepdims=True)
        y = x * lax.rsqrt(ms + EPS) * gf_ref[...]

        @pl.when(i < n_first)
        def _():
            yp_ref[...] = y

        @pl.when(i >= n_first)
        def _():
            ys_ref[...] = y


def _ffn(x, g2, w1, w2, gf, *, rows_first, tm, tf):
    T, D = x.shape
    F = w1.shape[1]
    n_first = rows_first // tm
    vec = pl.BlockSpec((1, D), lambda i, f: (0, 0))
    return pl.pallas_call(
        functools.partial(_ffn_kernel, n_first=n_first),
        grid=(T // tm, F // tf),
        in_specs=[pl.BlockSpec((tm, D), lambda i, f: (i, 0)), vec,
                  pl.BlockSpec((D, tf), lambda i, f: (0, f)),
                  pl.BlockSpec((tf, D), lambda i, f: (f, 0)), vec],
        out_specs=list(_two_streams(tm, D, n_first)),
        out_shape=[jax.ShapeDtypeStruct((rows_first, D), F32),
                   jax.ShapeDtypeStruct((T - rows_first, D), F32)],
        scratch_shapes=[pltpu.VMEM((tm, D), BF16), pltpu.VMEM((tm, D), F32)],
        compiler_params=_params("arbitrary", "arbitrary"),
        name="ffn",
    )(x, g2, w1, w2, gf)


def _tile(n, pref):
    t = min(pref, n)
    while n % t:
        t //= 2
    return t


def kernel(x_prompt, x_sample, cache_fox_k, cache_fox_v, cache_fox_logf, state_hgrn, norm1, w_in,
           b_fox_f, lb_logits, gnorm_a, w_pa, w_pb, w_o, norm2, w1, w2, norm_f):
    B, L, D = x_prompt.shape
    Bs, Ls, _ = x_sample.shape
    depth, _, P, H, DH = cache_fox_k.shape
    assert depth == 1 and H == N_HEADS and DH == HEAD_DIM
    Tp, Ts = B * L, Bs * Ls
    T = Tp + Ts
    W = WIDTH

    xp, xs = x_prompt.reshape(Tp, D), x_sample.reshape(Ts, D)
    w_inb = w_in[0].astype(BF16)
    w_fl_t = w_inb[:, 7 * W:7 * W + H].T
    w_g = w_inb[:, 7 * W + H:]
    g1 = norm1[0].reshape(1, D)

    tm = _tile(math.gcd(Tp, Ts), 512)
    tm1 = _tile(math.gcd(Tp, Ts), 1024)
    h1 = _rmsnorm_rows(xp, xs, g1, tm=_tile(tm, 256))
    a_proj = _proj(h1, w_inb, n_cols=4 * W, act=None, out_dtype=F32, tm=tm1, tn=W)
    gates = _proj(h1, w_g, n_cols=2 * D, act="sigmoid", out_dtype=BF16, tm=tm1, tn=_tile(2 * D, W))
    q_b, k_b, v_b, k_p, k_s, v_p, v_s, fl_t = _proj_fox(h1, w_inb, w_fl_t, col0=4 * W,
                                                        rows_first=Tp, tm=_tile(tm, 512))

    bias = b_fox_f[0].reshape(H, 1)
    lf_p, c_p = _fox_gates(fl_t, bias, col0=0, cols=Tp, seg=L, block=L, pad_rows=LANES)
    lf_s, c_s = _fox_gates(fl_t, bias, col0=Tp, cols=Ts, seg=Ls, block=_tile(Ts, 1024), pad_rows=H)
    cache_bias = _cache_suffix(jnp.transpose(cache_fox_logf[0], (0, 2, 1)))

    zeros_state = jnp.zeros((B, H, DH, DH), F32)
    chunk_p = _tile(L, 64)
    oa_p, s_p = _hgrn2(a_proj, lb_logits, gnorm_a, zeros_state, batch=B, length=L, row0=0,
                       chunk=chunk_p, n_chunks=_tile(L // chunk_p, 4))
    oa_s, s_s = _hgrn2(a_proj, lb_logits, gnorm_a, state_hgrn[0], batch=Bs, length=Ls, row0=Tp,
                       chunk=_tile(Ls, 64), n_chunks=1)

    ob_p = _fox_prompt(q_b, k_b, v_b, c_p, batch=B, length=L, blk=_tile(L, 512))
    c_s3 = c_s.transpose(1, 0, 2).reshape(H, Bs, Ls).transpose(1, 0, 2)
    ob_s = _fox_sample(q_b, k_b, v_b, c_s3, cache_fox_k[0], cache_fox_v[0], cache_bias,
                       row0=Tp, tk=_tile(P, 1024))

    tm5 = _tile(tm, 512)
    x1 = _merge_out(xp, xs, oa_p, oa_s, ob_p, ob_s, gates, w_pa[0].astype(BF16),
                    w_pb[0].astype(BF16), w_o[0].astype(BF16), tm=_tile(tm, 256))
    y_p, y_s = _ffn(x1, norm2[0].reshape(1, D), w1[0].astype(BF16), w2[0].astype(BF16),
                    norm_f.reshape(1, D), rows_first=Tp, tm=tm5, tf=_tile(w1.shape[-1], 1024))

    return (y_p.reshape(B, L, D), y_s.reshape(Bs, Ls, D),
            k_p.reshape(1, B, L, H, DH), v_p.reshape(1, B, L, H, DH),
            lf_p.T.reshape(1, B, L, H), s_p[None],
            k_s.reshape(1, Bs, Ls, H, DH), v_s.reshape(1, Bs, Ls, H, DH),
            lf_s.T.reshape(1, Bs, Ls, H), s_s[None])
```

```python
import functools
import math

import jax
import jax.numpy as jnp
from jax import lax
from jax.experimental import pallas as pl
from jax.experimental.pallas import tpu as pltpu

F32 = jnp.float32
BF16 = jnp.bfloat16

EPS = 1e-6
N_HEADS = 8
HEAD_DIM = 128
WIDTH = N_HEADS * HEAD_DIM
LANES = 128
VMEM_LIMIT = 56 * 1024 * 1024
LOG2E = math.log2(math.e)
QK_SCALE = HEAD_DIM ** -0.5 * LOG2E

NT_DIMS = (((1,), (1,)), ((), ()))
TN_DIMS = (((0,), (0,)), ((), ()))


def _params(*sem):
    return pltpu.CompilerParams(dimension_semantics=sem, vmem_limit_bytes=VMEM_LIMIT)


def _sigmoid(x):
    return 1.0 / (1.0 + jnp.exp(-x))


def _head(h):
    return slice(h * HEAD_DIM, (h + 1) * HEAD_DIM)


def _two_streams(rows, width, n_first, n_second=None):
    mode = {"pipeline_mode": pl.Buffered(1)} if n_second == 1 else {}
    return (pl.BlockSpec((rows, width), lambda i, *_: (jnp.minimum(i, n_first - 1), 0)),
            pl.BlockSpec((rows, width), lambda i, *_: (jnp.maximum(i - n_first, 0), 0), **mode))


def _head_rows(h, rows):
    return pl.ds(h, rows, stride=N_HEADS)


def _normed(x, g_ref):
    ms = jnp.mean(x * x, axis=-1, keepdims=True)
    return (x * lax.rsqrt(ms + EPS) * g_ref[...]).astype(BF16)


def _rmsnorm_kernel(xp_ref, xs_ref, g_ref, o_ref, *, n_first):
    @pl.when(pl.program_id(0) < n_first)
    def _():
        o_ref[...] = _normed(xp_ref[...], g_ref)

    @pl.when(pl.program_id(0) >= n_first)
    def _():
        o_ref[...] = _normed(xs_ref[...], g_ref)


def _rmsnorm_rows(xp, xs, g, *, tm):
    D = xp.shape[1]
    n_first = xp.shape[0] // tm
    T = xp.shape[0] + xs.shape[0]
    return pl.pallas_call(
        functools.partial(_rmsnorm_kernel, n_first=n_first),
        grid=(T // tm,),
        in_specs=[*_two_streams(tm, D, n_first), pl.BlockSpec((1, D), lambda i: (0, 0))],
        out_specs=pl.BlockSpec((tm, D), lambda i: (i, 0)),
        out_shape=jax.ShapeDtypeStruct((T, D), BF16),
        compiler_params=_params("parallel"),
        name="rmsnorm_rows",
    )(xp, xs, g)


def _proj_kernel(h_ref, w_ref, o_ref, *, act):
    acc = jnp.dot(h_ref[...], w_ref[...], preferred_element_type=F32)
    if act == "sigmoid":
        acc = _sigmoid(acc)
    o_ref[...] = acc.astype(o_ref.dtype)


def _proj(h, w, *, n_cols, act, out_dtype, tm, tn):
    T, D = h.shape
    return pl.pallas_call(
        functools.partial(_proj_kernel, act=act),
        grid=(T // tm, n_cols // tn),
        in_specs=[pl.BlockSpec((tm, D), lambda i, j: (i, 0)),
                  pl.BlockSpec((D, tn), lambda i, j: (0, j))],
        out_specs=pl.BlockSpec((tm, tn), lambda i, j: (i, j)),
        out_shape=jax.ShapeDtypeStruct((T, n_cols), out_dtype),
        compiler_params=_params("parallel", "arbitrary"),
        name="proj_" + (act or "lin"),
    )(h, w)


def _proj_fox_kernel(h_ref, w_ref, wfl_ref,
                     q_ref, kb_ref, vb_ref, kp_ref, ks_ref, vp_ref, vs_ref, fl_ref, *, n_first):
    i, j = pl.program_id(0), pl.program_id(1)

    @pl.when(j == 0)
    def _():
        fl_ref[...] = lax.dot_general(wfl_ref[...], h_ref[...], NT_DIMS,
                                      preferred_element_type=F32)

    h = h_ref[...]
    tm = h.shape[0]
    pair = 2 * HEAD_DIM

    def column_pairs():
        for c in range(WIDTH // pair):
            cs = slice(c * pair, (c + 1) * pair)
            yield c, cs, jnp.dot(h, w_ref[:, cs], preferred_element_type=F32)

    @pl.when(j == 0)
    def _():
        for _, cs, acc in column_pairs():
            q_ref[:, cs] = (acc * QK_SCALE).astype(BF16)

    def key_or_value(bf_ref, heads_ref):
        for c, cs, acc in column_pairs():
            bf_ref[:, cs] = acc.astype(BF16)
            for hh in range(2):
                heads_ref[_head_rows(2 * c + hh, tm), :] = acc[:, _head(hh)]

    for jj, bf_ref, first_ref, second_ref in ((1, kb_ref, kp_ref, ks_ref),
                                              (2, vb_ref, vp_ref, vs_ref)):
        pl.when((j == jj) & (i < n_first))(
            functools.partial(key_or_value, bf_ref, first_ref))
        pl.when((j == jj) & (i >= n_first))(
            functools.partial(key_or_value, bf_ref, second_ref))


def _proj_fox(h, w, w_fl_t, *, col0, rows_first, tm):
    T, D = h.shape
    Tp, Ts = rows_first, T - rows_first
    n_first = Tp // tm
    kv_p, kv_s = _two_streams(tm * N_HEADS, HEAD_DIM, n_first)
    kv_shape = lambda rows: jax.ShapeDtypeStruct((rows * N_HEADS, HEAD_DIM), F32)
    tok = pl.BlockSpec((tm, WIDTH), lambda i, j: (i, 0))
    tok_shape = jax.ShapeDtypeStruct((T, WIDTH), BF16)
    return pl.pallas_call(
        functools.partial(_proj_fox_kernel, n_first=n_first),
        grid=(T // tm, 3),
        in_specs=[pl.BlockSpec((tm, D), lambda i, j: (i, 0)),
                  pl.BlockSpec((D, WIDTH), lambda i, j: (0, col0 // WIDTH + j)),
                  pl.BlockSpec((N_HEADS, D), lambda i, j: (0, 0))],
        out_specs=[tok, tok, tok, kv_p, kv_s, kv_p, kv_s,
                   pl.BlockSpec((N_HEADS, tm), lambda i, j: (0, i))],
        out_shape=[tok_shape, tok_shape, tok_shape,
                   kv_shape(Tp), kv_shape(Ts), kv_shape(Tp), kv_shape(Ts),
                   jax.ShapeDtypeStruct((N_HEADS, T), F32)],
        compiler_params=_params("arbitrary", "arbitrary"),
        name="proj_fox",
    )(h, w, w_fl_t)


def _lane_cumsum(x, seg):
    pos = lax.broadcasted_iota(jnp.int32, x.shape, x.ndim - 1) & (seg - 1)
    shift = 1
    while shift < seg:
        x = x + jnp.where(pos >= shift, pltpu.roll(x, shift, x.ndim - 1), 0.0)
        shift *= 2
    return x


def _gate_kernel(fl_ref, bias_ref, lf_ref, c_ref, *, seg):
    z = fl_ref[...] + bias_ref[...]
    lf = jnp.minimum(z, 0.0) - jnp.log(1.0 + jnp.exp(-jnp.abs(z)))
    lf_ref[...] = lf
    c = _lane_cumsum(lf, seg) * LOG2E
    c_ref[...] = jnp.zeros(c_ref.shape, F32)
    c_ref[0:c.shape[0], :] = c


def _fox_gates(fl_t, bias, *, col0, cols, seg, block, pad_rows):
    H = fl_t.shape[0]
    nb = cols // block
    base = col0 // block
    return pl.pallas_call(
        functools.partial(_gate_kernel, seg=seg),
        grid=(nb,),
        in_specs=[pl.BlockSpec((H, block), lambda i: (0, base + i)),
                  pl.BlockSpec((H, 1), lambda i: (0, 0))],
        out_specs=[pl.BlockSpec((H, block), lambda i: (0, i)),
                   pl.BlockSpec((None, pad_rows, block), lambda i: (i, 0, 0))],
        out_shape=[jax.ShapeDtypeStruct((H, cols), F32),
                   jax.ShapeDtypeStruct((nb, pad_rows, block), F32)],
        compiler_params=_params("parallel"),
        name="fox_gates",
    )(fl_t, bias)


def _suffix_kernel(lf_ref, o_ref):
    lf = lf_ref[...]
    c = _lane_cumsum(lf, lf.shape[-1])
    o_ref[...] = (c[:, -1:] - c) * LOG2E


def _cache_suffix(clf_t):
    B, H, P = clf_t.shape
    spec = pl.BlockSpec((None, H, P), lambda b: (b, 0, 0))
    return pl.pallas_call(
        _suffix_kernel,
        grid=(B,),
        in_specs=[spec],
        out_specs=spec,
        out_shape=jax.ShapeDtypeStruct((B, H, P), F32),
        compiler_params=_params("parallel"),
        name="fox_cache_suffix",
    )(clf_t)


SUB = 16


def _gla_kernel(aq_ref, af_ref, ai_ref, ag_ref, lbl_ref, gn_ref, s0_ref, tri_ref,
                o_ref, s_out_ref, st_ref, *, chunk, n_chunks):
    t = pl.program_id(1)

    @pl.when(t == 0)
    def _():
        for h in range(N_HEADS):
            st_ref[h] = s0_ref[h].T

    lbl = lbl_ref[...]
    e = jnp.exp(lbl - jnp.max(lbl, axis=0, keepdims=True))
    lb = e[0:1] / jnp.sum(e, axis=0, keepdims=True)
    gn = gn_ref[...]
    tri = tri_ref[...]
    n_sub = chunk // SUB

    def one_chunk(ci):
        r = slice(ci * chunk, (ci + 1) * chunk)
        fa = lb + (1.0 - lb) * _sigmoid(af_ref[r, :])
        lf = jnp.log(fa)
        ka = 1.0 - fa
        aq = aq_ref[r, :]
        qa = aq * _sigmoid(aq)
        v = ai_ref[r, :].astype(BF16)
        ag = ag_ref[r, :]
        gate = ag * _sigmoid(ag)

        hi = lf.astype(BF16)
        r1 = lf - hi.astype(F32)
        mid = r1.astype(BF16)
        lo = (r1 - mid.astype(F32)).astype(BF16)
        b = (jnp.dot(tri, hi, preferred_element_type=F32)
             + jnp.dot(tri, mid, preferred_element_type=F32)
             + jnp.dot(tri, lo, preferred_element_type=F32))
        b_end = b[chunk - 1:chunk, :]

        q_in = (qa * jnp.exp(b)).astype(BF16)
        k_out = (ka * jnp.exp(b_end - b)).astype(BF16)
        decay = jnp.exp(b_end)
        qp, kp = [], []
        for i in range(n_sub):
            r0, n = i * SUB, (i + 1) * SUB
            m_i = b[r0 + SUB // 2:r0 + SUB // 2 + 1, :]
            qp.append((qa[r0:n] * jnp.exp(b[r0:n] - m_i)).astype(BF16))
            kp.append((ka[:n] * jnp.exp(m_i - b[:n])).astype(BF16))

        st = [st_ref[h] for h in range(N_HEADS)]
        inter = [lax.dot_general(q_in[:, _head(h)], st[h].astype(BF16), NT_DIMS,
                                 preferred_element_type=F32) for h in range(N_HEADS)]
        att = [[lax.dot_general(qp[i][:, _head(h)], kp[i][:, _head(h)], NT_DIMS,
                                preferred_element_type=F32) for h in range(N_HEADS)]
               for i in range(n_sub)]
        for h in range(N_HEADS):
            st_ref[h] = st[h] * decay[:, _head(h)] + lax.dot_general(
                v[:, _head(h)], k_out[:, _head(h)], TN_DIMS, preferred_element_type=F32)
        for i in range(n_sub):
            r0, n = i * SUB, (i + 1) * SUB
            causal = (lax.broadcasted_iota(jnp.int32, (SUB, n), 1)
                      <= lax.broadcasted_iota(jnp.int32, (SUB, n), 0) + r0)
            att[i] = [jnp.where(causal, a, 0.0).astype(BF16) for a in att[i]]
        for h in range(N_HEADS):
            o = jnp.concatenate(
                [inter[h][i * SUB:(i + 1) * SUB]
                 + jnp.dot(att[i][h], v[:(i + 1) * SUB, _head(h)], preferred_element_type=F32)
                 for i in range(n_sub)], axis=0)
            ms = jnp.mean(o * o, axis=-1, keepdims=True)
            o_ref[r, _head(h)] = (o * lax.rsqrt(ms + EPS) * gn * gate[:, _head(h)]).astype(o_ref.dtype)

    for ci in range(n_chunks):
        one_chunk(ci)

    @pl.when(t == pl.num_programs(1) - 1)
    def _():
        for h in range(N_HEADS):
            s_out_ref[h] = st_ref[h].T


def _hgrn2(a_proj, lb_logits, gnorm, s0, *, batch, length, row0, chunk, n_chunks):
    step = chunk * n_chunks
    nt = length // step
    base = row0 // step

    def col(g):
        return pl.BlockSpec((step, WIDTH), lambda b, t: (base + b * nt + t, g))

    st_spec = pl.BlockSpec((None, N_HEADS, HEAD_DIM, HEAD_DIM), lambda b, t: (b, 0, 0, 0))
    tri = jnp.tril(jnp.ones((chunk, chunk), BF16))
    return pl.pallas_call(
        functools.partial(_gla_kernel, chunk=chunk, n_chunks=n_chunks),
        grid=(batch, nt),
        in_specs=[col(0), col(1), col(2), col(3),
                  pl.BlockSpec(lb_logits.shape, lambda b, t: (0, 0)),
                  pl.BlockSpec((1, HEAD_DIM), lambda b, t: (0, 0)),
                  st_spec,
                  pl.BlockSpec((chunk, chunk), lambda b, t: (0, 0))],
        out_specs=[pl.BlockSpec((step, WIDTH), lambda b, t: (b * nt + t, 0)), st_spec],
        out_shape=[jax.ShapeDtypeStruct((batch * length, WIDTH), BF16),
                   jax.ShapeDtypeStruct((batch, N_HEADS, HEAD_DIM, HEAD_DIM), F32)],
        scratch_shapes=[pltpu.VMEM((N_HEADS, HEAD_DIM, HEAD_DIM), F32)],
        compiler_params=_params("parallel", "arbitrary"),
        name="hgrn2",
    )(a_proj, a_proj, a_proj, a_proj, lb_logits, gnorm, s0, tri)


def _attn_prompt_kernel(q_ref, k_ref, v_ref, c_ref, o_ref,
                        vt_ref, cb_ref, s_ref, m_ref, l_ref, acc_ref, *, blk):
    h = pl.program_id(1)
    length = k_ref.shape[0]
    halves = (slice(0, blk // 2), slice(blk // 2, blk))

    for r0 in range(0, length, blk):
        rs = slice(r0, r0 + blk)
        vt_ref[:, rs] = v_ref[rs, :].astype(F32).T.astype(BF16)
        c_cols = c_ref[:, rs].T
        onehot = lax.broadcasted_iota(jnp.int32, c_cols.shape, 1) == h
        cb_ref[rs, :] = jnp.broadcast_to(
            jnp.sum(jnp.where(onehot, c_cols, 0.0), axis=1, keepdims=True), (blk, LANES))

    def scores(iq, j, buf):
        ks = slice(j * blk, (j + 1) * blk)
        kb = k_ref[ks, :]
        cb = cb_ref[ks, :]
        for hv in halves:
            x = lax.dot_general(kb, q_ref[iq * blk + hv.start:iq * blk + hv.stop, :], NT_DIMS,
                                preferred_element_type=F32)
            s_ref[buf, :, hv] = x - jnp.concatenate([cb] * (x.shape[1] // LANES), axis=1)

    def consume(iq, j, buf):
        st = iq % 2
        masked = j == iq
        vt = vt_ref[:, j * blk:(j + 1) * blk]
        p = []
        for hv in halves:
            keys = hv.stop if masked else blk
            x = s_ref[buf, :keys, hv]
            if masked:
                key = lax.broadcasted_iota(jnp.int32, x.shape, 0)
                qry = lax.broadcasted_iota(jnp.int32, x.shape, 1) + hv.start
                x = jnp.where(key <= qry, x, -jnp.inf)
            if j == 0:
                m_new = jnp.max(x, axis=0, keepdims=True)
                e = jnp.exp2(x - m_new)
                l_ref[st, :, hv] = jnp.sum(e, axis=0, keepdims=True)
            else:
                m_prev = m_ref[st, :, hv]
                m_new = jnp.maximum(m_prev, jnp.max(x, axis=0, keepdims=True))
                alpha = jnp.exp2(m_prev - m_new)
                e = jnp.exp2(x - m_new)
                l_ref[st, :, hv] = alpha * l_ref[st, :, hv] + jnp.sum(e, axis=0, keepdims=True)
                acc_ref[st, :, hv] = alpha * acc_ref[st, :, hv]
            m_ref[st, :, hv] = m_new
            p.append(e.astype(BF16))
        for e, hv in zip(p, halves):
            pv = jnp.dot(vt[:, :e.shape[0]], e, preferred_element_type=F32)
            if j == 0:
                acc_ref[st, :, hv] = pv
            else:
                acc_ref[st, :, hv] += pv

    visits = [(iq, j) for iq in range(length // blk) for j in range(iq + 1)]
    scores(*visits[0], 0)
    for n, (iq, j) in enumerate(visits):
        if n + 1 < len(visits):
            scores(*visits[n + 1], (n + 1) % 2)
        consume(iq, j, n % 2)
        if j == iq:
            st = iq % 2
            o_ref[iq * blk:(iq + 1) * blk, :] = (acc_ref[st] / l_ref[st]).T.astype(o_ref.dtype)


def _fox_prompt(q, k, v, c, *, batch, length, blk):
    spec = pl.BlockSpec((length, HEAD_DIM), lambda b, h: (b, h))
    return pl.pallas_call(
        functools.partial(_attn_prompt_kernel, blk=blk),
        grid=(batch, N_HEADS),
        in_specs=[spec, spec, spec, pl.BlockSpec((None,) + c.shape[1:], lambda b, h: (b, 0, 0))],
        out_specs=spec,
        out_shape=jax.ShapeDtypeStruct((batch * length, WIDTH), BF16),
        scratch_shapes=[pltpu.VMEM((HEAD_DIM, length), BF16),
                        pltpu.VMEM((length, LANES), F32),
                        pltpu.VMEM((2, blk, blk), F32),
                        pltpu.VMEM((2, 1, blk), F32), pltpu.VMEM((2, 1, blk), F32),
                        pltpu.VMEM((2, HEAD_DIM, blk), F32)],
        compiler_params=_params("parallel", "parallel"),
        name="fox_prompt",
    )(q, k, v, c)


def _attend_heads(score_fn, value_fn, m_ref, l_ref, acc_ref):
    s = [score_fn(h) for h in range(N_HEADS)]
    stats = []
    for h in range(N_HEADS):
        m_prev = m_ref[h]
        m_new = jnp.maximum(m_prev, jnp.max(s[h], axis=-1, keepdims=True))
        stats.append((jnp.exp2(m_prev - m_new), m_new))
    p = [jnp.exp2(s[h] - stats[h][1]) for h in range(N_HEADS)]
    for h in range(N_HEADS):
        alpha, m_new = stats[h]
        l_ref[h] = alpha * l_ref[h] + jnp.sum(p[h], axis=-1, keepdims=True)
        m_ref[h] = m_new
        acc_ref[:, _head(h)] = alpha * acc_ref[:, _head(h)] + jnp.dot(
            p[h].astype(BF16), value_fn(h), preferred_element_type=F32)


def _attn_sample_kernel(q_ref, kc_hbm, vc_hbm, bc_ref, kn_ref, vn_ref, bn_ref, o_ref,
                        kbuf_ref, vbuf_ref, sem_ref, m_ref, l_ref, acc_ref):
    b, j = pl.program_id(0), pl.program_id(1)
    nj = pl.num_programs(1)
    tk = bc_ref.shape[1]
    step = b * nj + j
    slot = step % 2

    def cache_copies(stream, blk, into):
        return [pltpu.make_async_copy(hbm.at[stream, pl.ds(blk * tk, tk), h, :],
                                      buf.at[into, h], sem_ref.at[t, into])
                for t, (hbm, buf) in enumerate(((kc_hbm, kbuf_ref), (vc_hbm, vbuf_ref)))
                for h in range(N_HEADS)]

    def start_all(copies):
        for cp in copies:
            cp.start()

    @pl.when(step == 0)
    def _():
        start_all(cache_copies(b, j, slot))

    @pl.when(step + 1 < pl.num_programs(0) * nj)
    def _():
        start_all(cache_copies((step + 1) // nj, (step + 1) % nj, 1 - slot))

    @pl.when(j == 0)
    def _():
        m_ref[...] = jnp.full(m_ref.shape, -jnp.inf, F32)
        l_ref[...] = jnp.zeros(l_ref.shape, F32)
        acc_ref[...] = jnp.zeros(acc_ref.shape, F32)

    for cp in cache_copies(b, j, slot):
        cp.wait()
    _attend_heads(
        lambda h: lax.dot_general(q_ref[:, _head(h)], kbuf_ref[slot, h].astype(BF16),
                                  NT_DIMS, preferred_element_type=F32) + bc_ref[h:h + 1, :],
        lambda h: vbuf_ref[slot, h].astype(BF16), m_ref, l_ref, acc_ref)

    @pl.when(j == pl.num_programs(1) - 1)
    def _():
        tq = q_ref.shape[0]
        causal = (lax.broadcasted_iota(jnp.int32, (tq, tq), 1)
                  <= lax.broadcasted_iota(jnp.int32, (tq, tq), 0))
        _attend_heads(
            lambda h: jnp.where(causal,
                                lax.dot_general(q_ref[:, _head(h)], kn_ref[:, _head(h)], NT_DIMS,
                                                preferred_element_type=F32) - bn_ref[h:h + 1, :],
                                -jnp.inf),
            lambda h: vn_ref[:, _head(h)], m_ref, l_ref, acc_ref)
        for h in range(N_HEADS):
            o_ref[:, _head(h)] = (acc_ref[:, _head(h)] / l_ref[h]).astype(o_ref.dtype)


def _fox_sample(q, k_new, v_new, c_new, cache_k, cache_v, cache_bias, *, row0, tk):
    B, H, P = cache_bias.shape
    Ls = c_new.shape[2]
    base = row0 // Ls
    new = pl.BlockSpec((Ls, WIDTH), lambda b, j: (base + b, 0))
    cache = pl.BlockSpec(memory_space=pl.ANY)
    cache_buf = pltpu.VMEM((2, H, tk, HEAD_DIM), F32)
    return pl.pallas_call(
        _attn_sample_kernel,
        grid=(B, P // tk),
        in_specs=[new, cache, cache,
                  pl.BlockSpec((None, N_HEADS, tk), lambda b, j: (b, 0, j)),
                  new, new,
                  pl.BlockSpec((None, N_HEADS, Ls), lambda b, j: (b, 0, 0))],
        out_specs=pl.BlockSpec((Ls, WIDTH), lambda b, j: (b, 0)),
        out_shape=jax.ShapeDtypeStruct((B * Ls, WIDTH), BF16),
        scratch_shapes=[cache_buf, cache_buf, pltpu.SemaphoreType.DMA((2, 2)),
                        pltpu.VMEM((N_HEADS, Ls, 1), F32), pltpu.VMEM((N_HEADS, Ls, 1), F32),
                        pltpu.VMEM((Ls, WIDTH), F32)],
        compiler_params=_params("arbitrary", "arbitrary"),
        name="fox_sample",
    )(q, cache_k, cache_v, cache_bias, k_new, v_new, c_new)


MERGE_COLS = 512


def _merge_kernel(xp_ref, xs_ref, oap_ref, oas_ref, obp_ref, obs_ref, g_ref,
                  wpa_ref, wpb_ref, wo_ref, g2_ref, o_ref, h2_ref, mg_ref, *, n_first):
    first = pl.program_id(0) < n_first
    D = xp_ref.shape[1]
    oa = jnp.where(first, oap_ref[...], oas_ref[...])
    ob = jnp.where(first, obp_ref[...], obs_ref[...])
    cols = min(MERGE_COLS, D)
    for n0 in range(0, D, cols):
        ns = slice(n0, n0 + cols)
        pa = jnp.dot(oa, wpa_ref[:, ns], preferred_element_type=F32)
        pb = jnp.dot(ob, wpb_ref[:, ns], preferred_element_type=F32)
        ga = g_ref[:, ns].astype(F32)
        gb = g_ref[:, D + n0:D + n0 + cols].astype(F32)
        mg_ref[:, ns] = (ga * pa + gb * pb).astype(BF16)
    x = jnp.where(first, xp_ref[...], xs_ref[...])
    x = x + jnp.dot(mg_ref[...], wo_ref[...], preferred_element_type=F32)
    o_ref[...] = x
    h2_ref[...] = _normed(x, g2_ref)


def _merge_out(xp, xs, oa_p, oa_s, ob_p, ob_s, g, w_pa, w_pb, w_o, g2, *, tm):
    D = xp.shape[1]
    T = xp.shape[0] + xs.shape[0]
    n_first = xp.shape[0] // tm
    fixed = lambda i: (0, 0)
    return pl.pallas_call(
        functools.partial(_merge_kernel, n_first=n_first),
        grid=(T // tm,),
        in_specs=[*_two_streams(tm, D, n_first), *_two_streams(tm, WIDTH, n_first),
                  *_two_streams(tm, WIDTH, n_first),
                  pl.BlockSpec((tm, 2 * D), lambda i: (i, 0)),
                  *[pl.BlockSpec(w.shape, fixed, pipeline_mode=pl.Buffered(1))
                    for w in (w_pa, w_pb, w_o)],
                  pl.BlockSpec((1, D), fixed)],
        out_specs=[pl.BlockSpec((tm, D), lambda i: (i, 0))] * 2,
        out_shape=[jax.ShapeDtypeStruct((T, D), F32), jax.ShapeDtypeStruct((T, D), BF16)],
        scratch_shapes=[pltpu.VMEM((tm, D), BF16)],
        compiler_params=_params("parallel"),
        name="merge_out",
    )(xp, xs, oa_p, oa_s, ob_p, ob_s, g, w_pa, w_pb, w_o, g2)


def _ffn_kernel(x_ref, h_ref, w1_ref, w2_ref, gf_ref, yp_ref, ys_ref, acc_ref, *, n_first):
    i, f = pl.program_id(0), pl.program_id(1)

    @pl.when(f == 0)
    def _():
        acc_ref[...] = x_ref[...]

    u = jnp.maximum(jnp.dot(h_ref[...], w1_ref[...], preferred_element_type=F32), 0.0)
    acc_ref[...] += jnp.dot((u * u).astype(BF16), w2_ref[...], preferred_element_type=F32)

    @pl.when(f == pl.num_programs(1) - 1)
    def _():
        x = acc_ref[...]
        ms = jnp.mean(x * x, axis=-1, keepdims=True)
        y = x * lax.rsqrt(ms + EPS) * gf_ref[...]

        @pl.when(i < n_first)
        def _():
            yp_ref[...] = y

        @pl.when(i >= n_first)
        def _():
            ys_ref[...] = y


def _ffn(x, h, w1, w2, gf, *, rows_first, tm, tf):
    T, D = x.shape
    F = w1.shape[1]
    n_first = rows_first // tm
    rows = pl.BlockSpec((tm, D), lambda i, f: (i, 0))
    return pl.pallas_call(
        functools.partial(_ffn_kernel, n_first=n_first),
        grid=(T // tm, F // tf),
        in_specs=[rows, rows,
                  pl.BlockSpec((D, tf), lambda i, f: (0, f)),
                  pl.BlockSpec((tf, D), lambda i, f: (f, 0)),
                  pl.BlockSpec((1, D), lambda i, f: (0, 0))],
        out_specs=list(_two_streams(tm, D, n_first)),
        out_shape=[jax.ShapeDtypeStruct((rows_first, D), F32),
                   jax.ShapeDtypeStruct((T - rows_first, D), F32)],
        scratch_shapes=[pltpu.VMEM((tm, D), F32)],
        compiler_params=_params("arbitrary", "arbitrary"),
        name="ffn",
    )(x, h, w1, w2, gf)


def _tile(n, pref):
    t = min(pref, n)
    while n % t:
        t //= 2
    return t


def kernel(x_prompt, x_sample, cache_fox_k, cache_fox_v, cache_fox_logf, state_hgrn, norm1, w_in,
           b_fox_f, lb_logits, gnorm_a, w_pa, w_pb, w_o, norm2, w1, w2, norm_f):
    B, L, D = x_prompt.shape
    Bs, Ls, _ = x_sample.shape
    depth, _, P, H, DH = cache_fox_k.shape
    assert depth == 1 and H == N_HEADS and DH == HEAD_DIM
    Tp, Ts = B * L, Bs * Ls
    T = Tp + Ts
    W = WIDTH

    xp, xs = x_prompt.reshape(Tp, D), x_sample.reshape(Ts, D)
    w_inb = w_in[0].astype(BF16)
    w_fl_t = w_inb[:, 7 * W:7 * W + H].T
    w_g = w_inb[:, 7 * W + H:]
    g1 = norm1[0].reshape(1, D)

    tm = _tile(math.gcd(Tp, Ts), 512)
    tm1 = _tile(math.gcd(Tp, Ts), 1024)
    h1 = _rmsnorm_rows(xp, xs, g1, tm=_tile(tm, 256))
    a_proj = _proj(h1, w_inb, n_cols=4 * W, act=None, out_dtype=F32, tm=tm1, tn=W)
    gates = _proj(h1, w_g, n_cols=2 * D, act="sigmoid", out_dtype=BF16, tm=tm1, tn=_tile(2 * D, W))
    q_b, k_b, v_b, k_p, k_s, v_p, v_s, fl_t = _proj_fox(h1, w_inb, w_fl_t, col0=4 * W,
                                                        rows_first=Tp, tm=_tile(tm, 512))

    bias = b_fox_f[0].reshape(H, 1)
    lf_p, c_p = _fox_gates(fl_t, bias, col0=0, cols=Tp, seg=L, block=L, pad_rows=LANES)
    lf_s, c_s = _fox_gates(fl_t, bias, col0=Tp, cols=Ts, seg=Ls, block=_tile(Ts, 1024), pad_rows=H)
    cache_bias = _cache_suffix(jnp.transpose(cache_fox_logf[0], (0, 2, 1)))

    zeros_state = jnp.zeros((B, H, DH, DH), F32)
    chunk_p = _tile(L, 64)
    oa_p, s_p = _hgrn2(a_proj, lb_logits, gnorm_a, zeros_state, batch=B, length=L, row0=0,
                       chunk=chunk_p, n_chunks=_tile(L // chunk_p, 4))
    oa_s, s_s = _hgrn2(a_proj, lb_logits, gnorm_a, state_hgrn[0], batch=Bs, length=Ls, row0=Tp,
                       chunk=_tile(Ls, 64), n_chunks=1)

    ob_p = _fox_prompt(q_b, k_b, v_b, c_p, batch=B, length=L, blk=_tile(L, 512))
    c_s3 = c_s.transpose(1, 0, 2).reshape(H, Bs, Ls).transpose(1, 0, 2)
    ob_s = _fox_sample(q_b, k_b, v_b, c_s3, cache_fox_k[0], cache_fox_v[0], cache_bias,
                       row0=Tp, tk=_tile(P, 1024))

    tm5 = _tile(tm, 512)
    x1, h2 = _merge_out(xp, xs, oa_p, oa_s, ob_p, ob_s, gates, w_pa[0].astype(BF16),
                        w_pb[0].astype(BF16), w_o[0].astype(BF16), norm2[0].reshape(1, D),
                        tm=_tile(tm, 256))
    y_p, y_s = _ffn(x1, h2, w1[0].astype(BF16), w2[0].astype(BF16), norm_f.reshape(1, D),
                    rows_first=Tp, tm=tm5, tf=_tile(w1.shape[-1], 1024))

    return (y_p.reshape(B, L, D), y_s.reshape(Bs, Ls, D),
            k_p.reshape(1, B, L, H, DH), v_p.reshape(1, B, L, H, DH),
            lf_p.T.reshape(1, B, L, H), s_p[None],
            k_s.reshape(1, Bs, Ls, H, DH), v_s.reshape(1, Bs, Ls, H, DH),
            lf_s.T.reshape(1, Bs, Ls, H), s_s[None])
```

```python
import functools
import math

import jax
import jax.numpy as jnp
from jax import lax
from jax.experimental import pallas as pl
from jax.experimental.pallas import tpu as pltpu

F32 = jnp.float32
BF16 = jnp.bfloat16

EPS = 1e-6
N_HEADS = 8
HEAD_DIM = 128
WIDTH = N_HEADS * HEAD_DIM
LANES = 128
VMEM_LIMIT = 56 * 1024 * 1024
LOG2E = math.log2(math.e)
QK_SCALE = HEAD_DIM ** -0.5 * LOG2E

NT_DIMS = (((1,), (1,)), ((), ()))
TN_DIMS = (((0,), (0,)), ((), ()))


def _params(*sem):
    return pltpu.CompilerParams(dimension_semantics=sem, vmem_limit_bytes=VMEM_LIMIT)


def _sigmoid(x):
    return 1.0 / (1.0 + jnp.exp(-x))


def _head(h):
    return slice(h * HEAD_DIM, (h + 1) * HEAD_DIM)


def _two_streams(rows, width, n_first, n_second=None):
    mode = {"pipeline_mode": pl.Buffered(1)} if n_second == 1 else {}
    return (pl.BlockSpec((rows, width), lambda i, *_: (jnp.minimum(i, n_first - 1), 0)),
            pl.BlockSpec((rows, width), lambda i, *_: (jnp.maximum(i - n_first, 0), 0), **mode))


def _head_rows(h, rows):
    return pl.ds(h, rows, stride=N_HEADS)


def _normed(x, g_ref):
    ms = jnp.mean(x * x, axis=-1, keepdims=True)
    return (x * lax.rsqrt(ms + EPS) * g_ref[...]).astype(BF16)


def _rmsnorm_kernel(xp_ref, xs_ref, g_ref, o_ref, *, n_first):
    @pl.when(pl.program_id(0) < n_first)
    def _():
        o_ref[...] = _normed(xp_ref[...], g_ref)

    @pl.when(pl.program_id(0) >= n_first)
    def _():
        o_ref[...] = _normed(xs_ref[...], g_ref)


def _rmsnorm_rows(xp, xs, g, *, tm):
    D = xp.shape[1]
    n_first = xp.shape[0] // tm
    T = xp.shape[0] + xs.shape[0]
    return pl.pallas_call(
        functools.partial(_rmsnorm_kernel, n_first=n_first),
        grid=(T // tm,),
        in_specs=[*_two_streams(tm, D, n_first), pl.BlockSpec((1, D), lambda i: (0, 0))],
        out_specs=pl.BlockSpec((tm, D), lambda i: (i, 0)),
        out_shape=jax.ShapeDtypeStruct((T, D), BF16),
        compiler_params=_params("parallel"),
        name="rmsnorm_rows",
    )(xp, xs, g)


def _first_pass_weights(n_blocks, block0, shape, buffers=2):
    return pl.BlockSpec(
        shape, lambda i, j: (0, block0 + jnp.where(i == 0, j, n_blocks - 1)),
        pipeline_mode=pl.Buffered(buffers))


def _proj_kernel(h_ref, w_ref, o_ref, wb_ref, *, act):
    j = pl.program_id(1)

    @pl.when(pl.program_id(0) == 0)
    def _():
        wb_ref[j] = w_ref[...].astype(BF16)

    acc = jnp.dot(h_ref[...], wb_ref[j], preferred_element_type=F32)
    if act == "sigmoid":
        acc = _sigmoid(acc)
    o_ref[...] = acc.astype(o_ref.dtype)


def _proj(h, w, *, n_cols, act, out_dtype, tm, tn):
    T, D = h.shape
    nj = n_cols // tn
    return pl.pallas_call(
        functools.partial(_proj_kernel, act=act),
        grid=(T // tm, nj),
        in_specs=[pl.BlockSpec((tm, D), lambda i, j: (i, 0)),
                  _first_pass_weights(nj, 0, (D, tn))],
        out_specs=pl.BlockSpec((tm, tn), lambda i, j: (i, j)),
        out_shape=jax.ShapeDtypeStruct((T, n_cols), out_dtype),
        scratch_shapes=[pltpu.VMEM((nj, D, tn), BF16)],
        compiler_params=_params("arbitrary", "arbitrary"),
        name="proj_" + (act or "lin"),
    )(h, w)


def _proj_fox_kernel(h_ref, w_ref, wfl_ref,
                     q_ref, kb_ref, vb_ref, kp_ref, ks_ref, vp_ref, vs_ref, fl_ref, wb_ref,
                     *, n_first):
    i, j = pl.program_id(0), pl.program_id(1)

    @pl.when(i == 0)
    def _():
        wb_ref[j] = w_ref[...].astype(BF16)

    @pl.when(j == 0)
    def _():
        fl_ref[...] = lax.dot_general(wfl_ref[...], h_ref[...], NT_DIMS,
                                      preferred_element_type=F32)

    h = h_ref[...]
    tm = h.shape[0]
    pair = 2 * HEAD_DIM

    def column_pairs():
        for c in range(WIDTH // pair):
            cs = slice(c * pair, (c + 1) * pair)
            yield c, cs, jnp.dot(h, wb_ref[j, :, cs], preferred_element_type=F32)

    @pl.when(j == 0)
    def _():
        for _, cs, acc in column_pairs():
            q_ref[:, cs] = (acc * QK_SCALE).astype(BF16)

    def key_or_value(bf_ref, heads_ref):
        for c, cs, acc in column_pairs():
            bf_ref[:, cs] = acc.astype(BF16)
            for hh in range(2):
                heads_ref[_head_rows(2 * c + hh, tm), :] = acc[:, _head(hh)]

    for jj, bf_ref, first_ref, second_ref in ((1, kb_ref, kp_ref, ks_ref),
                                              (2, vb_ref, vp_ref, vs_ref)):
        pl.when((j == jj) & (i < n_first))(
            functools.partial(key_or_value, bf_ref, first_ref))
        pl.when((j == jj) & (i >= n_first))(
            functools.partial(key_or_value, bf_ref, second_ref))


def _proj_fox(h, w, w_fl_t, *, col0, rows_first, tm):
    T, D = h.shape
    Tp, Ts = rows_first, T - rows_first
    n_first = Tp // tm
    kv_p, kv_s = _two_streams(tm * N_HEADS, HEAD_DIM, n_first)
    kv_shape = lambda rows: jax.ShapeDtypeStruct((rows * N_HEADS, HEAD_DIM), F32)
    tok = pl.BlockSpec((tm, WIDTH), lambda i, j: (i, 0))
    tok_shape = jax.ShapeDtypeStruct((T, WIDTH), BF16)
    return pl.pallas_call(
        functools.partial(_proj_fox_kernel, n_first=n_first),
        grid=(T // tm, 3),
        in_specs=[pl.BlockSpec((tm, D), lambda i, j: (i, 0)),
                  _first_pass_weights(3, col0 // WIDTH, (D, WIDTH), buffers=1),
                  pl.BlockSpec((N_HEADS, D), lambda i, j: (0, 0))],
        out_specs=[tok, tok, tok, kv_p, kv_s, kv_p, kv_s,
                   pl.BlockSpec((N_HEADS, tm), lambda i, j: (0, i))],
        out_shape=[tok_shape, tok_shape, tok_shape,
                   kv_shape(Tp), kv_shape(Ts), kv_shape(Tp), kv_shape(Ts),
                   jax.ShapeDtypeStruct((N_HEADS, T), F32)],
        scratch_shapes=[pltpu.VMEM((3, D, WIDTH), BF16)],
        compiler_params=_params("arbitrary", "arbitrary"),
        name="proj_fox",
    )(h, w, w_fl_t)


def _lane_cumsum(x, seg):
    pos = lax.broadcasted_iota(jnp.int32, x.shape, x.ndim - 1) & (seg - 1)
    shift = 1
    while shift < seg:
        x = x + jnp.where(pos >= shift, pltpu.roll(x, shift, x.ndim - 1), 0.0)
        shift *= 2
    return x


def _gate_kernel(fl_ref, bias_ref, lf_ref, c_ref, *, seg):
    z = fl_ref[...] + bias_ref[...]
    lf = jnp.minimum(z, 0.0) - jnp.log(1.0 + jnp.exp(-jnp.abs(z)))
    lf_ref[...] = lf
    c = _lane_cumsum(lf, seg) * LOG2E
    c_ref[...] = jnp.zeros(c_ref.shape, F32)
    c_ref[0:c.shape[0], :] = c


def _fox_gates(fl_t, bias, *, col0, cols, seg, block, pad_rows):
    H = fl_t.shape[0]
    nb = cols // block
    base = col0 // block
    return pl.pallas_call(
        functools.partial(_gate_kernel, seg=seg),
        grid=(nb,),
        in_specs=[pl.BlockSpec((H, block), lambda i: (0, base + i)),
                  pl.BlockSpec((H, 1), lambda i: (0, 0))],
        out_specs=[pl.BlockSpec((H, block), lambda i: (0, i)),
                   pl.BlockSpec((None, pad_rows, block), lambda i: (i, 0, 0))],
        out_shape=[jax.ShapeDtypeStruct((H, cols), F32),
                   jax.ShapeDtypeStruct((nb, pad_rows, block), F32)],
        compiler_params=_params("parallel"),
        name="fox_gates",
    )(fl_t, bias)


def _suffix_kernel(lf_ref, o_ref):
    lf = lf_ref[...]
    c = _lane_cumsum(lf, lf.shape[-1])
    o_ref[...] = (c[:, -1:] - c) * LOG2E


def _cache_suffix(clf_t):
    B, H, P = clf_t.shape
    spec = pl.BlockSpec((None, H, P), lambda b: (b, 0, 0))
    return pl.pallas_call(
        _suffix_kernel,
        grid=(B,),
        in_specs=[spec],
        out_specs=spec,
        out_shape=jax.ShapeDtypeStruct((B, H, P), F32),
        compiler_params=_params("parallel"),
        name="fox_cache_suffix",
    )(clf_t)


SUB = 16


def _gla_kernel(aq_ref, af_ref, ai_ref, ag_ref, lbl_ref, gn_ref, s0_ref, tri_ref,
                o_ref, s_out_ref, st_ref, *, chunk, n_chunks):
    t = pl.program_id(1)

    @pl.when(t == 0)
    def _():
        for h in range(N_HEADS):
            st_ref[h] = s0_ref[h].T

    lbl = lbl_ref[...]
    e = jnp.exp(lbl - jnp.max(lbl, axis=0, keepdims=True))
    lb = e[0:1] / jnp.sum(e, axis=0, keepdims=True)
    gn = gn_ref[...]
    tri = tri_ref[...]
    n_sub = chunk // SUB

    def one_chunk(ci):
        r = slice(ci * chunk, (ci + 1) * chunk)
        fa = lb + (1.0 - lb) * _sigmoid(af_ref[r, :])
        lf = jnp.log(fa)
        ka = 1.0 - fa
        aq = aq_ref[r, :]
        qa = aq * _sigmoid(aq)
        v = ai_ref[r, :].astype(BF16)
        ag = ag_ref[r, :]
        gate = ag * _sigmoid(ag)

        hi = lf.astype(BF16)
        r1 = lf - hi.astype(F32)
        mid = r1.astype(BF16)
        lo = (r1 - mid.astype(F32)).astype(BF16)
        b = (jnp.dot(tri, hi, preferred_element_type=F32)
             + jnp.dot(tri, mid, preferred_element_type=F32)
             + jnp.dot(tri, lo, preferred_element_type=F32))
        b_end = b[chunk - 1:chunk, :]

        q_in = (qa * jnp.exp(b)).astype(BF16)
        k_out = (ka * jnp.exp(b_end - b)).astype(BF16)
        decay = jnp.exp(b_end)
        qp, kp = [], []
        for i in range(n_sub):
            r0, n = i * SUB, (i + 1) * SUB
            m_i = b[r0 + SUB // 2:r0 + SUB // 2 + 1, :]
            qp.append((qa[r0:n] * jnp.exp(b[r0:n] - m_i)).astype(BF16))
            kp.append((ka[:n] * jnp.exp(m_i - b[:n])).astype(BF16))

        st = [st_ref[h] for h in range(N_HEADS)]
        inter = [lax.dot_general(q_in[:, _head(h)], st[h].astype(BF16), NT_DIMS,
                                 preferred_element_type=F32) for h in range(N_HEADS)]
        att = [[lax.dot_general(qp[i][:, _head(h)], kp[i][:, _head(h)], NT_DIMS,
                                preferred_element_type=F32) for h in range(N_HEADS)]
               for i in range(n_sub)]
        for h in range(N_HEADS):
            st_ref[h] = st[h] * decay[:, _head(h)] + lax.dot_general(
                v[:, _head(h)], k_out[:, _head(h)], TN_DIMS, preferred_element_type=F32)
        for i in range(n_sub):
            r0, n = i * SUB, (i + 1) * SUB
            causal = (lax.broadcasted_iota(jnp.int32, (SUB, n), 1)
                      <= lax.broadcasted_iota(jnp.int32, (SUB, n), 0) + r0)
            att[i] = [jnp.where(causal, a, 0.0).astype(BF16) for a in att[i]]
        for h in range(N_HEADS):
            o = jnp.concatenate(
                [inter[h][i * SUB:(i + 1) * SUB]
                 + jnp.dot(att[i][h], v[:(i + 1) * SUB, _head(h)], preferred_element_type=F32)
                 for i in range(n_sub)], axis=0)
            ms = jnp.mean(o * o, axis=-1, keepdims=True)
            o_ref[r, _head(h)] = (o * lax.rsqrt(ms + EPS) * gn * gate[:, _head(h)]).astype(o_ref.dtype)

    for ci in range(n_chunks):
        one_chunk(ci)

    @pl.when(t == pl.num_programs(1) - 1)
    def _():
        for h in range(N_HEADS):
            s_out_ref[h] = st_ref[h].T


def _hgrn2(a_proj, lb_logits, gnorm, s0, *, batch, length, row0, chunk, n_chunks):
    step = chunk * n_chunks
    nt = length // step
    base = row0 // step

    def col(g):
        return pl.BlockSpec((step, WIDTH), lambda b, t: (base + b * nt + t, g))

    st_spec = pl.BlockSpec((None, N_HEADS, HEAD_DIM, HEAD_DIM), lambda b, t: (b, 0, 0, 0))
    tri = jnp.tril(jnp.ones((chunk, chunk), BF16))
    return pl.pallas_call(
        functools.partial(_gla_kernel, chunk=chunk, n_chunks=n_chunks),
        grid=(batch, nt),
        in_specs=[col(0), col(1), col(2), col(3),
                  pl.BlockSpec(lb_logits.shape, lambda b, t: (0, 0)),
                  pl.BlockSpec((1, HEAD_DIM), lambda b, t: (0, 0)),
                  st_spec,
                  pl.BlockSpec((chunk, chunk), lambda b, t: (0, 0))],
        out_specs=[pl.BlockSpec((step, WIDTH), lambda b, t: (b * nt + t, 0)), st_spec],
        out_shape=[jax.ShapeDtypeStruct((batch * length, WIDTH), BF16),
                   jax.ShapeDtypeStruct((batch, N_HEADS, HEAD_DIM, HEAD_DIM), F32)],
        scratch_shapes=[pltpu.VMEM((N_HEADS, HEAD_DIM, HEAD_DIM), F32)],
        compiler_params=_params("parallel", "arbitrary"),
        name="hgrn2",
    )(a_proj, a_proj, a_proj, a_proj, lb_logits, gnorm, s0, tri)


def _attn_prompt_kernel(q_ref, k_ref, v_ref, c_ref, o_ref,
                        vt_ref, cb_ref, s_ref, m_ref, l_ref, acc_ref, *, blk):
    h = pl.program_id(1)
    length = k_ref.shape[0]
    halves = (slice(0, blk // 2), slice(blk // 2, blk))

    for r0 in range(0, length, blk):
        rs = slice(r0, r0 + blk)
        vt_ref[:, rs] = v_ref[rs, :].astype(F32).T.astype(BF16)
        c_cols = c_ref[:, rs].T
        onehot = lax.broadcasted_iota(jnp.int32, c_cols.shape, 1) == h
        cb_ref[rs, :] = jnp.broadcast_to(
            jnp.sum(jnp.where(onehot, c_cols, 0.0), axis=1, keepdims=True), (blk, LANES))

    def scores(iq, j, buf):
        ks = slice(j * blk, (j + 1) * blk)
        kb = k_ref[ks, :]
        cb = cb_ref[ks, :]
        for hv in halves:
            x = lax.dot_general(kb, q_ref[iq * blk + hv.start:iq * blk + hv.stop, :], NT_DIMS,
                                preferred_element_type=F32)
            s_ref[buf, :, hv] = x - jnp.concatenate([cb] * (x.shape[1] // LANES), axis=1)

    def consume(iq, j, buf):
        st = iq % 2
        masked = j == iq
        vt = vt_ref[:, j * blk:(j + 1) * blk]
        p = []
        for hv in halves:
            keys = hv.stop if masked else blk
            x = s_ref[buf, :keys, hv]
            if masked:
                key = lax.broadcasted_iota(jnp.int32, x.shape, 0)
                qry = lax.broadcasted_iota(jnp.int32, x.shape, 1) + hv.start
                x = jnp.where(key <= qry, x, -jnp.inf)
            if j == 0:
                m_new = jnp.max(x, axis=0, keepdims=True)
                e = jnp.exp2(x - m_new)
                l_ref[st, :, hv] = jnp.sum(e, axis=0, keepdims=True)
            else:
                m_prev = m_ref[st, :, hv]
                m_new = jnp.maximum(m_prev, jnp.max(x, axis=0, keepdims=True))
                alpha = jnp.exp2(m_prev - m_new)
                e = jnp.exp2(x - m_new)
                l_ref[st, :, hv] = alpha * l_ref[st, :, hv] + jnp.sum(e, axis=0, keepdims=True)
                acc_ref[st, :, hv] = alpha * acc_ref[st, :, hv]
            m_ref[st, :, hv] = m_new
            p.append(e.astype(BF16))
        for e, hv in zip(p, halves):
            pv = jnp.dot(vt[:, :e.shape[0]], e, preferred_element_type=F32)
            if j == 0:
                acc_ref[st, :, hv] = pv
            else:
                acc_ref[st, :, hv] += pv

    visits = [(iq, j) for iq in range(length // blk) for j in range(iq + 1)]
    scores(*visits[0], 0)
    for n, (iq, j) in enumerate(visits):
        if n + 1 < len(visits):
            scores(*visits[n + 1], (n + 1) % 2)
        consume(iq, j, n % 2)
        if j == iq:
            st = iq % 2
            o_ref[iq * blk:(iq + 1) * blk, :] = (acc_ref[st] / l_ref[st]).T.astype(o_ref.dtype)


def _fox_prompt(q, k, v, c, *, batch, length, blk):
    spec = pl.BlockSpec((length, HEAD_DIM), lambda b, h: (b, h))
    return pl.pallas_call(
        functools.partial(_attn_prompt_kernel, blk=blk),
        grid=(batch, N_HEADS),
        in_specs=[spec, spec, spec, pl.BlockSpec((None,) + c.shape[1:], lambda b, h: (b, 0, 0))],
        out_specs=spec,
        out_shape=jax.ShapeDtypeStruct((batch * length, WIDTH), BF16),
        scratch_shapes=[pltpu.VMEM((HEAD_DIM, length), BF16),
                        pltpu.VMEM((length, LANES), F32),
                        pltpu.VMEM((2, blk, blk), F32),
                        pltpu.VMEM((2, 1, blk), F32), pltpu.VMEM((2, 1, blk), F32),
                        pltpu.VMEM((2, HEAD_DIM, blk), F32)],
        compiler_params=_params("parallel", "parallel"),
        name="fox_prompt",
    )(q, k, v, c)


def _attend_heads(score_fn, value_fn, m_ref, l_ref, acc_ref):
    s = [score_fn(h) for h in range(N_HEADS)]
    stats = []
    for h in range(N_HEADS):
        m_prev = m_ref[h]
        m_new = jnp.maximum(m_prev, jnp.max(s[h], axis=-1, keepdims=True))
        stats.append((jnp.exp2(m_prev - m_new), m_new))
    p = [jnp.exp2(s[h] - stats[h][1]) for h in range(N_HEADS)]
    for h in range(N_HEADS):
        alpha, m_new = stats[h]
        l_ref[h] = alpha * l_ref[h] + jnp.sum(p[h], axis=-1, keepdims=True)
        m_ref[h] = m_new
        acc_ref[:, _head(h)] = alpha * acc_ref[:, _head(h)] + jnp.dot(
            p[h].astype(BF16), value_fn(h), preferred_element_type=F32)


def _attn_sample_kernel(q_ref, kc_hbm, vc_hbm, bc_ref, kn_ref, vn_ref, bn_ref, o_ref,
                        kbuf_ref, vbuf_ref, sem_ref, m_ref, l_ref, acc_ref):
    b, j = pl.program_id(0), pl.program_id(1)
    nj = pl.num_programs(1)
    tk = bc_ref.shape[1]
    step = b * nj + j
    slot = step % 2

    def cache_copies(stream, blk, into):
        return [pltpu.make_async_copy(hbm.at[stream, pl.ds(blk * tk, tk), h, :],
                                      buf.at[into, h], sem_ref.at[t, into])
                for t, (hbm, buf) in enumerate(((kc_hbm, kbuf_ref), (vc_hbm, vbuf_ref)))
                for h in range(N_HEADS)]

    def start_all(copies):
        for cp in copies:
            cp.start()

    @pl.when(step == 0)
    def _():
        start_all(cache_copies(b, j, slot))

    @pl.when(step + 1 < pl.num_programs(0) * nj)
    def _():
        start_all(cache_copies((step + 1) // nj, (step + 1) % nj, 1 - slot))

    @pl.when(j == 0)
    def _():
        m_ref[...] = jnp.full(m_ref.shape, -jnp.inf, F32)
        l_ref[...] = jnp.zeros(l_ref.shape, F32)
        acc_ref[...] = jnp.zeros(acc_ref.shape, F32)

    for cp in cache_copies(b, j, slot):
        cp.wait()
    _attend_heads(
        lambda h: lax.dot_general(q_ref[:, _head(h)], kbuf_ref[slot, h].astype(BF16),
                                  NT_DIMS, preferred_element_type=F32) + bc_ref[h:h + 1, :],
        lambda h: vbuf_ref[slot, h].astype(BF16), m_ref, l_ref, acc_ref)

    @pl.when(j == pl.num_programs(1) - 1)
    def _():
        tq = q_ref.shape[0]
        causal = (lax.broadcasted_iota(jnp.int32, (tq, tq), 1)
                  <= lax.broadcasted_iota(jnp.int32, (tq, tq), 0))
        _attend_heads(
            lambda h: jnp.where(causal,
                                lax.dot_general(q_ref[:, _head(h)], kn_ref[:, _head(h)], NT_DIMS,
                                                preferred_element_type=F32) - bn_ref[h:h + 1, :],
                                -jnp.inf),
            lambda h: vn_ref[:, _head(h)], m_ref, l_ref, acc_ref)
        for h in range(N_HEADS):
            o_ref[:, _head(h)] = (acc_ref[:, _head(h)] / l_ref[h]).astype(o_ref.dtype)


def _fox_sample(q, k_new, v_new, c_new, cache_k, cache_v, cache_bias, *, row0, tk):
    B, H, P = cache_bias.shape
    Ls = c_new.shape[2]
    base = row0 // Ls
    new = pl.BlockSpec((Ls, WIDTH), lambda b, j: (base + b, 0))
    cache = pl.BlockSpec(memory_space=pl.ANY)
    cache_buf = pltpu.VMEM((2, H, tk, HEAD_DIM), F32)
    return pl.pallas_call(
        _attn_sample_kernel,
        grid=(B, P // tk),
        in_specs=[new, cache, cache,
                  pl.BlockSpec((None, N_HEADS, tk), lambda b, j: (b, 0, j)),
                  new, new,
                  pl.BlockSpec((None, N_HEADS, Ls), lambda b, j: (b, 0, 0))],
        out_specs=pl.BlockSpec((Ls, WIDTH), lambda b, j: (b, 0)),
        out_shape=jax.ShapeDtypeStruct((B * Ls, WIDTH), BF16),
        scratch_shapes=[cache_buf, cache_buf, pltpu.SemaphoreType.DMA((2, 2)),
                        pltpu.VMEM((N_HEADS, Ls, 1), F32), pltpu.VMEM((N_HEADS, Ls, 1), F32),
                        pltpu.VMEM((Ls, WIDTH), F32)],
        compiler_params=_params("arbitrary", "arbitrary"),
        name="fox_sample",
    )(q, cache_k, cache_v, cache_bias, k_new, v_new, c_new)


MERGE_COLS = 512


def _merge_kernel(xp_ref, xs_ref, oap_ref, oas_ref, obp_ref, obs_ref, g_ref,
                  wpa_ref, wpb_ref, wo_ref, g2_ref, o_ref, h2_ref, mg_ref, *, n_first):
    first = pl.program_id(0) < n_first
    D = xp_ref.shape[1]
    oa = jnp.where(first, oap_ref[...], oas_ref[...])
    ob = jnp.where(first, obp_ref[...], obs_ref[...])
    cols = min(MERGE_COLS, D)
    for n0 in range(0, D, cols):
        ns = slice(n0, n0 + cols)
        pa = jnp.dot(oa, wpa_ref[:, ns], preferred_element_type=F32)
        pb = jnp.dot(ob, wpb_ref[:, ns], preferred_element_type=F32)
        ga = g_ref[:, ns].astype(F32)
        gb = g_ref[:, D + n0:D + n0 + cols].astype(F32)
        mg_ref[:, ns] = (ga * pa + gb * pb).astype(BF16)
    x = jnp.where(first, xp_ref[...], xs_ref[...])
    x = x + jnp.dot(mg_ref[...], wo_ref[...], preferred_element_type=F32)
    o_ref[...] = x
    h2_ref[...] = _normed(x, g2_ref)


def _merge_out(xp, xs, oa_p, oa_s, ob_p, ob_s, g, w_pa, w_pb, w_o, g2, *, tm):
    D = xp.shape[1]
    T = xp.shape[0] + xs.shape[0]
    n_first = xp.shape[0] // tm
    fixed = lambda i: (0, 0)
    return pl.pallas_call(
        functools.partial(_merge_kernel, n_first=n_first),
        grid=(T // tm,),
        in_specs=[*_two_streams(tm, D, n_first), *_two_streams(tm, WIDTH, n_first),
                  *_two_streams(tm, WIDTH, n_first),
                  pl.BlockSpec((tm, 2 * D), lambda i: (i, 0)),
                  *[pl.BlockSpec(w.shape, fixed, pipeline_mode=pl.Buffered(1))
                    for w in (w_pa, w_pb, w_o)],
                  pl.BlockSpec((1, D), fixed)],
        out_specs=[pl.BlockSpec((tm, D), lambda i: (i, 0))] * 2,
        out_shape=[jax.ShapeDtypeStruct((T, D), F32), jax.ShapeDtypeStruct((T, D), BF16)],
        scratch_shapes=[pltpu.VMEM((tm, D), BF16)],
        compiler_params=_params("parallel"),
        name="merge_out",
    )(xp, xs, oa_p, oa_s, ob_p, ob_s, g, w_pa, w_pb, w_o, g2)


def _ffn_kernel(x_ref, h_ref, w1_ref, w2_ref, gf_ref, yp_ref, ys_ref, acc_ref, *, n_first):
    i, f = pl.program_id(0), pl.program_id(1)

    @pl.when(f == 0)
    def _():
        acc_ref[...] = x_ref[...]

    u = jnp.maximum(jnp.dot(h_ref[...], w1_ref[...], preferred_element_type=F32), 0.0)
    acc_ref[...] += jnp.dot((u * u).astype(BF16), w2_ref[...], preferred_element_type=F32)

    @pl.when(f == pl.num_programs(1) - 1)
    def _():
        x = acc_ref[...]
        ms = jnp.mean(x * x, axis=-1, keepdims=True)
        y = x * lax.rsqrt(ms + EPS) * gf_ref[...]

        @pl.when(i < n_first)
        def _():
            yp_ref[...] = y

        @pl.when(i >= n_first)
        def _():
            ys_ref[...] = y


def _ffn(x, h, w1, w2, gf, *, rows_first, tm, tf):
    T, D = x.shape
    F = w1.shape[1]
    n_first = rows_first // tm
    rows = pl.BlockSpec((tm, D), lambda i, f: (i, 0))
    return pl.pallas_call(
        functools.partial(_ffn_kernel, n_first=n_first),
        grid=(T // tm, F // tf),
        in_specs=[rows, rows,
                  pl.BlockSpec((D, tf), lambda i, f: (0, f)),
                  pl.BlockSpec((tf, D), lambda i, f: (f, 0)),
                  pl.BlockSpec((1, D), lambda i, f: (0, 0))],
        out_specs=list(_two_streams(tm, D, n_first)),
        out_shape=[jax.ShapeDtypeStruct((rows_first, D), F32),
                   jax.ShapeDtypeStruct((T - rows_first, D), F32)],
        scratch_shapes=[pltpu.VMEM((tm, D), F32)],
        compiler_params=_params("arbitrary", "arbitrary"),
        name="ffn",
    )(x, h, w1, w2, gf)


def _tile(n, pref):
    t = min(pref, n)
    while n % t:
        t //= 2
    return t


def kernel(x_prompt, x_sample, cache_fox_k, cache_fox_v, cache_fox_logf, state_hgrn, norm1, w_in,
           b_fox_f, lb_logits, gnorm_a, w_pa, w_pb, w_o, norm2, w1, w2, norm_f):
    B, L, D = x_prompt.shape
    Bs, Ls, _ = x_sample.shape
    depth, _, P, H, DH = cache_fox_k.shape
    assert depth == 1 and H == N_HEADS and DH == HEAD_DIM
    Tp, Ts = B * L, Bs * Ls
    T = Tp + Ts
    W = WIDTH

    xp, xs = x_prompt.reshape(Tp, D), x_sample.reshape(Ts, D)
    w_in0 = w_in[0]
    w_fl_t = w_in0[:, 7 * W:7 * W + H].T.astype(BF16)
    w_g = w_in0[:, 7 * W + H:]
    g1 = norm1[0].reshape(1, D)

    tm = _tile(math.gcd(Tp, Ts), 512)
    tm1 = _tile(math.gcd(Tp, Ts), 1024)
    h1 = _rmsnorm_rows(xp, xs, g1, tm=_tile(tm, 256))
    a_proj = _proj(h1, w_in0, n_cols=4 * W, act=None, out_dtype=F32, tm=tm1, tn=W // 2)
    gates = _proj(h1, w_g, n_cols=2 * D, act="sigmoid", out_dtype=BF16, tm=tm1,
                  tn=_tile(2 * D, W // 2))
    q_b, k_b, v_b, k_p, k_s, v_p, v_s, fl_t = _proj_fox(h1, w_in0, w_fl_t, col0=4 * W,
                                                        rows_first=Tp, tm=_tile(tm, 512))

    bias = b_fox_f[0].reshape(H, 1)
    lf_p, c_p = _fox_gates(fl_t, bias, col0=0, cols=Tp, seg=L, block=L, pad_rows=LANES)
    lf_s, c_s = _fox_gates(fl_t, bias, col0=Tp, cols=Ts, seg=Ls, block=_tile(Ts, 1024), pad_rows=H)
    cache_bias = _cache_suffix(jnp.transpose(cache_fox_logf[0], (0, 2, 1)))

    zeros_state = jnp.zeros((B, H, DH, DH), F32)
    chunk_p = _tile(L, 64)
    oa_p, s_p = _hgrn2(a_proj, lb_logits, gnorm_a, zeros_state, batch=B, length=L, row0=0,
                       chunk=chunk_p, n_chunks=_tile(L // chunk_p, 4))
    oa_s, s_s = _hgrn2(a_proj, lb_logits, gnorm_a, state_hgrn[0], batch=Bs, length=Ls, row0=Tp,
                       chunk=_tile(Ls, 64), n_chunks=1)

    ob_p = _fox_prompt(q_b, k_b, v_b, c_p, batch=B, length=L, blk=_tile(L, 512))
    c_s3 = c_s.transpose(1, 0, 2).reshape(H, Bs, Ls).transpose(1, 0, 2)
    ob_s = _fox_sample(q_b, k_b, v_b, c_s3, cache_fox_k[0], cache_fox_v[0], cache_bias,
                       row0=Tp, tk=_tile(P, 1024))

    tm5 = _tile(tm, 512)
    x1, h2 = _merge_out(xp, xs, oa_p, oa_s, ob_p, ob_s, gates, w_pa[0].astype(BF16),
                        w_pb[0].astype(BF16), w_o[0].astype(BF16), norm2[0].reshape(1, D),
                        tm=_tile(tm, 256))
    y_p, y_s = _ffn(x1, h2, w1[0].astype(BF16), w2[0].astype(BF16), norm_f.reshape(1, D),
                    rows_first=Tp, tm=tm5, tf=_tile(w1.shape[-1], 1024))

    return (y_p.reshape(B, L, D), y_s.reshape(Bs, Ls, D),
            k_p.reshape(1, B, L, H, DH), v_p.reshape(1, B, L, H, DH),
            lf_p.T.reshape(1, B, L, H), s_p[None],
            k_s.reshape(1, Bs, Ls, H, DH), v_s.reshape(1, Bs, Ls, H, DH),
            lf_s.T.reshape(1, Bs, Ls, H), s_s[None])
```

```python
import functools
import math

import jax
import jax.numpy as jnp
from jax import lax
from jax.experimental import pallas as pl
from jax.experimental.pallas import tpu as pltpu

F32 = jnp.float32
BF16 = jnp.bfloat16

EPS = 1e-6
N_HEADS = 8
HEAD_DIM = 128
WIDTH = N_HEADS * HEAD_DIM
LANES = 128
VMEM_LIMIT = 56 * 1024 * 1024
LOG2E = math.log2(math.e)
QK_SCALE = HEAD_DIM ** -0.5 * LOG2E

NT_DIMS = (((1,), (1,)), ((), ()))
TN_DIMS = (((0,), (0,)), ((), ()))


def _params(*sem):
    return pltpu.CompilerParams(dimension_semantics=sem, vmem_limit_bytes=VMEM_LIMIT)


def _sigmoid(x):
    return 1.0 / (1.0 + jnp.exp(-x))


def _head(h):
    return slice(h * HEAD_DIM, (h + 1) * HEAD_DIM)


def _two_streams(rows, width, n_first, n_second=None):
    mode = {"pipeline_mode": pl.Buffered(1)} if n_second == 1 else {}
    return (pl.BlockSpec((rows, width), lambda i, *_: (jnp.minimum(i, n_first - 1), 0)),
            pl.BlockSpec((rows, width), lambda i, *_: (jnp.maximum(i - n_first, 0), 0), **mode))


def _head_rows(h, rows):
    return pl.ds(h, rows, stride=N_HEADS)


def _normed(x, g_ref):
    ms = jnp.mean(x * x, axis=-1, keepdims=True)
    return (x * lax.rsqrt(ms + EPS) * g_ref[...]).astype(BF16)


def _rmsnorm_kernel(xp_ref, xs_ref, g_ref, o_ref, *, n_first):
    @pl.when(pl.program_id(0) < n_first)
    def _():
        o_ref[...] = _normed(xp_ref[...], g_ref)

    @pl.when(pl.program_id(0) >= n_first)
    def _():
        o_ref[...] = _normed(xs_ref[...], g_ref)


def _rmsnorm_rows(xp, xs, g, *, tm):
    D = xp.shape[1]
    n_first = xp.shape[0] // tm
    T = xp.shape[0] + xs.shape[0]
    return pl.pallas_call(
        functools.partial(_rmsnorm_kernel, n_first=n_first),
        grid=(T // tm,),
        in_specs=[*_two_streams(tm, D, n_first), pl.BlockSpec((1, D), lambda i: (0, 0))],
        out_specs=pl.BlockSpec((tm, D), lambda i: (i, 0)),
        out_shape=jax.ShapeDtypeStruct((T, D), BF16),
        compiler_params=_params("parallel"),
        name="rmsnorm_rows",
    )(xp, xs, g)


def _first_pass_weights(n_blocks, block0, shape, buffers=2):
    return pl.BlockSpec(
        shape, lambda i, j: (block0 + jnp.where(i == 0, j, n_blocks - 1), 0),
        pipeline_mode=pl.Buffered(buffers))


def _proj_kernel(h_ref, w_ref, o_ref, wb_ref, *, act):
    j = pl.program_id(1)

    @pl.when(pl.program_id(0) == 0)
    def _():
        wb_ref[j] = w_ref[...].astype(BF16)

    acc = lax.dot_general(h_ref[...], wb_ref[j], NT_DIMS, preferred_element_type=F32)
    if act == "sigmoid":
        acc = _sigmoid(acc)
    o_ref[...] = acc.astype(o_ref.dtype)


def _proj(h, w, *, n_cols, act, out_dtype, tm, tn):
    T, D = h.shape
    nj = n_cols // tn
    return pl.pallas_call(
        functools.partial(_proj_kernel, act=act),
        grid=(T // tm, nj),
        in_specs=[pl.BlockSpec((tm, D), lambda i, j: (i, 0)),
                  _first_pass_weights(nj, 0, (tn, D))],
        out_specs=pl.BlockSpec((tm, tn), lambda i, j: (i, j)),
        out_shape=jax.ShapeDtypeStruct((T, n_cols), out_dtype),
        scratch_shapes=[pltpu.VMEM((nj, tn, D), BF16)],
        compiler_params=_params("arbitrary", "arbitrary"),
        name="proj_" + (act or "lin"),
    )(h, w)


def _proj_fox_kernel(h_ref, w_ref, wfl_ref,
                     q_ref, kb_ref, vb_ref, kp_ref, ks_ref, vp_ref, vs_ref, fl_ref, wb_ref,
                     *, n_first):
    i, j = pl.program_id(0), pl.program_id(1)

    @pl.when(i == 0)
    def _():
        wb_ref[j] = w_ref[...].astype(BF16)

    @pl.when(j == 0)
    def _():
        fl_ref[...] = lax.dot_general(wfl_ref[...], h_ref[...], NT_DIMS,
                                      preferred_element_type=F32)

    h = h_ref[...]
    tm = h.shape[0]
    pair = 2 * HEAD_DIM

    def column_pairs():
        for c in range(WIDTH // pair):
            cs = slice(c * pair, (c + 1) * pair)
            yield c, cs, lax.dot_general(h, wb_ref[j, cs, :], NT_DIMS,
                                         preferred_element_type=F32)

    @pl.when(j == 0)
    def _():
        for _, cs, acc in column_pairs():
            q_ref[:, cs] = (acc * QK_SCALE).astype(BF16)

    def key_or_value(bf_ref, heads_ref):
        for c, cs, acc in column_pairs():
            bf_ref[:, cs] = acc.astype(BF16)
            for hh in range(2):
                heads_ref[_head_rows(2 * c + hh, tm), :] = acc[:, _head(hh)]

    for jj, bf_ref, first_ref, second_ref in ((1, kb_ref, kp_ref, ks_ref),
                                              (2, vb_ref, vp_ref, vs_ref)):
        pl.when((j == jj) & (i < n_first))(
            functools.partial(key_or_value, bf_ref, first_ref))
        pl.when((j == jj) & (i >= n_first))(
            functools.partial(key_or_value, bf_ref, second_ref))


def _proj_fox(h, w, w_fl_t, *, col0, rows_first, tm):
    T, D = h.shape
    Tp, Ts = rows_first, T - rows_first
    n_first = Tp // tm
    kv_p, kv_s = _two_streams(tm * N_HEADS, HEAD_DIM, n_first)
    kv_shape = lambda rows: jax.ShapeDtypeStruct((rows * N_HEADS, HEAD_DIM), F32)
    tok = pl.BlockSpec((tm, WIDTH), lambda i, j: (i, 0))
    tok_shape = jax.ShapeDtypeStruct((T, WIDTH), BF16)
    return pl.pallas_call(
        functools.partial(_proj_fox_kernel, n_first=n_first),
        grid=(T // tm, 3),
        in_specs=[pl.BlockSpec((tm, D), lambda i, j: (i, 0)),
                  _first_pass_weights(3, col0 // WIDTH, (WIDTH, D), buffers=1),
                  pl.BlockSpec((N_HEADS, D), lambda i, j: (0, 0))],
        out_specs=[tok, tok, tok, kv_p, kv_s, kv_p, kv_s,
                   pl.BlockSpec((N_HEADS, tm), lambda i, j: (0, i))],
        out_shape=[tok_shape, tok_shape, tok_shape,
                   kv_shape(Tp), kv_shape(Ts), kv_shape(Tp), kv_shape(Ts),
                   jax.ShapeDtypeStruct((N_HEADS, T), F32)],
        scratch_shapes=[pltpu.VMEM((3, WIDTH, D), BF16)],
        compiler_params=_params("arbitrary", "arbitrary"),
        name="proj_fox",
    )(h, w, w_fl_t)


def _lane_cumsum(x, seg):
    pos = lax.broadcasted_iota(jnp.int32, x.shape, x.ndim - 1) & (seg - 1)
    shift = 1
    while shift < seg:
        x = x + jnp.where(pos >= shift, pltpu.roll(x, shift, x.ndim - 1), 0.0)
        shift *= 2
    return x


def _gate_kernel(fl_ref, bias_ref, lf_ref, c_ref, *, seg):
    z = fl_ref[...] + bias_ref[...]
    lf = jnp.minimum(z, 0.0) - jnp.log(1.0 + jnp.exp(-jnp.abs(z)))
    lf_ref[...] = lf
    c = _lane_cumsum(lf, seg) * LOG2E
    c_ref[...] = jnp.zeros(c_ref.shape, F32)
    c_ref[0:c.shape[0], :] = c


def _fox_gates(fl_t, bias, *, col0, cols, seg, block, pad_rows):
    H = fl_t.shape[0]
    nb = cols // block
    base = col0 // block
    return pl.pallas_call(
        functools.partial(_gate_kernel, seg=seg),
        grid=(nb,),
        in_specs=[pl.BlockSpec((H, block), lambda i: (0, base + i)),
                  pl.BlockSpec((H, 1), lambda i: (0, 0))],
        out_specs=[pl.BlockSpec((H, block), lambda i: (0, i)),
                   pl.BlockSpec((None, pad_rows, block), lambda i: (i, 0, 0))],
        out_shape=[jax.ShapeDtypeStruct((H, cols), F32),
                   jax.ShapeDtypeStruct((nb, pad_rows, block), F32)],
        compiler_params=_params("parallel"),
        name="fox_gates",
    )(fl_t, bias)


def _suffix_kernel(lf_ref, o_ref):
    lf = lf_ref[...]
    c = _lane_cumsum(lf, lf.shape[-1])
    o_ref[...] = (c[:, -1:] - c) * LOG2E


def _cache_suffix(clf_t):
    B, H, P = clf_t.shape
    spec = pl.BlockSpec((None, H, P), lambda b: (b, 0, 0))
    return pl.pallas_call(
        _suffix_kernel,
        grid=(B,),
        in_specs=[spec],
        out_specs=spec,
        out_shape=jax.ShapeDtypeStruct((B, H, P), F32),
        compiler_params=_params("parallel"),
        name="fox_cache_suffix",
    )(clf_t)


SUB = 16


def _gla_kernel(aq_ref, af_ref, ai_ref, ag_ref, lbl_ref, gn_ref, s0_ref, tri_ref,
                o_ref, s_out_ref, st_ref, *, chunk, n_chunks):
    t = pl.program_id(1)

    @pl.when(t == 0)
    def _():
        for h in range(N_HEADS):
            st_ref[h] = s0_ref[h].T

    lbl = lbl_ref[...]
    e = jnp.exp(lbl - jnp.max(lbl, axis=0, keepdims=True))
    lb = e[0:1] / jnp.sum(e, axis=0, keepdims=True)
    gn = gn_ref[...]
    tri = tri_ref[...]
    n_sub = chunk // SUB

    def one_chunk(ci):
        r = slice(ci * chunk, (ci + 1) * chunk)
        fa = lb + (1.0 - lb) * _sigmoid(af_ref[r, :])
        lf = jnp.log(fa)
        ka = 1.0 - fa
        aq = aq_ref[r, :]
        qa = aq * _sigmoid(aq)
        v = ai_ref[r, :].astype(BF16)
        ag = ag_ref[r, :]
        gate = ag * _sigmoid(ag)

        hi = lf.astype(BF16)
        r1 = lf - hi.astype(F32)
        mid = r1.astype(BF16)
        lo = (r1 - mid.astype(F32)).astype(BF16)
        b = (jnp.dot(tri, hi, preferred_element_type=F32)
             + jnp.dot(tri, mid, preferred_element_type=F32)
             + jnp.dot(tri, lo, preferred_element_type=F32))
        b_end = b[chunk - 1:chunk, :]

        q_in = (qa * jnp.exp(b)).astype(BF16)
        k_out = (ka * jnp.exp(b_end - b)).astype(BF16)
        decay = jnp.exp(b_end)
        qp, kp = [], []
        for i in range(n_sub):
            r0, n = i * SUB, (i + 1) * SUB
            m_i = b[r0 + SUB // 2:r0 + SUB // 2 + 1, :]
            qp.append((qa[r0:n] * jnp.exp(b[r0:n] - m_i)).astype(BF16))
            kp.append((ka[:n] * jnp.exp(m_i - b[:n])).astype(BF16))

        st = [st_ref[h] for h in range(N_HEADS)]
        inter = [lax.dot_general(q_in[:, _head(h)], st[h].astype(BF16), NT_DIMS,
                                 preferred_element_type=F32) for h in range(N_HEADS)]
        att = [[lax.dot_general(qp[i][:, _head(h)], kp[i][:, _head(h)], NT_DIMS,
                                preferred_element_type=F32) for h in range(N_HEADS)]
               for i in range(n_sub)]
        for h in range(N_HEADS):
            st_ref[h] = st[h] * decay[:, _head(h)] + lax.dot_general(
                v[:, _head(h)], k_out[:, _head(h)], TN_DIMS, preferred_element_type=F32)
        for i in range(n_sub):
            r0, n = i * SUB, (i + 1) * SUB
            causal = (lax.broadcasted_iota(jnp.int32, (SUB, n), 1)
                      <= lax.broadcasted_iota(jnp.int32, (SUB, n), 0) + r0)
            att[i] = [jnp.where(causal, a, 0.0).astype(BF16) for a in att[i]]
        for h in range(N_HEADS):
            o = jnp.concatenate(
                [inter[h][i * SUB:(i + 1) * SUB]
                 + jnp.dot(att[i][h], v[:(i + 1) * SUB, _head(h)], preferred_element_type=F32)
                 for i in range(n_sub)], axis=0)
            ms = jnp.mean(o * o, axis=-1, keepdims=True)
            o_ref[r, _head(h)] = (o * lax.rsqrt(ms + EPS) * gn * gate[:, _head(h)]).astype(o_ref.dtype)

    for ci in range(n_chunks):
        one_chunk(ci)

    @pl.when(t == pl.num_programs(1) - 1)
    def _():
        for h in range(N_HEADS):
            s_out_ref[h] = st_ref[h].T


def _hgrn2(a_proj, lb_logits, gnorm, s0, *, batch, length, row0, chunk, n_chunks):
    step = chunk * n_chunks
    nt = length // step
    base = row0 // step

    def col(g):
        return pl.BlockSpec((step, WIDTH), lambda b, t: (base + b * nt + t, g))

    st_spec = pl.BlockSpec((None, N_HEADS, HEAD_DIM, HEAD_DIM), lambda b, t: (b, 0, 0, 0))
    tri = jnp.tril(jnp.ones((chunk, chunk), BF16))
    return pl.pallas_call(
        functools.partial(_gla_kernel, chunk=chunk, n_chunks=n_chunks),
        grid=(batch, nt),
        in_specs=[col(0), col(1), col(2), col(3),
                  pl.BlockSpec(lb_logits.shape, lambda b, t: (0, 0)),
                  pl.BlockSpec((1, HEAD_DIM), lambda b, t: (0, 0)),
                  st_spec,
                  pl.BlockSpec((chunk, chunk), lambda b, t: (0, 0))],
        out_specs=[pl.BlockSpec((step, WIDTH), lambda b, t: (b * nt + t, 0)), st_spec],
        out_shape=[jax.ShapeDtypeStruct((batch * length, WIDTH), BF16),
                   jax.ShapeDtypeStruct((batch, N_HEADS, HEAD_DIM, HEAD_DIM), F32)],
        scratch_shapes=[pltpu.VMEM((N_HEADS, HEAD_DIM, HEAD_DIM), F32)],
        compiler_params=_params("parallel", "arbitrary"),
        name="hgrn2",
    )(a_proj, a_proj, a_proj, a_proj, lb_logits, gnorm, s0, tri)


def _attn_prompt_kernel(q_ref, k_ref, v_ref, c_ref, o_ref,
                        vt_ref, cb_ref, s_ref, m_ref, l_ref, acc_ref, *, blk):
    h = pl.program_id(1)
    length = k_ref.shape[0]
    halves = (slice(0, blk // 2), slice(blk // 2, blk))

    for r0 in range(0, length, blk):
        rs = slice(r0, r0 + blk)
        vt_ref[:, rs] = v_ref[rs, :].astype(F32).T.astype(BF16)
        c_cols = c_ref[:, rs].T
        onehot = lax.broadcasted_iota(jnp.int32, c_cols.shape, 1) == h
        cb_ref[rs, :] = jnp.broadcast_to(
            jnp.sum(jnp.where(onehot, c_cols, 0.0), axis=1, keepdims=True), (blk, LANES))

    def scores(iq, j, buf):
        ks = slice(j * blk, (j + 1) * blk)
        kb = k_ref[ks, :]
        cb = cb_ref[ks, :]
        for hv in halves:
            x = lax.dot_general(kb, q_ref[iq * blk + hv.start:iq * blk + hv.stop, :], NT_DIMS,
                                preferred_element_type=F32)
            s_ref[buf, :, hv] = x - jnp.concatenate([cb] * (x.shape[1] // LANES), axis=1)

    def consume(iq, j, buf):
        st = iq % 2
        masked = j == iq
        vt = vt_ref[:, j * blk:(j + 1) * blk]
        p = []
        for hv in halves:
            keys = hv.stop if masked else blk
            x = s_ref[buf, :keys, hv]
            if masked:
                key = lax.broadcasted_iota(jnp.int32, x.shape, 0)
                qry = lax.broadcasted_iota(jnp.int32, x.shape, 1) + hv.start
                x = jnp.where(key <= qry, x, -jnp.inf)
            if j == 0:
                m_new = jnp.max(x, axis=0, keepdims=True)
                e = jnp.exp2(x - m_new)
                l_ref[st, :, hv] = jnp.sum(e, axis=0, keepdims=True)
            else:
                m_prev = m_ref[st, :, hv]
                m_new = jnp.maximum(m_prev, jnp.max(x, axis=0, keepdims=True))
                alpha = jnp.exp2(m_prev - m_new)
                e = jnp.exp2(x - m_new)
                l_ref[st, :, hv] = alpha * l_ref[st, :, hv] + jnp.sum(e, axis=0, keepdims=True)
                acc_ref[st, :, hv] = alpha * acc_ref[st, :, hv]
            m_ref[st, :, hv] = m_new
            p.append(e.astype(BF16))
        for e, hv in zip(p, halves):
            pv = jnp.dot(vt[:, :e.shape[0]], e, preferred_element_type=F32)
            if j == 0:
                acc_ref[st, :, hv] = pv
            else:
                acc_ref[st, :, hv] += pv

    visits = [(iq, j) for iq in range(length // blk) for j in range(iq + 1)]
    scores(*visits[0], 0)
    for n, (iq, j) in enumerate(visits):
        if n + 1 < len(visits):
            scores(*visits[n + 1], (n + 1) % 2)
        consume(iq, j, n % 2)
        if j == iq:
            st = iq % 2
            o_ref[iq * blk:(iq + 1) * blk, :] = (acc_ref[st] / l_ref[st]).T.astype(o_ref.dtype)


def _fox_prompt(q, k, v, c, *, batch, length, blk):
    spec = pl.BlockSpec((length, HEAD_DIM), lambda b, h: (b, h))
    return pl.pallas_call(
        functools.partial(_attn_prompt_kernel, blk=blk),
        grid=(batch, N_HEADS),
        in_specs=[spec, spec, spec, pl.BlockSpec((None,) + c.shape[1:], lambda b, h: (b, 0, 0))],
        out_specs=spec,
        out_shape=jax.ShapeDtypeStruct((batch * length, WIDTH), BF16),
        scratch_shapes=[pltpu.VMEM((HEAD_DIM, length), BF16),
                        pltpu.VMEM((length, LANES), F32),
                        pltpu.VMEM((2, blk, blk), F32),
                        pltpu.VMEM((2, 1, blk), F32), pltpu.VMEM((2, 1, blk), F32),
                        pltpu.VMEM((2, HEAD_DIM, blk), F32)],
        compiler_params=_params("parallel", "parallel"),
        name="fox_prompt",
    )(q, k, v, c)


def _attend_heads(score_fn, value_fn, m_ref, l_ref, acc_ref):
    s = [score_fn(h) for h in range(N_HEADS)]
    stats = []
    for h in range(N_HEADS):
        m_prev = m_ref[h]
        m_new = jnp.maximum(m_prev, jnp.max(s[h], axis=-1, keepdims=True))
        stats.append((jnp.exp2(m_prev - m_new), m_new))
    p = [jnp.exp2(s[h] - stats[h][1]) for h in range(N_HEADS)]
    for h in range(N_HEADS):
        alpha, m_new = stats[h]
        l_ref[h] = alpha * l_ref[h] + jnp.sum(p[h], axis=-1, keepdims=True)
        m_ref[h] = m_new
        acc_ref[:, _head(h)] = alpha * acc_ref[:, _head(h)] + jnp.dot(
            p[h].astype(BF16), value_fn(h), preferred_element_type=F32)


def _attn_sample_kernel(q_ref, kc_hbm, vc_hbm, bc_ref, kn_ref, vn_ref, bn_ref, o_ref,
                        kbuf_ref, vbuf_ref, sem_ref, m_ref, l_ref, acc_ref):
    b, j = pl.program_id(0), pl.program_id(1)
    nj = pl.num_programs(1)
    tk = bc_ref.shape[1]
    step = b * nj + j
    slot = step % 2

    def cache_copies(stream, blk, into):
        return [pltpu.make_async_copy(hbm.at[stream, pl.ds(blk * tk, tk), h, :],
                                      buf.at[into, h], sem_ref.at[t, into])
                for t, (hbm, buf) in enumerate(((kc_hbm, kbuf_ref), (vc_hbm, vbuf_ref)))
                for h in range(N_HEADS)]

    def start_all(copies):
        for cp in copies:
            cp.start()

    @pl.when(step == 0)
    def _():
        start_all(cache_copies(b, j, slot))

    @pl.when(step + 1 < pl.num_programs(0) * nj)
    def _():
        start_all(cache_copies((step + 1) // nj, (step + 1) % nj, 1 - slot))

    @pl.when(j == 0)
    def _():
        m_ref[...] = jnp.full(m_ref.shape, -jnp.inf, F32)
        l_ref[...] = jnp.zeros(l_ref.shape, F32)
        acc_ref[...] = jnp.zeros(acc_ref.shape, F32)

    for cp in cache_copies(b, j, slot):
        cp.wait()
    _attend_heads(
        lambda h: lax.dot_general(q_ref[:, _head(h)], kbuf_ref[slot, h].astype(BF16),
                                  NT_DIMS, preferred_element_type=F32) + bc_ref[h:h + 1, :],
        lambda h: vbuf_ref[slot, h].astype(BF16), m_ref, l_ref, acc_ref)

    @pl.when(j == pl.num_programs(1) - 1)
    def _():
        tq = q_ref.shape[0]
        causal = (lax.broadcasted_iota(jnp.int32, (tq, tq), 1)
                  <= lax.broadcasted_iota(jnp.int32, (tq, tq), 0))
        _attend_heads(
            lambda h: jnp.where(causal,
                                lax.dot_general(q_ref[:, _head(h)], kn_ref[:, _head(h)], NT_DIMS,
                                                preferred_element_type=F32) - bn_ref[h:h + 1, :],
                                -jnp.inf),
            lambda h: vn_ref[:, _head(h)], m_ref, l_ref, acc_ref)
        for h in range(N_HEADS):
            o_ref[:, _head(h)] = (acc_ref[:, _head(h)] / l_ref[h]).astype(o_ref.dtype)


def _fox_sample(q, k_new, v_new, c_new, cache_k, cache_v, cache_bias, *, row0, tk):
    B, H, P = cache_bias.shape
    Ls = c_new.shape[2]
    base = row0 // Ls
    new = pl.BlockSpec((Ls, WIDTH), lambda b, j: (base + b, 0))
    cache = pl.BlockSpec(memory_space=pl.ANY)
    cache_buf = pltpu.VMEM((2, H, tk, HEAD_DIM), F32)
    return pl.pallas_call(
        _attn_sample_kernel,
        grid=(B, P // tk),
        in_specs=[new, cache, cache,
                  pl.BlockSpec((None, N_HEADS, tk), lambda b, j: (b, 0, j)),
                  new, new,
                  pl.BlockSpec((None, N_HEADS, Ls), lambda b, j: (b, 0, 0))],
        out_specs=pl.BlockSpec((Ls, WIDTH), lambda b, j: (b, 0)),
        out_shape=jax.ShapeDtypeStruct((B * Ls, WIDTH), BF16),
        scratch_shapes=[cache_buf, cache_buf, pltpu.SemaphoreType.DMA((2, 2)),
                        pltpu.VMEM((N_HEADS, Ls, 1), F32), pltpu.VMEM((N_HEADS, Ls, 1), F32),
                        pltpu.VMEM((Ls, WIDTH), F32)],
        compiler_params=_params("arbitrary", "arbitrary"),
        name="fox_sample",
    )(q, cache_k, cache_v, cache_bias, k_new, v_new, c_new)


MERGE_COLS = 512


def _merge_kernel(xp_ref, xs_ref, oap_ref, oas_ref, obp_ref, obs_ref, g_ref,
                  wpa_ref, wpb_ref, wo_ref, g2_ref, o_ref, h2_ref, mg_ref, *, n_first):
    first = pl.program_id(0) < n_first
    D = xp_ref.shape[1]
    oa = jnp.where(first, oap_ref[...], oas_ref[...])
    ob = jnp.where(first, obp_ref[...], obs_ref[...])
    cols = min(MERGE_COLS, D)
    for n0 in range(0, D, cols):
        ns = slice(n0, n0 + cols)
        pa = jnp.dot(oa, wpa_ref[:, ns], preferred_element_type=F32)
        pb = jnp.dot(ob, wpb_ref[:, ns], preferred_element_type=F32)
        ga = g_ref[:, ns].astype(F32)
        gb = g_ref[:, D + n0:D + n0 + cols].astype(F32)
        mg_ref[:, ns] = (ga * pa + gb * pb).astype(BF16)
    x = jnp.where(first, xp_ref[...], xs_ref[...])
    x = x + jnp.dot(mg_ref[...], wo_ref[...], preferred_element_type=F32)
    o_ref[...] = x
    h2_ref[...] = _normed(x, g2_ref)


def _merge_out(xp, xs, oa_p, oa_s, ob_p, ob_s, g, w_pa, w_pb, w_o, g2, *, tm):
    D = xp.shape[1]
    T = xp.shape[0] + xs.shape[0]
    n_first = xp.shape[0] // tm
    fixed = lambda i: (0, 0)
    return pl.pallas_call(
        functools.partial(_merge_kernel, n_first=n_first),
        grid=(T // tm,),
        in_specs=[*_two_streams(tm, D, n_first), *_two_streams(tm, WIDTH, n_first),
                  *_two_streams(tm, WIDTH, n_first),
                  pl.BlockSpec((tm, 2 * D), lambda i: (i, 0)),
                  *[pl.BlockSpec(w.shape, fixed, pipeline_mode=pl.Buffered(1))
                    for w in (w_pa, w_pb, w_o)],
                  pl.BlockSpec((1, D), fixed)],
        out_specs=[pl.BlockSpec((tm, D), lambda i: (i, 0))] * 2,
        out_shape=[jax.ShapeDtypeStruct((T, D), F32), jax.ShapeDtypeStruct((T, D), BF16)],
        scratch_shapes=[pltpu.VMEM((tm, D), BF16)],
        compiler_params=_params("parallel"),
        name="merge_out",
    )(xp, xs, oa_p, oa_s, ob_p, ob_s, g, w_pa, w_pb, w_o, g2)


def _ffn_kernel(x_ref, h_ref, w1_ref, w2_ref, gf_ref, yp_ref, ys_ref, acc_ref, *, n_first):
    i, f = pl.program_id(0), pl.program_id(1)

    @pl.when(f == 0)
    def _():
        acc_ref[...] = x_ref[...]

    u = jnp.maximum(jnp.dot(h_ref[...], w1_ref[...], preferred_element_type=F32), 0.0)
    acc_ref[...] += jnp.dot((u * u).astype(BF16), w2_ref[...], preferred_element_type=F32)

    @pl.when(f == pl.num_programs(1) - 1)
    def _():
        x = acc_ref[...]
        ms = jnp.mean(x * x, axis=-1, keepdims=True)
        y = x * lax.rsqrt(ms + EPS) * gf_ref[...]

        @pl.when(i < n_first)
        def _():
            yp_ref[...] = y

        @pl.when(i >= n_first)
        def _():
            ys_ref[...] = y


def _ffn(x, h, w1, w2, gf, *, rows_first, tm, tf):
    T, D = x.shape
    F = w1.shape[1]
    n_first = rows_first // tm
    rows = pl.BlockSpec((tm, D), lambda i, f: (i, 0))
    return pl.pallas_call(
        functools.partial(_ffn_kernel, n_first=n_first),
        grid=(T // tm, F // tf),
        in_specs=[rows, rows,
                  pl.BlockSpec((D, tf), lambda i, f: (0, f)),
                  pl.BlockSpec((tf, D), lambda i, f: (f, 0)),
                  pl.BlockSpec((1, D), lambda i, f: (0, 0))],
        out_specs=list(_two_streams(tm, D, n_first)),
        out_shape=[jax.ShapeDtypeStruct((rows_first, D), F32),
                   jax.ShapeDtypeStruct((T - rows_first, D), F32)],
        scratch_shapes=[pltpu.VMEM((tm, D), F32)],
        compiler_params=_params("arbitrary", "arbitrary"),
        name="ffn",
    )(x, h, w1, w2, gf)


def _tile(n, pref):
    t = min(pref, n)
    while n % t:
        t //= 2
    return t


def kernel(x_prompt, x_sample, cache_fox_k, cache_fox_v, cache_fox_logf, state_hgrn, norm1, w_in,
           b_fox_f, lb_logits, gnorm_a, w_pa, w_pb, w_o, norm2, w1, w2, norm_f):
    B, L, D = x_prompt.shape
    Bs, Ls, _ = x_sample.shape
    depth, _, P, H, DH = cache_fox_k.shape
    assert depth == 1 and H == N_HEADS and DH == HEAD_DIM
    Tp, Ts = B * L, Bs * Ls
    T = Tp + Ts
    W = WIDTH

    xp, xs = x_prompt.reshape(Tp, D), x_sample.reshape(Ts, D)
    w_in_t = jnp.swapaxes(w_in[0], 0, 1)
    w_fl_t = w_in_t[7 * W:7 * W + H].astype(BF16)
    w_g = w_in_t[7 * W + H:]
    g1 = norm1[0].reshape(1, D)

    tm = _tile(math.gcd(Tp, Ts), 512)
    tm1 = _tile(math.gcd(Tp, Ts), 1024)
    h1 = _rmsnorm_rows(xp, xs, g1, tm=_tile(tm, 256))
    a_proj = _proj(h1, w_in_t, n_cols=4 * W, act=None, out_dtype=F32, tm=tm1, tn=W // 2)
    gates = _proj(h1, w_g, n_cols=2 * D, act="sigmoid", out_dtype=BF16, tm=tm1,
                  tn=_tile(2 * D, W // 2))
    q_b, k_b, v_b, k_p, k_s, v_p, v_s, fl_t = _proj_fox(h1, w_in_t, w_fl_t, col0=4 * W,
                                                        rows_first=Tp, tm=_tile(tm, 512))

    bias = b_fox_f[0].reshape(H, 1)
    lf_p, c_p = _fox_gates(fl_t, bias, col0=0, cols=Tp, seg=L, block=L, pad_rows=LANES)
    lf_s, c_s = _fox_gates(fl_t, bias, col0=Tp, cols=Ts, seg=Ls, block=_tile(Ts, 1024), pad_rows=H)
    cache_bias = _cache_suffix(jnp.transpose(cache_fox_logf[0], (0, 2, 1)))

    zeros_state = jnp.zeros((B, H, DH, DH), F32)
    chunk_p = _tile(L, 64)
    oa_p, s_p = _hgrn2(a_proj, lb_logits, gnorm_a, zeros_state, batch=B, length=L, row0=0,
                       chunk=chunk_p, n_chunks=_tile(L // chunk_p, 4))
    oa_s, s_s = _hgrn2(a_proj, lb_logits, gnorm_a, state_hgrn[0], batch=Bs, length=Ls, row0=Tp,
                       chunk=_tile(Ls, 64), n_chunks=1)

    ob_p = _fox_prompt(q_b, k_b, v_b, c_p, batch=B, length=L, blk=_tile(L, 512))
    c_s3 = c_s.transpose(1, 0, 2).reshape(H, Bs, Ls).transpose(1, 0, 2)
    ob_s = _fox_sample(q_b, k_b, v_b, c_s3, cache_fox_k[0], cache_fox_v[0], cache_bias,
                       row0=Tp, tk=_tile(P, 1024))

    tm5 = _tile(tm, 512)
    x1, h2 = _merge_out(xp, xs, oa_p, oa_s, ob_p, ob_s, gates, w_pa[0].astype(BF16),
                        w_pb[0].astype(BF16), w_o[0].astype(BF16), norm2[0].reshape(1, D),
                        tm=_tile(tm, 256))
    y_p, y_s = _ffn(x1, h2, w1[0].astype(BF16), w2[0].astype(BF16), norm_f.reshape(1, D),
                    rows_first=Tp, tm=tm5, tf=_tile(w1.shape[-1], 1024))

    return (y_p.reshape(B, L, D), y_s.reshape(Bs, Ls, D),
            k_p.reshape(1, B, L, H, DH), v_p.reshape(1, B, L, H, DH),
            lf_p.T.reshape(1, B, L, H), s_p[None],
            k_s.reshape(1, Bs, Ls, H, DH), v_s.reshape(1, Bs, Ls, H, DH),
            lf_s.T.reshape(1, Bs, Ls, H), s_s[None])
```

```python
import functools
import math

import jax
import jax.numpy as jnp
from jax import lax
from jax.experimental import pallas as pl
from jax.experimental.pallas import tpu as pltpu

F32 = jnp.float32
BF16 = jnp.bfloat16

EPS = 1e-6
N_HEADS = 8
HEAD_DIM = 128
WIDTH = N_HEADS * HEAD_DIM
LANES = 128
VMEM_LIMIT = 56 * 1024 * 1024
LOG2E = math.log2(math.e)
QK_SCALE = HEAD_DIM ** -0.5 * LOG2E

NT_DIMS = (((1,), (1,)), ((), ()))
TN_DIMS = (((0,), (0,)), ((), ()))


def _params(*sem):
    return pltpu.CompilerParams(dimension_semantics=sem, vmem_limit_bytes=VMEM_LIMIT)


def _sigmoid(x):
    return 1.0 / (1.0 + jnp.exp(-x))


def _head(h):
    return slice(h * HEAD_DIM, (h + 1) * HEAD_DIM)


def _two_streams(rows, width, n_first, n_second=None):
    mode = {"pipeline_mode": pl.Buffered(1)} if n_second == 1 else {}
    return (pl.BlockSpec((rows, width), lambda i, *_: (jnp.minimum(i, n_first - 1), 0)),
            pl.BlockSpec((rows, width), lambda i, *_: (jnp.maximum(i - n_first, 0), 0), **mode))


def _head_rows(h, rows):
    return pl.ds(h, rows, stride=N_HEADS)


def _normed(x, g_ref):
    ms = jnp.mean(x * x, axis=-1, keepdims=True)
    return (x * lax.rsqrt(ms + EPS) * g_ref[...]).astype(BF16)


def _rmsnorm_kernel(xp_ref, xs_ref, g_ref, o_ref, *, n_first):
    @pl.when(pl.program_id(0) < n_first)
    def _():
        o_ref[...] = _normed(xp_ref[...], g_ref)

    @pl.when(pl.program_id(0) >= n_first)
    def _():
        o_ref[...] = _normed(xs_ref[...], g_ref)


def _rmsnorm_rows(xp, xs, g, *, tm):
    D = xp.shape[1]
    n_first = xp.shape[0] // tm
    T = xp.shape[0] + xs.shape[0]
    return pl.pallas_call(
        functools.partial(_rmsnorm_kernel, n_first=n_first),
        grid=(T // tm,),
        in_specs=[*_two_streams(tm, D, n_first), pl.BlockSpec((1, D), lambda i: (0, 0))],
        out_specs=pl.BlockSpec((tm, D), lambda i: (i, 0)),
        out_shape=jax.ShapeDtypeStruct((T, D), BF16),
        compiler_params=_params("parallel"),
        name="rmsnorm_rows",
    )(xp, xs, g)


def _first_pass_weights(n_blocks, block0, shape, buffers=2):
    return pl.BlockSpec(
        shape, lambda i, j: (block0 + jnp.where(i == 0, j, n_blocks - 1), 0),
        pipeline_mode=pl.Buffered(buffers))


def _proj_kernel(h_ref, w_ref, o_ref, wb_ref, *, act):
    j = pl.program_id(1)

    @pl.when(pl.program_id(0) == 0)
    def _():
        wb_ref[j] = w_ref[...].astype(BF16)

    acc = lax.dot_general(h_ref[...], wb_ref[j], NT_DIMS, preferred_element_type=F32)
    if act == "sigmoid":
        acc = _sigmoid(acc)
    o_ref[...] = acc.astype(o_ref.dtype)


def _proj(h, w, *, n_cols, act, out_dtype, tm, tn):
    T, D = h.shape
    nj = n_cols // tn
    return pl.pallas_call(
        functools.partial(_proj_kernel, act=act),
        grid=(T // tm, nj),
        in_specs=[pl.BlockSpec((tm, D), lambda i, j: (i, 0)),
                  _first_pass_weights(nj, 0, (tn, D))],
        out_specs=pl.BlockSpec((tm, tn), lambda i, j: (i, j)),
        out_shape=jax.ShapeDtypeStruct((T, n_cols), out_dtype),
        scratch_shapes=[pltpu.VMEM((nj, tn, D), BF16)],
        compiler_params=_params("arbitrary", "arbitrary"),
        name="proj_" + (act or "lin"),
    )(h, w)


def _proj_fox_kernel(h_ref, w_ref, wfl_ref,
                     q_ref, kb_ref, vb_ref, kp_ref, ks_ref, vp_ref, vs_ref, fl_ref, wb_ref,
                     *, n_first):
    i, j = pl.program_id(0), pl.program_id(1)

    @pl.when(i == 0)
    def _():
        wb_ref[j] = w_ref[...].astype(BF16)

    @pl.when(j == 0)
    def _():
        fl_ref[...] = lax.dot_general(wfl_ref[...], h_ref[...], NT_DIMS,
                                      preferred_element_type=F32)

    h = h_ref[...]
    tm = h.shape[0]
    pair = 2 * HEAD_DIM

    def column_pairs():
        for c in range(WIDTH // pair):
            cs = slice(c * pair, (c + 1) * pair)
            yield c, cs, lax.dot_general(h, wb_ref[j, cs, :], NT_DIMS,
                                         preferred_element_type=F32)

    @pl.when(j == 0)
    def _():
        for _, cs, acc in column_pairs():
            q_ref[:, cs] = (acc * QK_SCALE).astype(BF16)

    def key_or_value(bf_ref, heads_ref):
        for c, cs, acc in column_pairs():
            bf_ref[:, cs] = acc.astype(BF16)
            for hh in range(2):
                heads_ref[_head_rows(2 * c + hh, tm), :] = acc[:, _head(hh)]

    for jj, bf_ref, first_ref, second_ref in ((1, kb_ref, kp_ref, ks_ref),
                                              (2, vb_ref, vp_ref, vs_ref)):
        pl.when((j == jj) & (i < n_first))(
            functools.partial(key_or_value, bf_ref, first_ref))
        pl.when((j == jj) & (i >= n_first))(
            functools.partial(key_or_value, bf_ref, second_ref))


def _proj_fox(h, w, w_fl_t, *, col0, rows_first, tm):
    T, D = h.shape
    Tp, Ts = rows_first, T - rows_first
    n_first = Tp // tm
    kv_p, kv_s = _two_streams(tm * N_HEADS, HEAD_DIM, n_first)
    kv_shape = lambda rows: jax.ShapeDtypeStruct((rows * N_HEADS, HEAD_DIM), F32)
    tok = pl.BlockSpec((tm, WIDTH), lambda i, j: (i, 0))
    tok_shape = jax.ShapeDtypeStruct((T, WIDTH), BF16)
    return pl.pallas_call(
        functools.partial(_proj_fox_kernel, n_first=n_first),
        grid=(T // tm, 3),
        in_specs=[pl.BlockSpec((tm, D), lambda i, j: (i, 0)),
                  _first_pass_weights(3, col0 // WIDTH, (WIDTH, D), buffers=1),
                  pl.BlockSpec((N_HEADS, D), lambda i, j: (0, 0))],
        out_specs=[tok, tok, tok, kv_p, kv_s, kv_p, kv_s,
                   pl.BlockSpec((N_HEADS, tm), lambda i, j: (0, i))],
        out_shape=[tok_shape, tok_shape, tok_shape,
                   kv_shape(Tp), kv_shape(Ts), kv_shape(Tp), kv_shape(Ts),
                   jax.ShapeDtypeStruct((N_HEADS, T), F32)],
        scratch_shapes=[pltpu.VMEM((3, WIDTH, D), BF16)],
        compiler_params=_params("arbitrary", "arbitrary"),
        name="proj_fox",
    )(h, w, w_fl_t)


def _lane_cumsum(x, seg):
    pos = lax.broadcasted_iota(jnp.int32, x.shape, x.ndim - 1) & (seg - 1)
    shift = 1
    while shift < seg:
        x = x + jnp.where(pos >= shift, pltpu.roll(x, shift, x.ndim - 1), 0.0)
        shift *= 2
    return x


def _gate_kernel(fl_ref, bias_ref, lf_ref, c_ref, *, seg):
    z = fl_ref[...] + bias_ref[...]
    lf = jnp.minimum(z, 0.0) - jnp.log(1.0 + jnp.exp(-jnp.abs(z)))
    lf_ref[...] = lf
    c = _lane_cumsum(lf, seg) * LOG2E
    c_ref[...] = jnp.zeros(c_ref.shape, F32)
    c_ref[0:c.shape[0], :] = c


def _fox_gates(fl_t, bias, *, col0, cols, seg, block, pad_rows):
    H = fl_t.shape[0]
    nb = cols // block
    base = col0 // block
    return pl.pallas_call(
        functools.partial(_gate_kernel, seg=seg),
        grid=(nb,),
        in_specs=[pl.BlockSpec((H, block), lambda i: (0, base + i)),
                  pl.BlockSpec((H, 1), lambda i: (0, 0))],
        out_specs=[pl.BlockSpec((H, block), lambda i: (0, i)),
                   pl.BlockSpec((None, pad_rows, block), lambda i: (i, 0, 0))],
        out_shape=[jax.ShapeDtypeStruct((H, cols), F32),
                   jax.ShapeDtypeStruct((nb, pad_rows, block), F32)],
        compiler_params=_params("parallel"),
        name="fox_gates",
    )(fl_t, bias)


def _suffix_kernel(lf_ref, o_ref):
    lf = lf_ref[...]
    c = _lane_cumsum(lf, lf.shape[-1])
    o_ref[...] = (c[:, -1:] - c) * LOG2E


def _cache_suffix(clf_t):
    B, H, P = clf_t.shape
    spec = pl.BlockSpec((None, H, P), lambda b: (b, 0, 0))
    return pl.pallas_call(
        _suffix_kernel,
        grid=(B,),
        in_specs=[spec],
        out_specs=spec,
        out_shape=jax.ShapeDtypeStruct((B, H, P), F32),
        compiler_params=_params("parallel"),
        name="fox_cache_suffix",
    )(clf_t)


SUB = 16


def _gla_kernel(aq_ref, af_ref, ai_ref, ag_ref, lbl_ref, gn_ref, s0_ref, tri_ref,
                o_ref, s_out_ref, st_ref, *, chunk, n_chunks):
    t = pl.program_id(1)

    @pl.when(t == 0)
    def _():
        for h in range(N_HEADS):
            st_ref[h] = s0_ref[h].T

    lbl = lbl_ref[...]
    e = jnp.exp(lbl - jnp.max(lbl, axis=0, keepdims=True))
    lb = e[0:1] / jnp.sum(e, axis=0, keepdims=True)
    gn = gn_ref[...]
    tri = tri_ref[...]
    n_sub = chunk // SUB

    def one_chunk(ci):
        r = slice(ci * chunk, (ci + 1) * chunk)
        fa = lb + (1.0 - lb) * _sigmoid(af_ref[r, :])
        lf = jnp.log(fa)
        ka = 1.0 - fa
        aq = aq_ref[r, :]
        qa = aq * _sigmoid(aq)
        v = ai_ref[r, :].astype(BF16)
        ag = ag_ref[r, :]
        gate = ag * _sigmoid(ag)

        hi = lf.astype(BF16)
        r1 = lf - hi.astype(F32)
        mid = r1.astype(BF16)
        lo = (r1 - mid.astype(F32)).astype(BF16)
        b = (jnp.dot(tri, hi, preferred_element_type=F32)
             + jnp.dot(tri, mid, preferred_element_type=F32)
             + jnp.dot(tri, lo, preferred_element_type=F32))
        b_end = b[chunk - 1:chunk, :]

        q_in = (qa * jnp.exp(b)).astype(BF16)
        k_out = (ka * jnp.exp(b_end - b)).astype(BF16)
        decay = jnp.exp(b_end)
        qp, kp = [], []
        for i in range(n_sub):
            r0, n = i * SUB, (i + 1) * SUB
            m_i = b[r0 + SUB // 2:r0 + SUB // 2 + 1, :]
            qp.append((qa[r0:n] * jnp.exp(b[r0:n] - m_i)).astype(BF16))
            kp.append((ka[:n] * jnp.exp(m_i - b[:n])).astype(BF16))

        st = [st_ref[h] for h in range(N_HEADS)]
        inter = [lax.dot_general(q_in[:, _head(h)], st[h].astype(BF16), NT_DIMS,
                                 preferred_element_type=F32) for h in range(N_HEADS)]
        att = [[lax.dot_general(qp[i][:, _head(h)], kp[i][:, _head(h)], NT_DIMS,
                                preferred_element_type=F32) for h in range(N_HEADS)]
               for i in range(n_sub)]
        for h in range(N_HEADS):
            st_ref[h] = st[h] * decay[:, _head(h)] + lax.dot_general(
                v[:, _head(h)], k_out[:, _head(h)], TN_DIMS, preferred_element_type=F32)
        for i in range(n_sub):
            r0, n = i * SUB, (i + 1) * SUB
            causal = (lax.broadcasted_iota(jnp.int32, (SUB, n), 1)
                      <= lax.broadcasted_iota(jnp.int32, (SUB, n), 0) + r0)
            att[i] = [jnp.where(causal, a, 0.0).astype(BF16) for a in att[i]]
        for h in range(N_HEADS):
            o = jnp.concatenate(
                [inter[h][i * SUB:(i + 1) * SUB]
                 + jnp.dot(att[i][h], v[:(i + 1) * SUB, _head(h)], preferred_element_type=F32)
                 for i in range(n_sub)], axis=0)
            ms = jnp.mean(o * o, axis=-1, keepdims=True)
            o_ref[r, _head(h)] = (o * lax.rsqrt(ms + EPS) * gn * gate[:, _head(h)]).astype(o_ref.dtype)

    for ci in range(n_chunks):
        one_chunk(ci)

    @pl.when(t == pl.num_programs(1) - 1)
    def _():
        for h in range(N_HEADS):
            s_out_ref[h] = st_ref[h].T


def _hgrn2(a_proj, lb_logits, gnorm, s0, *, batch, length, row0, chunk, n_chunks):
    step = chunk * n_chunks
    nt = length // step
    base = row0 // step

    def col(g):
        return pl.BlockSpec((step, WIDTH), lambda b, t: (base + b * nt + t, g))

    st_spec = pl.BlockSpec((None, N_HEADS, HEAD_DIM, HEAD_DIM), lambda b, t: (b, 0, 0, 0))
    tri = jnp.tril(jnp.ones((chunk, chunk), BF16))
    return pl.pallas_call(
        functools.partial(_gla_kernel, chunk=chunk, n_chunks=n_chunks),
        grid=(batch, nt),
        in_specs=[col(0), col(1), col(2), col(3),
                  pl.BlockSpec(lb_logits.shape, lambda b, t: (0, 0)),
                  pl.BlockSpec((1, HEAD_DIM), lambda b, t: (0, 0)),
                  st_spec,
                  pl.BlockSpec((chunk, chunk), lambda b, t: (0, 0))],
        out_specs=[pl.BlockSpec((step, WIDTH), lambda b, t: (b * nt + t, 0)), st_spec],
        out_shape=[jax.ShapeDtypeStruct((batch * length, WIDTH), BF16),
                   jax.ShapeDtypeStruct((batch, N_HEADS, HEAD_DIM, HEAD_DIM), F32)],
        scratch_shapes=[pltpu.VMEM((N_HEADS, HEAD_DIM, HEAD_DIM), F32)],
        compiler_params=_params("parallel", "arbitrary"),
        name="hgrn2",
    )(a_proj, a_proj, a_proj, a_proj, lb_logits, gnorm, s0, tri)


def _attn_prompt_kernel(q_ref, k_ref, v_ref, c_ref, o_ref,
                        vt_ref, cb_ref, s_ref, m_ref, l_ref, acc_ref, *, blk):
    h = pl.program_id(1)
    length = k_ref.shape[0]
    halves = (slice(0, blk // 2), slice(blk // 2, blk))

    for r0 in range(0, length, blk):
        rs = slice(r0, r0 + blk)
        vt_ref[:, rs] = v_ref[rs, :].astype(F32).T.astype(BF16)
        c_cols = c_ref[:, rs].T
        onehot = lax.broadcasted_iota(jnp.int32, c_cols.shape, 1) == h
        cb_ref[rs, :] = jnp.broadcast_to(
            jnp.sum(jnp.where(onehot, c_cols, 0.0), axis=1, keepdims=True), (blk, LANES))

    def scores(iq, j, buf):
        ks = slice(j * blk, (j + 1) * blk)
        kb = k_ref[ks, :]
        cb = cb_ref[ks, :]
        for hv in halves:
            x = lax.dot_general(kb, q_ref[iq * blk + hv.start:iq * blk + hv.stop, :], NT_DIMS,
                                preferred_element_type=F32)
            s_ref[buf, :, hv] = x - jnp.concatenate([cb] * (x.shape[1] // LANES), axis=1)

    def consume(iq, j, buf):
        st = iq % 2
        masked = j == iq
        vt = vt_ref[:, j * blk:(j + 1) * blk]
        p = []
        for hv in halves:
            keys = hv.stop if masked else blk
            x = s_ref[buf, :keys, hv]
            if masked:
                key = lax.broadcasted_iota(jnp.int32, x.shape, 0)
                qry = lax.broadcasted_iota(jnp.int32, x.shape, 1) + hv.start
                x = jnp.where(key <= qry, x, -jnp.inf)
            if j == 0:
                m_new = jnp.max(x, axis=0, keepdims=True)
                e = jnp.exp2(x - m_new)
                l_ref[st, :, hv] = jnp.sum(e, axis=0, keepdims=True)
            else:
                m_prev = m_ref[st, :, hv]
                m_new = jnp.maximum(m_prev, jnp.max(x, axis=0, keepdims=True))
                alpha = jnp.exp2(m_prev - m_new)
                e = jnp.exp2(x - m_new)
                l_ref[st, :, hv] = alpha * l_ref[st, :, hv] + jnp.sum(e, axis=0, keepdims=True)
                acc_ref[st, :, hv] = alpha * acc_ref[st, :, hv]
            m_ref[st, :, hv] = m_new
            p.append(e.astype(BF16))
        for e, hv in zip(p, halves):
            pv = jnp.dot(vt[:, :e.shape[0]], e, preferred_element_type=F32)
            if j == 0:
                acc_ref[st, :, hv] = pv
            else:
                acc_ref[st, :, hv] += pv

    visits = [(iq, j) for iq in range(length // blk) for j in range(iq + 1)]
    scores(*visits[0], 0)
    for n, (iq, j) in enumerate(visits):
        if n + 1 < len(visits):
            scores(*visits[n + 1], (n + 1) % 2)
        consume(iq, j, n % 2)
        if j == iq:
            st = iq % 2
            o_ref[iq * blk:(iq + 1) * blk, :] = (acc_ref[st] / l_ref[st]).T.astype(o_ref.dtype)


def _fox_prompt(q, k, v, c, *, batch, length, blk):
    spec = pl.BlockSpec((length, HEAD_DIM), lambda b, h: (b, h))
    return pl.pallas_call(
        functools.partial(_attn_prompt_kernel, blk=blk),
        grid=(batch, N_HEADS),
        in_specs=[spec, spec, spec, pl.BlockSpec((None,) + c.shape[1:], lambda b, h: (b, 0, 0))],
        out_specs=spec,
        out_shape=jax.ShapeDtypeStruct((batch * length, WIDTH), BF16),
        scratch_shapes=[pltpu.VMEM((HEAD_DIM, length), BF16),
                        pltpu.VMEM((length, LANES), F32),
                        pltpu.VMEM((2, blk, blk), F32),
                        pltpu.VMEM((2, 1, blk), F32), pltpu.VMEM((2, 1, blk), F32),
                        pltpu.VMEM((2, HEAD_DIM, blk), F32)],
        compiler_params=_params("parallel", "parallel"),
        name="fox_prompt",
    )(q, k, v, c)


def _attend_heads(score_fn, value_fn, m_ref, l_ref, acc_ref):
    s = [score_fn(h) for h in range(N_HEADS)]
    stats = []
    for h in range(N_HEADS):
        m_prev = m_ref[h]
        m_new = jnp.maximum(m_prev, jnp.max(s[h], axis=-1, keepdims=True))
        stats.append((jnp.exp2(m_prev - m_new), m_new))
    p = [jnp.exp2(s[h] - stats[h][1]) for h in range(N_HEADS)]
    for h in range(N_HEADS):
        alpha, m_new = stats[h]
        l_ref[h] = alpha * l_ref[h] + jnp.sum(p[h], axis=-1, keepdims=True)
        m_ref[h] = m_new
        acc_ref[:, _head(h)] = alpha * acc_ref[:, _head(h)] + jnp.dot(
            p[h].astype(BF16), value_fn(h), preferred_element_type=F32)


DMA_HEADS = 6


def _attn_sample_kernel(q_ref, kc_hbm, vc_hbm, kd_ref, vd_ref, bc_ref, kn_ref, vn_ref, bn_ref,
                        o_ref, kbuf_ref, vbuf_ref, sem_ref, m_ref, l_ref, acc_ref):
    b, j = pl.program_id(0), pl.program_id(1)
    nj = pl.num_programs(1)
    tk = bc_ref.shape[1]
    step = b * nj + j
    slot = step % 2

    def cache_copies(stream, blk, into):
        return [pltpu.make_async_copy(hbm.at[stream, pl.ds(blk * tk, tk), h, :],
                                      buf.at[into, h], sem_ref.at[t, into])
                for t, (hbm, buf) in enumerate(((kc_hbm, kbuf_ref), (vc_hbm, vbuf_ref)))
                for h in range(DMA_HEADS)]

    def cached(buf_ref, dense_ref, h):
        rows = buf_ref[slot, h] if h < DMA_HEADS else dense_ref[_head_rows(h, tk), :]
        return rows.astype(BF16)

    def start_all(copies):
        for cp in copies:
            cp.start()

    @pl.when(step == 0)
    def _():
        start_all(cache_copies(b, j, slot))

    @pl.when(step + 1 < pl.num_programs(0) * nj)
    def _():
        start_all(cache_copies((step + 1) // nj, (step + 1) % nj, 1 - slot))

    @pl.when(j == 0)
    def _():
        m_ref[...] = jnp.full(m_ref.shape, -jnp.inf, F32)
        l_ref[...] = jnp.zeros(l_ref.shape, F32)
        acc_ref[...] = jnp.zeros(acc_ref.shape, F32)

    for cp in cache_copies(b, j, slot):
        cp.wait()
    _attend_heads(
        lambda h: lax.dot_general(q_ref[:, _head(h)], cached(kbuf_ref, kd_ref, h),
                                  NT_DIMS, preferred_element_type=F32) + bc_ref[h:h + 1, :],
        lambda h: cached(vbuf_ref, vd_ref, h), m_ref, l_ref, acc_ref)

    @pl.when(j == pl.num_programs(1) - 1)
    def _():
        tq = q_ref.shape[0]
        causal = (lax.broadcasted_iota(jnp.int32, (tq, tq), 1)
                  <= lax.broadcasted_iota(jnp.int32, (tq, tq), 0))
        _attend_heads(
            lambda h: jnp.where(causal,
                                lax.dot_general(q_ref[:, _head(h)], kn_ref[:, _head(h)], NT_DIMS,
                                                preferred_element_type=F32) - bn_ref[h:h + 1, :],
                                -jnp.inf),
            lambda h: vn_ref[:, _head(h)], m_ref, l_ref, acc_ref)
        for h in range(N_HEADS):
            o_ref[:, _head(h)] = (acc_ref[:, _head(h)] / l_ref[h]).astype(o_ref.dtype)


def _fox_sample(q, k_new, v_new, c_new, cache_k, cache_v, cache_bias, *, row0, tk):
    B, H, P = cache_bias.shape
    Ls = c_new.shape[2]
    base = row0 // Ls
    new = pl.BlockSpec((Ls, WIDTH), lambda b, j: (base + b, 0))
    cache = pl.BlockSpec(memory_space=pl.ANY)
    dense = pl.BlockSpec((None, tk * H, HEAD_DIM), lambda b, j: (b, j, 0))
    cache_buf = pltpu.VMEM((2, DMA_HEADS, tk, HEAD_DIM), F32)
    return pl.pallas_call(
        _attn_sample_kernel,
        grid=(B, P // tk),
        in_specs=[new, cache, cache, dense, dense,
                  pl.BlockSpec((None, N_HEADS, tk), lambda b, j: (b, 0, j)),
                  new, new,
                  pl.BlockSpec((None, N_HEADS, Ls), lambda b, j: (b, 0, 0))],
        out_specs=pl.BlockSpec((Ls, WIDTH), lambda b, j: (b, 0)),
        out_shape=jax.ShapeDtypeStruct((B * Ls, WIDTH), BF16),
        scratch_shapes=[cache_buf, cache_buf, pltpu.SemaphoreType.DMA((2, 2)),
                        pltpu.VMEM((N_HEADS, Ls, 1), F32), pltpu.VMEM((N_HEADS, Ls, 1), F32),
                        pltpu.VMEM((Ls, WIDTH), F32)],
        compiler_params=_params("arbitrary", "arbitrary"),
        name="fox_sample",
    )(q, cache_k, cache_v, cache_k.reshape(B, P * H, HEAD_DIM), cache_v.reshape(B, P * H, HEAD_DIM),
      cache_bias, k_new, v_new, c_new)


MERGE_COLS = 512


def _merge_kernel(xp_ref, xs_ref, oap_ref, oas_ref, obp_ref, obs_ref, g_ref,
                  wpa_ref, wpb_ref, wo_ref, g2_ref, o_ref, h2_ref, mg_ref, *, n_first):
    first = pl.program_id(0) < n_first
    D = xp_ref.shape[1]
    oa = jnp.where(first, oap_ref[...], oas_ref[...])
    ob = jnp.where(first, obp_ref[...], obs_ref[...])
    cols = min(MERGE_COLS, D)
    for n0 in range(0, D, cols):
        ns = slice(n0, n0 + cols)
        pa = jnp.dot(oa, wpa_ref[:, ns], preferred_element_type=F32)
        pb = jnp.dot(ob, wpb_ref[:, ns], preferred_element_type=F32)
        ga = g_ref[:, ns].astype(F32)
        gb = g_ref[:, D + n0:D + n0 + cols].astype(F32)
        mg_ref[:, ns] = (ga * pa + gb * pb).astype(BF16)
    x = jnp.where(first, xp_ref[...], xs_ref[...])
    x = x + jnp.dot(mg_ref[...], wo_ref[...], preferred_element_type=F32)
    o_ref[...] = x
    h2_ref[...] = _normed(x, g2_ref)


def _merge_out(xp, xs, oa_p, oa_s, ob_p, ob_s, g, w_pa, w_pb, w_o, g2, *, tm):
    D = xp.shape[1]
    T = xp.shape[0] + xs.shape[0]
    n_first = xp.shape[0] // tm
    fixed = lambda i: (0, 0)
    return pl.pallas_call(
        functools.partial(_merge_kernel, n_first=n_first),
        grid=(T // tm,),
        in_specs=[*_two_streams(tm, D, n_first), *_two_streams(tm, WIDTH, n_first),
                  *_two_streams(tm, WIDTH, n_first),
                  pl.BlockSpec((tm, 2 * D), lambda i: (i, 0)),
                  *[pl.BlockSpec(w.shape, fixed, pipeline_mode=pl.Buffered(1))
                    for w in (w_pa, w_pb, w_o)],
                  pl.BlockSpec((1, D), fixed)],
        out_specs=[pl.BlockSpec((tm, D), lambda i: (i, 0))] * 2,
        out_shape=[jax.ShapeDtypeStruct((T, D), F32), jax.ShapeDtypeStruct((T, D), BF16)],
        scratch_shapes=[pltpu.VMEM((tm, D), BF16)],
        compiler_params=_params("parallel"),
        name="merge_out",
    )(xp, xs, oa_p, oa_s, ob_p, ob_s, g, w_pa, w_pb, w_o, g2)


def _ffn_kernel(x_ref, h_ref, w1_ref, w2_ref, gf_ref, yp_ref, ys_ref, acc_ref, *, n_first):
    i, f = pl.program_id(0), pl.program_id(1)

    @pl.when(f == 0)
    def _():
        acc_ref[...] = x_ref[...]

    u = jnp.maximum(jnp.dot(h_ref[...], w1_ref[...], preferred_element_type=F32), 0.0)
    acc_ref[...] += jnp.dot((u * u).astype(BF16), w2_ref[...], preferred_element_type=F32)

    @pl.when(f == pl.num_programs(1) - 1)
    def _():
        x = acc_ref[...]
        ms = jnp.mean(x * x, axis=-1, keepdims=True)
        y = x * lax.rsqrt(ms + EPS) * gf_ref[...]

        @pl.when(i < n_first)
        def _():
            yp_ref[...] = y

        @pl.when(i >= n_first)
        def _():
            ys_ref[...] = y


def _ffn(x, h, w1, w2, gf, *, rows_first, tm, tf):
    T, D = x.shape
    F = w1.shape[1]
    n_first = rows_first // tm
    rows = pl.BlockSpec((tm, D), lambda i, f: (i, 0))
    return pl.pallas_call(
        functools.partial(_ffn_kernel, n_first=n_first),
        grid=(T // tm, F // tf),
        in_specs=[rows, rows,
                  pl.BlockSpec((D, tf), lambda i, f: (0, f)),
                  pl.BlockSpec((tf, D), lambda i, f: (f, 0)),
                  pl.BlockSpec((1, D), lambda i, f: (0, 0))],
        out_specs=list(_two_streams(tm, D, n_first)),
        out_shape=[jax.ShapeDtypeStruct((rows_first, D), F32),
                   jax.ShapeDtypeStruct((T - rows_first, D), F32)],
        scratch_shapes=[pltpu.VMEM((tm, D), F32)],
        compiler_params=_params("arbitrary", "arbitrary"),
        name="ffn",
    )(x, h, w1, w2, gf)


def _tile(n, pref):
    t = min(pref, n)
    while n % t:
        t //= 2
    return t


def kernel(x_prompt, x_sample, cache_fox_k, cache_fox_v, cache_fox_logf, state_hgrn, norm1, w_in,
           b_fox_f, lb_logits, gnorm_a, w_pa, w_pb, w_o, norm2, w1, w2, norm_f):
    B, L, D = x_prompt.shape
    Bs, Ls, _ = x_sample.shape
    depth, _, P, H, DH = cache_fox_k.shape
    assert depth == 1 and H == N_HEADS and DH == HEAD_DIM
    Tp, Ts = B * L, Bs * Ls
    T = Tp + Ts
    W = WIDTH

    xp, xs = x_prompt.reshape(Tp, D), x_sample.reshape(Ts, D)
    w_in_t = jnp.swapaxes(w_in[0], 0, 1)
    w_fl_t = w_in_t[7 * W:7 * W + H].astype(BF16)
    w_g = w_in_t[7 * W + H:]
    g1 = norm1[0].reshape(1, D)

    tm = _tile(math.gcd(Tp, Ts), 512)
    tm1 = _tile(math.gcd(Tp, Ts), 1024)
    h1 = _rmsnorm_rows(xp, xs, g1, tm=_tile(tm, 256))
    a_proj = _proj(h1, w_in_t, n_cols=4 * W, act=None, out_dtype=F32, tm=tm1, tn=W // 2)
    gates = _proj(h1, w_g, n_cols=2 * D, act="sigmoid", out_dtype=BF16, tm=tm1,
                  tn=_tile(2 * D, W // 2))
    q_b, k_b, v_b, k_p, k_s, v_p, v_s, fl_t = _proj_fox(h1, w_in_t, w_fl_t, col0=4 * W,
                                                        rows_first=Tp, tm=_tile(tm, 512))

    bias = b_fox_f[0].reshape(H, 1)
    lf_p, c_p = _fox_gates(fl_t, bias, col0=0, cols=Tp, seg=L, block=L, pad_rows=LANES)
    lf_s, c_s = _fox_gates(fl_t, bias, col0=Tp, cols=Ts, seg=Ls, block=_tile(Ts, 1024), pad_rows=H)
    cache_bias = _cache_suffix(jnp.transpose(cache_fox_logf[0], (0, 2, 1)))

    zeros_state = jnp.zeros((B, H, DH, DH), F32)
    chunk_p = _tile(L, 64)
    oa_p, s_p = _hgrn2(a_proj, lb_logits, gnorm_a, zeros_state, batch=B, length=L, row0=0,
                       chunk=chunk_p, n_chunks=_tile(L // chunk_p, 4))
    oa_s, s_s = _hgrn2(a_proj, lb_logits, gnorm_a, state_hgrn[0], batch=Bs, length=Ls, row0=Tp,
                       chunk=_tile(Ls, 64), n_chunks=1)

    ob_p = _fox_prompt(q_b, k_b, v_b, c_p, batch=B, length=L, blk=_tile(L, 512))
    c_s3 = c_s.transpose(1, 0, 2).reshape(H, Bs, Ls).transpose(1, 0, 2)
    ob_s = _fox_sample(q_b, k_b, v_b, c_s3, cache_fox_k[0], cache_fox_v[0], cache_bias,
                       row0=Tp, tk=_tile(P, 1024))

    tm5 = _tile(tm, 512)
    x1, h2 = _merge_out(xp, xs, oa_p, oa_s, ob_p, ob_s, gates, w_pa[0].astype(BF16),
                        w_pb[0].astype(BF16), w_o[0].astype(BF16), norm2[0].reshape(1, D),
                        tm=_tile(tm, 256))
    y_p, y_s = _ffn(x1, h2, w1[0].astype(BF16), w2[0].astype(BF16), norm_f.reshape(1, D),
                    rows_first=Tp, tm=tm5, tf=_tile(w1.shape[-1], 1024))

    return (y_p.reshape(B, L, D), y_s.reshape(Bs, Ls, D),
            k_p.reshape(1, B, L, H, DH), v_p.reshape(1, B, L, H, DH),
            lf_p.T.reshape(1, B, L, H), s_p[None],
            k_s.reshape(1, Bs, Ls, H, DH), v_s.reshape(1, Bs, Ls, H, DH),
            lf_s.T.reshape(1, Bs, Ls, H), s_s[None])
```

```python
import functools
import math

import jax
import jax.numpy as jnp
from jax import lax
from jax.experimental import pallas as pl
from jax.experimental.pallas import tpu as pltpu

F32 = jnp.float32
BF16 = jnp.bfloat16

EPS = 1e-6
N_HEADS = 8
HEAD_DIM = 128
WIDTH = N_HEADS * HEAD_DIM
LANES = 128
VMEM_LIMIT = 56 * 1024 * 1024
LOG2E = math.log2(math.e)
QK_SCALE = HEAD_DIM ** -0.5 * LOG2E

NT_DIMS = (((1,), (1,)), ((), ()))
TN_DIMS = (((0,), (0,)), ((), ()))


def _params(*sem):
    return pltpu.CompilerParams(dimension_semantics=sem, vmem_limit_bytes=VMEM_LIMIT)


def _sigmoid(x):
    return 1.0 / (1.0 + jnp.exp(-x))


def _head(h):
    return slice(h * HEAD_DIM, (h + 1) * HEAD_DIM)


def _two_streams(rows, width, n_first, n_second=None):
    mode = {"pipeline_mode": pl.Buffered(1)} if n_second == 1 else {}
    return (pl.BlockSpec((rows, width), lambda i, *_: (jnp.minimum(i, n_first - 1), 0)),
            pl.BlockSpec((rows, width), lambda i, *_: (jnp.maximum(i - n_first, 0), 0), **mode))


def _head_rows(h, rows):
    return pl.ds(h, rows, stride=N_HEADS)


def _normed(x, g_ref):
    ms = jnp.mean(x * x, axis=-1, keepdims=True)
    return (x * lax.rsqrt(ms + EPS) * g_ref[...]).astype(BF16)


def _rmsnorm_kernel(xp_ref, xs_ref, g_ref, o_ref, *, n_first):
    @pl.when(pl.program_id(0) < n_first)
    def _():
        o_ref[...] = _normed(xp_ref[...], g_ref)

    @pl.when(pl.program_id(0) >= n_first)
    def _():
        o_ref[...] = _normed(xs_ref[...], g_ref)


def _rmsnorm_rows(xp, xs, g, *, tm):
    D = xp.shape[1]
    n_first = xp.shape[0] // tm
    T = xp.shape[0] + xs.shape[0]
    return pl.pallas_call(
        functools.partial(_rmsnorm_kernel, n_first=n_first),
        grid=(T // tm,),
        in_specs=[*_two_streams(tm, D, n_first), pl.BlockSpec((1, D), lambda i: (0, 0))],
        out_specs=pl.BlockSpec((tm, D), lambda i: (i, 0)),
        out_shape=jax.ShapeDtypeStruct((T, D), BF16),
        compiler_params=_params("parallel"),
        name="rmsnorm_rows",
    )(xp, xs, g)


def _first_pass_weights(n_blocks, block0, shape, buffers=2):
    return pl.BlockSpec(
        shape, lambda i, j: (block0 + jnp.where(i == 0, j, n_blocks - 1), 0),
        pipeline_mode=pl.Buffered(buffers))


def _proj_kernel(h_ref, w_ref, o_ref, wb_ref, *, act):
    j = pl.program_id(1)

    @pl.when(pl.program_id(0) == 0)
    def _():
        wb_ref[j] = w_ref[...].astype(BF16)

    acc = lax.dot_general(h_ref[...], wb_ref[j], NT_DIMS, preferred_element_type=F32)
    if act == "sigmoid":
        acc = _sigmoid(acc)
    o_ref[...] = acc.astype(o_ref.dtype)


def _proj(h, w, *, n_cols, act, out_dtype, tm, tn):
    T, D = h.shape
    nj = n_cols // tn
    return pl.pallas_call(
        functools.partial(_proj_kernel, act=act),
        grid=(T // tm, nj),
        in_specs=[pl.BlockSpec((tm, D), lambda i, j: (i, 0)),
                  _first_pass_weights(nj, 0, (tn, D))],
        out_specs=pl.BlockSpec((tm, tn), lambda i, j: (i, j)),
        out_shape=jax.ShapeDtypeStruct((T, n_cols), out_dtype),
        scratch_shapes=[pltpu.VMEM((nj, tn, D), BF16)],
        compiler_params=_params("arbitrary", "arbitrary"),
        name="proj_" + (act or "lin"),
    )(h, w)


def _proj_fox_kernel(h_ref, w_ref, wfl_ref,
                     q_ref, kb_ref, vb_ref, fl_ref, kp_hbm, ks_hbm, vp_hbm, vs_hbm,
                     wb_ref, stage_ref, sem_ref, *, n_first):
    i, j = pl.program_id(0), pl.program_id(1)

    @pl.when(i == 0)
    def _():
        wb_ref[j] = w_ref[...].astype(BF16)

    @pl.when(j == 0)
    def _():
        fl_ref[...] = lax.dot_general(wfl_ref[...], h_ref[...], NT_DIMS,
                                      preferred_element_type=F32)

    h = h_ref[...]
    tm = h.shape[0]
    pair = 2 * HEAD_DIM

    def column_pairs():
        for c in range(WIDTH // pair):
            cs = slice(c * pair, (c + 1) * pair)
            yield cs, lax.dot_general(h, wb_ref[j, cs, :], NT_DIMS, preferred_element_type=F32)

    @pl.when(j == 0)
    def _():
        for cs, acc in column_pairs():
            q_ref[:, cs] = (acc * QK_SCALE).astype(BF16)

    def on_head_copies(t, row_tile, action):
        def each(dst, tile):
            for hd in range(N_HEADS):
                action(pltpu.make_async_copy(stage_ref.at[t, :, _head(hd)],
                                             dst.at[pl.ds(tile * tm, tm), hd, :], sem_ref.at[t]))

        first, second = ((kp_hbm, ks_hbm), (vp_hbm, vs_hbm))[t]
        pl.when(row_tile < n_first)(lambda: each(first, row_tile))
        pl.when(row_tile >= n_first)(lambda: each(second, row_tile - n_first))

    for t, bf_ref in enumerate((kb_ref, vb_ref)):
        @pl.when(j == t + 1)
        def _(t=t, bf_ref=bf_ref):
            pl.when(i > 0)(lambda: on_head_copies(t, i - 1, lambda cp: cp.wait()))
            for cs, acc in column_pairs():
                bf_ref[:, cs] = acc.astype(BF16)
                stage_ref[t, :, cs] = acc
            on_head_copies(t, i, lambda cp: cp.start())
            pl.when(i == pl.num_programs(0) - 1)(
                lambda: on_head_copies(t, i, lambda cp: cp.wait()))


def _proj_fox(h, w, w_fl_t, *, col0, rows_first, tm):
    T, D = h.shape
    Tp, Ts = rows_first, T - rows_first
    kv_shape = lambda rows: jax.ShapeDtypeStruct((rows, N_HEADS, HEAD_DIM), F32)
    tok = pl.BlockSpec((tm, WIDTH), lambda i, j: (i, 0))
    tok_shape = jax.ShapeDtypeStruct((T, WIDTH), BF16)
    hbm = pl.BlockSpec(memory_space=pl.ANY)
    return pl.pallas_call(
        functools.partial(_proj_fox_kernel, n_first=Tp // tm),
        grid=(T // tm, 3),
        in_specs=[pl.BlockSpec((tm, D), lambda i, j: (i, 0)),
                  _first_pass_weights(3, col0 // WIDTH, (WIDTH, D), buffers=1),
                  pl.BlockSpec((N_HEADS, D), lambda i, j: (0, 0))],
        out_specs=[tok, tok, tok, pl.BlockSpec((N_HEADS, tm), lambda i, j: (0, i)),
                   hbm, hbm, hbm, hbm],
        out_shape=[tok_shape, tok_shape, tok_shape, jax.ShapeDtypeStruct((N_HEADS, T), F32),
                   kv_shape(Tp), kv_shape(Ts), kv_shape(Tp), kv_shape(Ts)],
        scratch_shapes=[pltpu.VMEM((3, WIDTH, D), BF16), pltpu.VMEM((2, tm, WIDTH), F32),
                        pltpu.SemaphoreType.DMA((2,))],
        compiler_params=_params("arbitrary", "arbitrary"),
        name="proj_fox",
    )(h, w, w_fl_t)


def _lane_cumsum(x, seg):
    pos = lax.broadcasted_iota(jnp.int32, x.shape, x.ndim - 1) & (seg - 1)
    shift = 1
    while shift < seg:
        x = x + jnp.where(pos >= shift, pltpu.roll(x, shift, x.ndim - 1), 0.0)
        shift *= 2
    return x


def _gate_kernel(fl_ref, bias_ref, lf_ref, c_ref, *, seg):
    z = fl_ref[...] + bias_ref[...]
    lf = jnp.minimum(z, 0.0) - jnp.log(1.0 + jnp.exp(-jnp.abs(z)))
    lf_ref[...] = lf
    c = _lane_cumsum(lf, seg) * LOG2E
    c_ref[...] = jnp.zeros(c_ref.shape, F32)
    c_ref[0:c.shape[0], :] = c


def _fox_gates(fl_t, bias, *, col0, cols, seg, block, pad_rows):
    H = fl_t.shape[0]
    nb = cols // block
    base = col0 // block
    return pl.pallas_call(
        functools.partial(_gate_kernel, seg=seg),
        grid=(nb,),
        in_specs=[pl.BlockSpec((H, block), lambda i: (0, base + i)),
                  pl.BlockSpec((H, 1), lambda i: (0, 0))],
        out_specs=[pl.BlockSpec((H, block), lambda i: (0, i)),
                   pl.BlockSpec((None, pad_rows, block), lambda i: (i, 0, 0))],
        out_shape=[jax.ShapeDtypeStruct((H, cols), F32),
                   jax.ShapeDtypeStruct((nb, pad_rows, block), F32)],
        compiler_params=_params("parallel"),
        name="fox_gates",
    )(fl_t, bias)


def _suffix_kernel(lf_ref, o_ref):
    lf = lf_ref[...]
    c = _lane_cumsum(lf, lf.shape[-1])
    o_ref[...] = (c[:, -1:] - c) * LOG2E


def _cache_suffix(clf_t):
    B, H, P = clf_t.shape
    spec = pl.BlockSpec((None, H, P), lambda b: (b, 0, 0))
    return pl.pallas_call(
        _suffix_kernel,
        grid=(B,),
        in_specs=[spec],
        out_specs=spec,
        out_shape=jax.ShapeDtypeStruct((B, H, P), F32),
        compiler_params=_params("parallel"),
        name="fox_cache_suffix",
    )(clf_t)


SUB = 16


def _gla_kernel(aq_ref, af_ref, ai_ref, ag_ref, lbl_ref, gn_ref, s0_ref, tri_ref,
                o_ref, s_out_ref, st_ref, *, chunk, n_chunks):
    t = pl.program_id(1)

    @pl.when(t == 0)
    def _():
        for h in range(N_HEADS):
            st_ref[h] = s0_ref[h].T

    lbl = lbl_ref[...]
    e = jnp.exp(lbl - jnp.max(lbl, axis=0, keepdims=True))
    lb = e[0:1] / jnp.sum(e, axis=0, keepdims=True)
    gn = gn_ref[...]
    tri = tri_ref[...]
    n_sub = chunk // SUB

    def one_chunk(ci):
        r = slice(ci * chunk, (ci + 1) * chunk)
        fa = lb + (1.0 - lb) * _sigmoid(af_ref[r, :])
        lf = jnp.log(fa)
        ka = 1.0 - fa
        aq = aq_ref[r, :]
        qa = aq * _sigmoid(aq)
        v = ai_ref[r, :].astype(BF16)
        ag = ag_ref[r, :]
        gate = ag * _sigmoid(ag)

        hi = lf.astype(BF16)
        r1 = lf - hi.astype(F32)
        mid = r1.astype(BF16)
        lo = (r1 - mid.astype(F32)).astype(BF16)
        b = (jnp.dot(tri, hi, preferred_element_type=F32)
             + jnp.dot(tri, mid, preferred_element_type=F32)
             + jnp.dot(tri, lo, preferred_element_type=F32))
        b_end = b[chunk - 1:chunk, :]

        q_in = (qa * jnp.exp(b)).astype(BF16)
        k_out = (ka * jnp.exp(b_end - b)).astype(BF16)
        decay = jnp.exp(b_end)
        qp, kp = [], []
        for i in range(n_sub):
            r0, n = i * SUB, (i + 1) * SUB
            m_i = b[r0 + SUB // 2:r0 + SUB // 2 + 1, :]
            qp.append((qa[r0:n] * jnp.exp(b[r0:n] - m_i)).astype(BF16))
            kp.append((ka[:n] * jnp.exp(m_i - b[:n])).astype(BF16))

        st = [st_ref[h] for h in range(N_HEADS)]
        inter = [lax.dot_general(q_in[:, _head(h)], st[h].astype(BF16), NT_DIMS,
                                 preferred_element_type=F32) for h in range(N_HEADS)]
        att = [[lax.dot_general(qp[i][:, _head(h)], kp[i][:, _head(h)], NT_DIMS,
                                preferred_element_type=F32) for h in range(N_HEADS)]
               for i in range(n_sub)]
        for h in range(N_HEADS):
            st_ref[h] = st[h] * decay[:, _head(h)] + lax.dot_general(
                v[:, _head(h)], k_out[:, _head(h)], TN_DIMS, preferred_element_type=F32)
        for i in range(n_sub):
            r0, n = i * SUB, (i + 1) * SUB
            causal = (lax.broadcasted_iota(jnp.int32, (SUB, n), 1)
                      <= lax.broadcasted_iota(jnp.int32, (SUB, n), 0) + r0)
            att[i] = [jnp.where(causal, a, 0.0).astype(BF16) for a in att[i]]
        for h in range(N_HEADS):
            o = jnp.concatenate(
                [inter[h][i * SUB:(i + 1) * SUB]
                 + jnp.dot(att[i][h], v[:(i + 1) * SUB, _head(h)], preferred_element_type=F32)
                 for i in range(n_sub)], axis=0)
            ms = jnp.mean(o * o, axis=-1, keepdims=True)
            o_ref[r, _head(h)] = (o * lax.rsqrt(ms + EPS) * gn * gate[:, _head(h)]).astype(o_ref.dtype)

    for ci in range(n_chunks):
        one_chunk(ci)

    @pl.when(t == pl.num_programs(1) - 1)
    def _():
        for h in range(N_HEADS):
            s_out_ref[h] = st_ref[h].T


def _hgrn2(a_proj, lb_logits, gnorm, s0, *, batch, length, row0, chunk, n_chunks):
    step = chunk * n_chunks
    nt = length // step
    base = row0 // step

    def col(g):
        return pl.BlockSpec((step, WIDTH), lambda b, t: (base + b * nt + t, g))

    st_spec = pl.BlockSpec((None, N_HEADS, HEAD_DIM, HEAD_DIM), lambda b, t: (b, 0, 0, 0))
    tri = jnp.tril(jnp.ones((chunk, chunk), BF16))
    return pl.pallas_call(
        functools.partial(_gla_kernel, chunk=chunk, n_chunks=n_chunks),
        grid=(batch, nt),
        in_specs=[col(0), col(1), col(2), col(3),
                  pl.BlockSpec(lb_logits.shape, lambda b, t: (0, 0)),
                  pl.BlockSpec((1, HEAD_DIM), lambda b, t: (0, 0)),
                  st_spec,
                  pl.BlockSpec((chunk, chunk), lambda b, t: (0, 0))],
        out_specs=[pl.BlockSpec((step, WIDTH), lambda b, t: (b * nt + t, 0)), st_spec],
        out_shape=[jax.ShapeDtypeStruct((batch * length, WIDTH), BF16),
                   jax.ShapeDtypeStruct((batch, N_HEADS, HEAD_DIM, HEAD_DIM), F32)],
        scratch_shapes=[pltpu.VMEM((N_HEADS, HEAD_DIM, HEAD_DIM), F32)],
        compiler_params=_params("parallel", "arbitrary"),
        name="hgrn2",
    )(a_proj, a_proj, a_proj, a_proj, lb_logits, gnorm, s0, tri)


def _attn_prompt_kernel(q_ref, k_ref, v_ref, c_ref, o_ref,
                        vt_ref, cb_ref, s_ref, m_ref, l_ref, acc_ref, *, blk):
    h = pl.program_id(1)
    length = k_ref.shape[0]
    halves = (slice(0, blk // 2), slice(blk // 2, blk))

    for r0 in range(0, length, blk):
        rs = slice(r0, r0 + blk)
        vt_ref[:, rs] = v_ref[rs, :].astype(F32).T.astype(BF16)
        c_cols = c_ref[:, rs].T
        onehot = lax.broadcasted_iota(jnp.int32, c_cols.shape, 1) == h
        cb_ref[rs, :] = jnp.broadcast_to(
            jnp.sum(jnp.where(onehot, c_cols, 0.0), axis=1, keepdims=True), (blk, LANES))

    def scores(iq, j, buf):
        ks = slice(j * blk, (j + 1) * blk)
        kb = k_ref[ks, :]
        cb = cb_ref[ks, :]
        for hv in halves:
            x = lax.dot_general(kb, q_ref[iq * blk + hv.start:iq * blk + hv.stop, :], NT_DIMS,
                                preferred_element_type=F32)
            s_ref[buf, :, hv] = x - jnp.concatenate([cb] * (x.shape[1] // LANES), axis=1)

    def consume(iq, j, buf):
        st = iq % 2
        masked = j == iq
        vt = vt_ref[:, j * blk:(j + 1) * blk]
        p = []
        for hv in halves:
            keys = hv.stop if masked else blk
            x = s_ref[buf, :keys, hv]
            if masked:
                key = lax.broadcasted_iota(jnp.int32, x.shape, 0)
                qry = lax.broadcasted_iota(jnp.int32, x.shape, 1) + hv.start
                x = jnp.where(key <= qry, x, -jnp.inf)
            if j == 0:
                m_new = jnp.max(x, axis=0, keepdims=True)
                e = jnp.exp2(x - m_new)
                l_ref[st, :, hv] = jnp.sum(e, axis=0, keepdims=True)
            else:
                m_prev = m_ref[st, :, hv]
                m_new = jnp.maximum(m_prev, jnp.max(x, axis=0, keepdims=True))
                alpha = jnp.exp2(m_prev - m_new)
                e = jnp.exp2(x - m_new)
                l_ref[st, :, hv] = alpha * l_ref[st, :, hv] + jnp.sum(e, axis=0, keepdims=True)
                acc_ref[st, :, hv] = alpha * acc_ref[st, :, hv]
            m_ref[st, :, hv] = m_new
            p.append(e.astype(BF16))
        for e, hv in zip(p, halves):
            pv = jnp.dot(vt[:, :e.shape[0]], e, preferred_element_type=F32)
            if j == 0:
                acc_ref[st, :, hv] = pv
            else:
                acc_ref[st, :, hv] += pv

    visits = [(iq, j) for iq in range(length // blk) for j in range(iq + 1)]
    scores(*visits[0], 0)
    for n, (iq, j) in enumerate(visits):
        if n + 1 < len(visits):
            scores(*visits[n + 1], (n + 1) % 2)
        consume(iq, j, n % 2)
        if j == iq:
            st = iq % 2
            o_ref[iq * blk:(iq + 1) * blk, :] = (acc_ref[st] / l_ref[st]).T.astype(o_ref.dtype)


def _fox_prompt(q, k, v, c, *, batch, length, blk):
    spec = pl.BlockSpec((length, HEAD_DIM), lambda b, h: (b, h))
    return pl.pallas_call(
        functools.partial(_attn_prompt_kernel, blk=blk),
        grid=(batch, N_HEADS),
        in_specs=[spec, spec, spec, pl.BlockSpec((None,) + c.shape[1:], lambda b, h: (b, 0, 0))],
        out_specs=spec,
        out_shape=jax.ShapeDtypeStruct((batch * length, WIDTH), BF16),
        scratch_shapes=[pltpu.VMEM((HEAD_DIM, length), BF16),
                        pltpu.VMEM((length, LANES), F32),
                        pltpu.VMEM((2, blk, blk), F32),
                        pltpu.VMEM((2, 1, blk), F32), pltpu.VMEM((2, 1, blk), F32),
                        pltpu.VMEM((2, HEAD_DIM, blk), F32)],
        compiler_params=_params("parallel", "parallel"),
        name="fox_prompt",
    )(q, k, v, c)


def _attend_heads(score_fn, value_fn, m_ref, l_ref, acc_ref):
    s = [score_fn(h) for h in range(N_HEADS)]
    stats = []
    for h in range(N_HEADS):
        m_prev = m_ref[h]
        m_new = jnp.maximum(m_prev, jnp.max(s[h], axis=-1, keepdims=True))
        stats.append((jnp.exp2(m_prev - m_new), m_new))
    p = [jnp.exp2(s[h] - stats[h][1]) for h in range(N_HEADS)]
    for h in range(N_HEADS):
        alpha, m_new = stats[h]
        l_ref[h] = alpha * l_ref[h] + jnp.sum(p[h], axis=-1, keepdims=True)
        m_ref[h] = m_new
        acc_ref[:, _head(h)] = alpha * acc_ref[:, _head(h)] + jnp.dot(
            p[h].astype(BF16), value_fn(h), preferred_element_type=F32)


def _attn_sample_kernel(q_ref, kc_hbm, vc_hbm, bc_ref, kn_ref, vn_ref, bn_ref, o_ref,
                        kbuf_ref, vbuf_ref, sem_ref, m_ref, l_ref, acc_ref):
    b, j = pl.program_id(0), pl.program_id(1)
    nj = pl.num_programs(1)
    tk = bc_ref.shape[1]
    step = b * nj + j
    slot = step % 2

    def cache_copies(stream, blk, into):
        return [pltpu.make_async_copy(hbm.at[stream, pl.ds(blk * tk, tk), h, :],
                                      buf.at[into, h], sem_ref.at[t, into])
                for t, (hbm, buf) in enumerate(((kc_hbm, kbuf_ref), (vc_hbm, vbuf_ref)))
                for h in range(N_HEADS)]

    def start_all(copies):
        for cp in copies:
            cp.start()

    @pl.when(step == 0)
    def _():
        start_all(cache_copies(b, j, slot))

    @pl.when(step + 1 < pl.num_programs(0) * nj)
    def _():
        start_all(cache_copies((step + 1) // nj, (step + 1) % nj, 1 - slot))

    @pl.when(j == 0)
    def _():
        m_ref[...] = jnp.full(m_ref.shape, -jnp.inf, F32)
        l_ref[...] = jnp.zeros(l_ref.shape, F32)
        acc_ref[...] = jnp.zeros(acc_ref.shape, F32)

    for cp in cache_copies(b, j, slot):
        cp.wait()
    _attend_heads(
        lambda h: lax.dot_general(q_ref[:, _head(h)], kbuf_ref[slot, h].astype(BF16),
                                  NT_DIMS, preferred_element_type=F32) + bc_ref[h:h + 1, :],
        lambda h: vbuf_ref[slot, h].astype(BF16), m_ref, l_ref, acc_ref)

    @pl.when(j == pl.num_programs(1) - 1)
    def _():
        tq = q_ref.shape[0]
        causal = (lax.broadcasted_iota(jnp.int32, (tq, tq), 1)
                  <= lax.broadcasted_iota(jnp.int32, (tq, tq), 0))
        _attend_heads(
            lambda h: jnp.where(causal,
                                lax.dot_general(q_ref[:, _head(h)], kn_ref[:, _head(h)], NT_DIMS,
                                                preferred_element_type=F32) - bn_ref[h:h + 1, :],
                                -jnp.inf),
            lambda h: vn_ref[:, _head(h)], m_ref, l_ref, acc_ref)
        for h in range(N_HEADS):
            o_ref[:, _head(h)] = (acc_ref[:, _head(h)] / l_ref[h]).astype(o_ref.dtype)


def _fox_sample(q, k_new, v_new, c_new, cache_k, cache_v, cache_bias, *, row0, tk):
    B, H, P = cache_bias.shape
    Ls = c_new.shape[2]
    base = row0 // Ls
    new = pl.BlockSpec((Ls, WIDTH), lambda b, j: (base + b, 0))
    cache = pl.BlockSpec(memory_space=pl.ANY)
    cache_buf = pltpu.VMEM((2, H, tk, HEAD_DIM), F32)
    return pl.pallas_call(
        _attn_sample_kernel,
        grid=(B, P // tk),
        in_specs=[new, cache, cache,
                  pl.BlockSpec((None, N_HEADS, tk), lambda b, j: (b, 0, j)),
                  new, new,
                  pl.BlockSpec((None, N_HEADS, Ls), lambda b, j: (b, 0, 0))],
        out_specs=pl.BlockSpec((Ls, WIDTH), lambda b, j: (b, 0)),
        out_shape=jax.ShapeDtypeStruct((B * Ls, WIDTH), BF16),
        scratch_shapes=[cache_buf, cache_buf, pltpu.SemaphoreType.DMA((2, 2)),
                        pltpu.VMEM((N_HEADS, Ls, 1), F32), pltpu.VMEM((N_HEADS, Ls, 1), F32),
                        pltpu.VMEM((Ls, WIDTH), F32)],
        compiler_params=_params("arbitrary", "arbitrary"),
        name="fox_sample",
    )(q, cache_k, cache_v, cache_bias, k_new, v_new, c_new)


MERGE_COLS = 512


def _merge_kernel(xp_ref, xs_ref, oap_ref, oas_ref, obp_ref, obs_ref, g_ref,
                  wpa_ref, wpb_ref, wo_ref, g2_ref, o_ref, h2_ref, mg_ref, *, n_first):
    first = pl.program_id(0) < n_first
    D = xp_ref.shape[1]
    oa = jnp.where(first, oap_ref[...], oas_ref[...])
    ob = jnp.where(first, obp_ref[...], obs_ref[...])
    cols = min(MERGE_COLS, D)
    for n0 in range(0, D, cols):
        ns = slice(n0, n0 + cols)
        pa = jnp.dot(oa, wpa_ref[:, ns], preferred_element_type=F32)
        pb = jnp.dot(ob, wpb_ref[:, ns], preferred_element_type=F32)
        ga = g_ref[:, ns].astype(F32)
        gb = g_ref[:, D + n0:D + n0 + cols].astype(F32)
        mg_ref[:, ns] = (ga * pa + gb * pb).astype(BF16)
    x = jnp.where(first, xp_ref[...], xs_ref[...])
    x = x + jnp.dot(mg_ref[...], wo_ref[...], preferred_element_type=F32)
    o_ref[...] = x
    h2_ref[...] = _normed(x, g2_ref)


def _merge_out(xp, xs, oa_p, oa_s, ob_p, ob_s, g, w_pa, w_pb, w_o, g2, *, tm):
    D = xp.shape[1]
    T = xp.shape[0] + xs.shape[0]
    n_first = xp.shape[0] // tm
    fixed = lambda i: (0, 0)
    return pl.pallas_call(
        functools.partial(_merge_kernel, n_first=n_first),
        grid=(T // tm,),
        in_specs=[*_two_streams(tm, D, n_first), *_two_streams(tm, WIDTH, n_first),
                  *_two_streams(tm, WIDTH, n_first),
                  pl.BlockSpec((tm, 2 * D), lambda i: (i, 0)),
                  *[pl.BlockSpec(w.shape, fixed, pipeline_mode=pl.Buffered(1))
                    for w in (w_pa, w_pb, w_o)],
                  pl.BlockSpec((1, D), fixed)],
        out_specs=[pl.BlockSpec((tm, D), lambda i: (i, 0))] * 2,
        out_shape=[jax.ShapeDtypeStruct((T, D), F32), jax.ShapeDtypeStruct((T, D), BF16)],
        scratch_shapes=[pltpu.VMEM((tm, D), BF16)],
        compiler_params=_params("parallel"),
        name="merge_out",
    )(xp, xs, oa_p, oa_s, ob_p, ob_s, g, w_pa, w_pb, w_o, g2)


def _ffn_kernel(x_ref, h_ref, w1_ref, w2_ref, gf_ref, yp_ref, ys_ref, acc_ref, *, n_first):
    i, f = pl.program_id(0), pl.program_id(1)

    @pl.when(f == 0)
    def _():
        acc_ref[...] = x_ref[...]

    u = jnp.maximum(jnp.dot(h_ref[...], w1_ref[...], preferred_element_type=F32), 0.0)
    acc_ref[...] += jnp.dot((u * u).astype(BF16), w2_ref[...], preferred_element_type=F32)

    @pl.when(f == pl.num_programs(1) - 1)
    def _():
        x = acc_ref[...]
        ms = jnp.mean(x * x, axis=-1, keepdims=True)
        y = x * lax.rsqrt(ms + EPS) * gf_ref[...]

        @pl.when(i < n_first)
        def _():
            yp_ref[...] = y

        @pl.when(i >= n_first)
        def _():
            ys_ref[...] = y


def _ffn(x, h, w1, w2, gf, *, rows_first, tm, tf):
    T, D = x.shape
    F = w1.shape[1]
    n_first = rows_first // tm
    rows = pl.BlockSpec((tm, D), lambda i, f: (i, 0))
    return pl.pallas_call(
        functools.partial(_ffn_kernel, n_first=n_first),
        grid=(T // tm, F // tf),
        in_specs=[rows, rows,
                  pl.BlockSpec((D, tf), lambda i, f: (0, f)),
                  pl.BlockSpec((tf, D), lambda i, f: (f, 0)),
                  pl.BlockSpec((1, D), lambda i, f: (0, 0))],
        out_specs=list(_two_streams(tm, D, n_first)),
        out_shape=[jax.ShapeDtypeStruct((rows_first, D), F32),
                   jax.ShapeDtypeStruct((T - rows_first, D), F32)],
        scratch_shapes=[pltpu.VMEM((tm, D), F32)],
        compiler_params=_params("arbitrary", "arbitrary"),
        name="ffn",
    )(x, h, w1, w2, gf)


def _tile(n, pref):
    t = min(pref, n)
    while n % t:
        t //= 2
    return t


def kernel(x_prompt, x_sample, cache_fox_k, cache_fox_v, cache_fox_logf, state_hgrn, norm1, w_in,
           b_fox_f, lb_logits, gnorm_a, w_pa, w_pb, w_o, norm2, w1, w2, norm_f):
    B, L, D = x_prompt.shape
    Bs, Ls, _ = x_sample.shape
    depth, _, P, H, DH = cache_fox_k.shape
    assert depth == 1 and H == N_HEADS and DH == HEAD_DIM
    Tp, Ts = B * L, Bs * Ls
    T = Tp + Ts
    W = WIDTH

    xp, xs = x_prompt.reshape(Tp, D), x_sample.reshape(Ts, D)
    w_in_t = jnp.swapaxes(w_in[0], 0, 1)
    w_fl_t = w_in_t[7 * W:7 * W + H].astype(BF16)
    w_g = w_in_t[7 * W + H:]
    g1 = norm1[0].reshape(1, D)

    tm = _tile(math.gcd(Tp, Ts), 512)
    tm1 = _tile(math.gcd(Tp, Ts), 1024)
    h1 = _rmsnorm_rows(xp, xs, g1, tm=_tile(tm, 256))
    a_proj = _proj(h1, w_in_t, n_cols=4 * W, act=None, out_dtype=F32, tm=tm1, tn=W // 2)
    gates = _proj(h1, w_g, n_cols=2 * D, act="sigmoid", out_dtype=BF16, tm=tm1,
                  tn=_tile(2 * D, W // 2))
    q_b, k_b, v_b, fl_t, k_p, k_s, v_p, v_s = _proj_fox(h1, w_in_t, w_fl_t, col0=4 * W,
                                                        rows_first=Tp, tm=_tile(tm, 512))

    bias = b_fox_f[0].reshape(H, 1)
    lf_p, c_p = _fox_gates(fl_t, bias, col0=0, cols=Tp, seg=L, block=L, pad_rows=LANES)
    lf_s, c_s = _fox_gates(fl_t, bias, col0=Tp, cols=Ts, seg=Ls, block=_tile(Ts, 1024), pad_rows=H)
    cache_bias = _cache_suffix(jnp.transpose(cache_fox_logf[0], (0, 2, 1)))

    zeros_state = jnp.zeros((B, H, DH, DH), F32)
    chunk_p = _tile(L, 64)
    oa_p, s_p = _hgrn2(a_proj, lb_logits, gnorm_a, zeros_state, batch=B, length=L, row0=0,
                       chunk=chunk_p, n_chunks=_tile(L // chunk_p, 4))
    oa_s, s_s = _hgrn2(a_proj, lb_logits, gnorm_a, state_hgrn[0], batch=Bs, length=Ls, row0=Tp,
                       chunk=_tile(Ls, 64), n_chunks=1)

    ob_p = _fox_prompt(q_b, k_b, v_b, c_p, batch=B, length=L, blk=_tile(L, 512))
    c_s3 = c_s.transpose(1, 0, 2).reshape(H, Bs, Ls).transpose(1, 0, 2)
    ob_s = _fox_sample(q_b, k_b, v_b, c_s3, cache_fox_k[0], cache_fox_v[0], cache_bias,
                       row0=Tp, tk=_tile(P, 1024))

    tm5 = _tile(tm, 512)
    x1, h2 = _merge_out(xp, xs, oa_p, oa_s, ob_p, ob_s, gates, w_pa[0].astype(BF16),
                        w_pb[0].astype(BF16), w_o[0].astype(BF16), norm2[0].reshape(1, D),
                        tm=_tile(tm, 256))
    y_p, y_s = _ffn(x1, h2, w1[0].astype(BF16), w2[0].astype(BF16), norm_f.reshape(1, D),
                    rows_first=Tp, tm=tm5, tf=_tile(w1.shape[-1], 1024))

    return (y_p.reshape(B, L, D), y_s.reshape(Bs, Ls, D),
            k_p.reshape(1, B, L, H, DH), v_p.reshape(1, B, L, H, DH),
            lf_p.T.reshape(1, B, L, H), s_p[None],
            k_s.reshape(1, Bs, Ls, H, DH), v_s.reshape(1, Bs, Ls, H, DH),
            lf_s.T.reshape(1, Bs, Ls, H), s_s[None])
```

```python
import functools
import math

import jax
import jax.numpy as jnp
from jax import lax
from jax.experimental import pallas as pl
from jax.experimental.pallas import tpu as pltpu

F32 = jnp.float32
BF16 = jnp.bfloat16

EPS = 1e-6
N_HEADS = 8
HEAD_DIM = 128
WIDTH = N_HEADS * HEAD_DIM
LANES = 128
VMEM_LIMIT = 56 * 1024 * 1024
LOG2E = math.log2(math.e)
QK_SCALE = HEAD_DIM ** -0.5 * LOG2E

NT_DIMS = (((1,), (1,)), ((), ()))
TN_DIMS = (((0,), (0,)), ((), ()))


def _params(*sem):
    return pltpu.CompilerParams(dimension_semantics=sem, vmem_limit_bytes=VMEM_LIMIT)


def _sigmoid(x):
    return 1.0 / (1.0 + jnp.exp(-x))


def _head(h):
    return slice(h * HEAD_DIM, (h + 1) * HEAD_DIM)


def _two_streams(rows, width, n_first, n_second=None):
    mode = {"pipeline_mode": pl.Buffered(1)} if n_second == 1 else {}
    return (pl.BlockSpec((rows, width), lambda i, *_: (jnp.minimum(i, n_first - 1), 0)),
            pl.BlockSpec((rows, width), lambda i, *_: (jnp.maximum(i - n_first, 0), 0), **mode))


def _head_rows(h, rows):
    return pl.ds(h, rows, stride=N_HEADS)


def _normed(x, g_ref):
    ms = jnp.mean(x * x, axis=-1, keepdims=True)
    return (x * lax.rsqrt(ms + EPS) * g_ref[...]).astype(BF16)


def _rmsnorm_kernel(xp_ref, xs_ref, g_ref, o_ref, *, n_first):
    @pl.when(pl.program_id(0) < n_first)
    def _():
        o_ref[...] = _normed(xp_ref[...], g_ref)

    @pl.when(pl.program_id(0) >= n_first)
    def _():
        o_ref[...] = _normed(xs_ref[...], g_ref)


def _rmsnorm_rows(xp, xs, g, *, tm):
    D = xp.shape[1]
    n_first = xp.shape[0] // tm
    T = xp.shape[0] + xs.shape[0]
    return pl.pallas_call(
        functools.partial(_rmsnorm_kernel, n_first=n_first),
        grid=(T // tm,),
        in_specs=[*_two_streams(tm, D, n_first), pl.BlockSpec((1, D), lambda i: (0, 0))],
        out_specs=pl.BlockSpec((tm, D), lambda i: (i, 0)),
        out_shape=jax.ShapeDtypeStruct((T, D), BF16),
        compiler_params=_params("parallel"),
        name="rmsnorm_rows",
    )(xp, xs, g)


def _first_pass_weights(n_blocks, block0, shape, buffers=2):
    return pl.BlockSpec(
        shape, lambda i, j: (block0 + jnp.where(i == 0, j, n_blocks - 1), 0),
        pipeline_mode=pl.Buffered(buffers))


def _proj_kernel(h_ref, w_ref, o_ref, wb_ref, *, act):
    j = pl.program_id(1)

    @pl.when(pl.program_id(0) == 0)
    def _():
        wb_ref[j] = w_ref[...].astype(BF16)

    acc = lax.dot_general(h_ref[...], wb_ref[j], NT_DIMS, preferred_element_type=F32)
    if act == "sigmoid":
        acc = _sigmoid(acc)
    o_ref[...] = acc.astype(o_ref.dtype)


def _proj(h, w, *, n_cols, act, out_dtype, tm, tn):
    T, D = h.shape
    nj = n_cols // tn
    return pl.pallas_call(
        functools.partial(_proj_kernel, act=act),
        grid=(T // tm, nj),
        in_specs=[pl.BlockSpec((tm, D), lambda i, j: (i, 0)),
                  _first_pass_weights(nj, 0, (tn, D), buffers=1)],
        out_specs=pl.BlockSpec((tm, tn), lambda i, j: (i, j)),
        out_shape=jax.ShapeDtypeStruct((T, n_cols), out_dtype),
        scratch_shapes=[pltpu.VMEM((nj, tn, D), BF16)],
        compiler_params=_params("arbitrary", "arbitrary"),
        name="proj_" + (act or "lin"),
    )(h, w)


def _proj_fox_kernel(h_ref, w_ref, wfl_ref,
                     q_ref, kb_ref, vb_ref, kp_ref, ks_ref, vp_ref, vs_ref, fl_ref, wb_ref,
                     *, n_first):
    i, j = pl.program_id(0), pl.program_id(1)

    @pl.when(i == 0)
    def _():
        wb_ref[j] = w_ref[...].astype(BF16)

    @pl.when(j == 0)
    def _():
        fl_ref[...] = lax.dot_general(wfl_ref[...], h_ref[...], NT_DIMS,
                                      preferred_element_type=F32)

    h = h_ref[...]
    tm = h.shape[0]
    pair = 2 * HEAD_DIM

    def column_pairs():
        for c in range(WIDTH // pair):
            cs = slice(c * pair, (c + 1) * pair)
            yield c, cs, lax.dot_general(h, wb_ref[j, cs, :], NT_DIMS,
                                         preferred_element_type=F32)

    @pl.when(j == 0)
    def _():
        for _, cs, acc in column_pairs():
            q_ref[:, cs] = (acc * QK_SCALE).astype(BF16)

    def key_or_value(bf_ref, heads_ref):
        for c, cs, acc in column_pairs():
            bf_ref[:, cs] = acc.astype(BF16)
            for hh in range(2):
                heads_ref[_head_rows(2 * c + hh, tm), :] = acc[:, _head(hh)]

    for jj, bf_ref, first_ref, second_ref in ((1, kb_ref, kp_ref, ks_ref),
                                              (2, vb_ref, vp_ref, vs_ref)):
        pl.when((j == jj) & (i < n_first))(
            functools.partial(key_or_value, bf_ref, first_ref))
        pl.when((j == jj) & (i >= n_first))(
            functools.partial(key_or_value, bf_ref, second_ref))


def _proj_fox(h, w, w_fl_t, *, col0, rows_first, tm):
    T, D = h.shape
    Tp, Ts = rows_first, T - rows_first
    n_first = Tp // tm
    kv_p, kv_s = _two_streams(tm * N_HEADS, HEAD_DIM, n_first)
    kv_shape = lambda rows: jax.ShapeDtypeStruct((rows * N_HEADS, HEAD_DIM), F32)
    tok = pl.BlockSpec((tm, WIDTH), lambda i, j: (i, 0))
    tok_shape = jax.ShapeDtypeStruct((T, WIDTH), BF16)
    return pl.pallas_call(
        functools.partial(_proj_fox_kernel, n_first=n_first),
        grid=(T // tm, 3),
        in_specs=[pl.BlockSpec((tm, D), lambda i, j: (i, 0)),
                  _first_pass_weights(3, col0 // WIDTH, (WIDTH, D), buffers=1),
                  pl.BlockSpec((N_HEADS, D), lambda i, j: (0, 0))],
        out_specs=[tok, tok, tok, kv_p, kv_s, kv_p, kv_s,
                   pl.BlockSpec((N_HEADS, tm), lambda i, j: (0, i))],
        out_shape=[tok_shape, tok_shape, tok_shape,
                   kv_shape(Tp), kv_shape(Ts), kv_shape(Tp), kv_shape(Ts),
                   jax.ShapeDtypeStruct((N_HEADS, T), F32)],
        scratch_shapes=[pltpu.VMEM((3, WIDTH, D), BF16)],
        compiler_params=_params("arbitrary", "arbitrary"),
        name="proj_fox",
    )(h, w, w_fl_t)


def _lane_cumsum(x, seg):
    pos = lax.broadcasted_iota(jnp.int32, x.shape, x.ndim - 1) & (seg - 1)
    shift = 1
    while shift < seg:
        x = x + jnp.where(pos >= shift, pltpu.roll(x, shift, x.ndim - 1), 0.0)
        shift *= 2
    return x


def _gate_kernel(fl_ref, bias_ref, lf_ref, c_ref, *, seg):
    z = fl_ref[...] + bias_ref[...]
    lf = jnp.minimum(z, 0.0) - jnp.log(1.0 + jnp.exp(-jnp.abs(z)))
    lf_ref[...] = lf
    c = _lane_cumsum(lf, seg) * LOG2E
    c_ref[...] = jnp.zeros(c_ref.shape, F32)
    c_ref[0:c.shape[0], :] = c


def _fox_gates(fl_t, bias, *, col0, cols, seg, block, pad_rows):
    H = fl_t.shape[0]
    nb = cols // block
    base = col0 // block
    return pl.pallas_call(
        functools.partial(_gate_kernel, seg=seg),
        grid=(nb,),
        in_specs=[pl.BlockSpec((H, block), lambda i: (0, base + i)),
                  pl.BlockSpec((H, 1), lambda i: (0, 0))],
        out_specs=[pl.BlockSpec((H, block), lambda i: (0, i)),
                   pl.BlockSpec((None, pad_rows, block), lambda i: (i, 0, 0))],
        out_shape=[jax.ShapeDtypeStruct((H, cols), F32),
                   jax.ShapeDtypeStruct((nb, pad_rows, block), F32)],
        compiler_params=_params("parallel"),
        name="fox_gates",
    )(fl_t, bias)


def _suffix_kernel(lf_ref, o_ref):
    lf = lf_ref[...]
    c = _lane_cumsum(lf, lf.shape[-1])
    o_ref[...] = (c[:, -1:] - c) * LOG2E


def _cache_suffix(clf_t):
    B, H, P = clf_t.shape
    spec = pl.BlockSpec((None, H, P), lambda b: (b, 0, 0))
    return pl.pallas_call(
        _suffix_kernel,
        grid=(B,),
        in_specs=[spec],
        out_specs=spec,
        out_shape=jax.ShapeDtypeStruct((B, H, P), F32),
        compiler_params=_params("parallel"),
        name="fox_cache_suffix",
    )(clf_t)


SUB = 16


def _gla_kernel(aq_ref, af_ref, ai_ref, ag_ref, lbl_ref, gn_ref, s0_ref, tri_ref,
                o_ref, s_out_ref, st_ref, *, chunk, n_chunks):
    t = pl.program_id(1)

    @pl.when(t == 0)
    def _():
        for h in range(N_HEADS):
            st_ref[h] = s0_ref[h].T

    lbl = lbl_ref[...]
    e = jnp.exp(lbl - jnp.max(lbl, axis=0, keepdims=True))
    lb = e[0:1] / jnp.sum(e, axis=0, keepdims=True)
    gn = gn_ref[...]
    tri = tri_ref[...]
    n_sub = chunk // SUB

    def one_chunk(ci):
        r = slice(ci * chunk, (ci + 1) * chunk)
        fa = lb + (1.0 - lb) * _sigmoid(af_ref[r, :])
        lf = jnp.log(fa)
        ka = 1.0 - fa
        aq = aq_ref[r, :]
        qa = aq * _sigmoid(aq)
        v = ai_ref[r, :].astype(BF16)
        ag = ag_ref[r, :]
        gate = ag * _sigmoid(ag)

        hi = lf.astype(BF16)
        r1 = lf - hi.astype(F32)
        mid = r1.astype(BF16)
        lo = (r1 - mid.astype(F32)).astype(BF16)
        b = (jnp.dot(tri, hi, preferred_element_type=F32)
             + jnp.dot(tri, mid, preferred_element_type=F32)
             + jnp.dot(tri, lo, preferred_element_type=F32))
        b_end = b[chunk - 1:chunk, :]

        q_in = (qa * jnp.exp(b)).astype(BF16)
        k_out = (ka * jnp.exp(b_end - b)).astype(BF16)
        decay = jnp.exp(b_end)
        qp, kp = [], []
        for i in range(n_sub):
            r0, n = i * SUB, (i + 1) * SUB
            m_i = b[r0 + SUB // 2:r0 + SUB // 2 + 1, :]
            qp.append((qa[r0:n] * jnp.exp(b[r0:n] - m_i)).astype(BF16))
            kp.append((ka[:n] * jnp.exp(m_i - b[:n])).astype(BF16))

        st = [st_ref[h] for h in range(N_HEADS)]
        inter = [lax.dot_general(q_in[:, _head(h)], st[h].astype(BF16), NT_DIMS,
                                 preferred_element_type=F32) for h in range(N_HEADS)]
        att = [[lax.dot_general(qp[i][:, _head(h)], kp[i][:, _head(h)], NT_DIMS,
                                preferred_element_type=F32) for h in range(N_HEADS)]
               for i in range(n_sub)]
        for h in range(N_HEADS):
            st_ref[h] = st[h] * decay[:, _head(h)] + lax.dot_general(
                v[:, _head(h)], k_out[:, _head(h)], TN_DIMS, preferred_element_type=F32)
        for i in range(n_sub):
            r0, n = i * SUB, (i + 1) * SUB
            causal = (lax.broadcasted_iota(jnp.int32, (SUB, n), 1)
                      <= lax.broadcasted_iota(jnp.int32, (SUB, n), 0) + r0)
            att[i] = [jnp.where(causal, a, 0.0).astype(BF16) for a in att[i]]
        for h in range(N_HEADS):
            o = jnp.concatenate(
                [inter[h][i * SUB:(i + 1) * SUB]
                 + jnp.dot(att[i][h], v[:(i + 1) * SUB, _head(h)], preferred_element_type=F32)
                 for i in range(n_sub)], axis=0)
            ms = jnp.mean(o * o, axis=-1, keepdims=True)
            o_ref[r, _head(h)] = (o * lax.rsqrt(ms + EPS) * gn * gate[:, _head(h)]).astype(o_ref.dtype)

    for ci in range(n_chunks):
        one_chunk(ci)

    @pl.when(t == pl.num_programs(1) - 1)
    def _():
        for h in range(N_HEADS):
            s_out_ref[h] = st_ref[h].T


def _hgrn2(a_proj, lb_logits, gnorm, s0, *, batch, length, row0, chunk, n_chunks):
    step = chunk * n_chunks
    nt = length // step
    base = row0 // step

    def col(g):
        return pl.BlockSpec((step, WIDTH), lambda b, t: (base + b * nt + t, g))

    st_spec = pl.BlockSpec((None, N_HEADS, HEAD_DIM, HEAD_DIM), lambda b, t: (b, 0, 0, 0))
    tri = jnp.tril(jnp.ones((chunk, chunk), BF16))
    return pl.pallas_call(
        functools.partial(_gla_kernel, chunk=chunk, n_chunks=n_chunks),
        grid=(batch, nt),
        in_specs=[col(0), col(1), col(2), col(3),
                  pl.BlockSpec(lb_logits.shape, lambda b, t: (0, 0)),
                  pl.BlockSpec((1, HEAD_DIM), lambda b, t: (0, 0)),
                  st_spec,
                  pl.BlockSpec((chunk, chunk), lambda b, t: (0, 0))],
        out_specs=[pl.BlockSpec((step, WIDTH), lambda b, t: (b * nt + t, 0)), st_spec],
        out_shape=[jax.ShapeDtypeStruct((batch * length, WIDTH), BF16),
                   jax.ShapeDtypeStruct((batch, N_HEADS, HEAD_DIM, HEAD_DIM), F32)],
        scratch_shapes=[pltpu.VMEM((N_HEADS, HEAD_DIM, HEAD_DIM), F32)],
        compiler_params=_params("parallel", "arbitrary"),
        name="hgrn2",
    )(a_proj, a_proj, a_proj, a_proj, lb_logits, gnorm, s0, tri)


def _attn_prompt_kernel(q_ref, k_ref, v_ref, c_ref, o_ref,
                        vt_ref, cb_ref, s_ref, m_ref, l_ref, acc_ref, *, blk):
    h = pl.program_id(1)
    length = k_ref.shape[0]
    halves = (slice(0, blk // 2), slice(blk // 2, blk))

    for r0 in range(0, length, blk):
        rs = slice(r0, r0 + blk)
        vt_ref[:, rs] = v_ref[rs, :].astype(F32).T.astype(BF16)
        c_cols = c_ref[:, rs].T
        onehot = lax.broadcasted_iota(jnp.int32, c_cols.shape, 1) == h
        cb_ref[rs, :] = jnp.broadcast_to(
            jnp.sum(jnp.where(onehot, c_cols, 0.0), axis=1, keepdims=True), (blk, LANES))

    def scores(iq, j, buf):
        ks = slice(j * blk, (j + 1) * blk)
        kb = k_ref[ks, :]
        cb = cb_ref[ks, :]
        for hv in halves:
            x = lax.dot_general(kb, q_ref[iq * blk + hv.start:iq * blk + hv.stop, :], NT_DIMS,
                                preferred_element_type=F32)
            s_ref[buf, :, hv] = x - jnp.concatenate([cb] * (x.shape[1] // LANES), axis=1)

    def consume(iq, j, buf):
        st = iq % 2
        masked = j == iq
        vt = vt_ref[:, j * blk:(j + 1) * blk]
        p = []
        for hv in halves:
            keys = hv.stop if masked else blk
            x = s_ref[buf, :keys, hv]
            if masked:
                key = lax.broadcasted_iota(jnp.int32, x.shape, 0)
                qry = lax.broadcasted_iota(jnp.int32, x.shape, 1) + hv.start
                x = jnp.where(key <= qry, x, -jnp.inf)
            if j == 0:
                m_new = jnp.max(x, axis=0, keepdims=True)
                e = jnp.exp2(x - m_new)
                l_ref[st, :, hv] = jnp.sum(e, axis=0, keepdims=True)
            else:
                m_prev = m_ref[st, :, hv]
                m_new = jnp.maximum(m_prev, jnp.max(x, axis=0, keepdims=True))
                alpha = jnp.exp2(m_prev - m_new)
                e = jnp.exp2(x - m_new)
                l_ref[st, :, hv] = alpha * l_ref[st, :, hv] + jnp.sum(e, axis=0, keepdims=True)
                acc_ref[st, :, hv] = alpha * acc_ref[st, :, hv]
            m_ref[st, :, hv] = m_new
            p.append(e.astype(BF16))
        for e, hv in zip(p, halves):
            pv = jnp.dot(vt[:, :e.shape[0]], e, preferred_element_type=F32)
            if j == 0:
                acc_ref[st, :, hv] = pv
            else:
                acc_ref[st, :, hv] += pv

    visits = [(iq, j) for iq in range(length // blk) for j in range(iq + 1)]
    scores(*visits[0], 0)
    for n, (iq, j) in enumerate(visits):
        if n + 1 < len(visits):
            scores(*visits[n + 1], (n + 1) % 2)
        consume(iq, j, n % 2)
        if j == iq:
            st = iq % 2
            o_ref[iq * blk:(iq + 1) * blk, :] = (acc_ref[st] / l_ref[st]).T.astype(o_ref.dtype)


def _fox_prompt(q, k, v, c, *, batch, length, blk):
    spec = pl.BlockSpec((length, HEAD_DIM), lambda b, h: (b, h))
    return pl.pallas_call(
        functools.partial(_attn_prompt_kernel, blk=blk),
        grid=(batch, N_HEADS),
        in_specs=[spec, spec, spec, pl.BlockSpec((None,) + c.shape[1:], lambda b, h: (b, 0, 0))],
        out_specs=spec,
        out_shape=jax.ShapeDtypeStruct((batch * length, WIDTH), BF16),
        scratch_shapes=[pltpu.VMEM((HEAD_DIM, length), BF16),
                        pltpu.VMEM((length, LANES), F32),
                        pltpu.VMEM((2, blk, blk), F32),
                        pltpu.VMEM((2, 1, blk), F32), pltpu.VMEM((2, 1, blk), F32),
                        pltpu.VMEM((2, HEAD_DIM, blk), F32)],
        compiler_params=_params("parallel", "parallel"),
        name="fox_prompt",
    )(q, k, v, c)


def _attend_heads(score_fn, value_fn, m_ref, l_ref, acc_ref):
    s = [score_fn(h) for h in range(N_HEADS)]
    stats = []
    for h in range(N_HEADS):
        m_prev = m_ref[h]
        m_new = jnp.maximum(m_prev, jnp.max(s[h], axis=-1, keepdims=True))
        stats.append((jnp.exp2(m_prev - m_new), m_new))
    p = [jnp.exp2(s[h] - stats[h][1]) for h in range(N_HEADS)]
    for h in range(N_HEADS):
        alpha, m_new = stats[h]
        l_ref[h] = alpha * l_ref[h] + jnp.sum(p[h], axis=-1, keepdims=True)
        m_ref[h] = m_new
        acc_ref[:, _head(h)] = alpha * acc_ref[:, _head(h)] + jnp.dot(
            p[h].astype(BF16), value_fn(h), preferred_element_type=F32)


def _attn_sample_kernel(q_ref, kc_hbm, vc_hbm, bc_ref, kn_ref, vn_ref, bn_ref, o_ref,
                        kbuf_ref, vbuf_ref, sem_ref, m_ref, l_ref, acc_ref):
    b, j = pl.program_id(0), pl.program_id(1)
    nj = pl.num_programs(1)
    tk = bc_ref.shape[1]
    step = b * nj + j
    slot = step % 2

    def cache_copies(stream, blk, into):
        return [pltpu.make_async_copy(hbm.at[stream, pl.ds(blk * tk, tk), h, :],
                                      buf.at[into, h], sem_ref.at[t, into])
                for t, (hbm, buf) in enumerate(((kc_hbm, kbuf_ref), (vc_hbm, vbuf_ref)))
                for h in range(N_HEADS)]

    def start_all(copies):
        for cp in copies:
            cp.start()

    @pl.when(step == 0)
    def _():
        start_all(cache_copies(b, j, slot))

    @pl.when(step + 1 < pl.num_programs(0) * nj)
    def _():
        start_all(cache_copies((step + 1) // nj, (step + 1) % nj, 1 - slot))

    @pl.when(j == 0)
    def _():
        m_ref[...] = jnp.full(m_ref.shape, -jnp.inf, F32)
        l_ref[...] = jnp.zeros(l_ref.shape, F32)
        acc_ref[...] = jnp.zeros(acc_ref.shape, F32)

    for cp in cache_copies(b, j, slot):
        cp.wait()
    _attend_heads(
        lambda h: lax.dot_general(q_ref[:, _head(h)], kbuf_ref[slot, h].astype(BF16),
                                  NT_DIMS, preferred_element_type=F32) + bc_ref[h:h + 1, :],
        lambda h: vbuf_ref[slot, h].astype(BF16), m_ref, l_ref, acc_ref)

    @pl.when(j == pl.num_programs(1) - 1)
    def _():
        tq = q_ref.shape[0]
        causal = (lax.broadcasted_iota(jnp.int32, (tq, tq), 1)
                  <= lax.broadcasted_iota(jnp.int32, (tq, tq), 0))
        _attend_heads(
            lambda h: jnp.where(causal,
                                lax.dot_general(q_ref[:, _head(h)], kn_ref[:, _head(h)], NT_DIMS,
                                                preferred_element_type=F32) - bn_ref[h:h + 1, :],
                                -jnp.inf),
            lambda h: vn_ref[:, _head(h)], m_ref, l_ref, acc_ref)
        for h in range(N_HEADS):
            o_ref[:, _head(h)] = (acc_ref[:, _head(h)] / l_ref[h]).astype(o_ref.dtype)


def _fox_sample(q, k_new, v_new, c_new, cache_k, cache_v, cache_bias, *, row0, tk):
    B, H, P = cache_bias.shape
    Ls = c_new.shape[2]
    base = row0 // Ls
    new = pl.BlockSpec((Ls, WIDTH), lambda b, j: (base + b, 0))
    cache = pl.BlockSpec(memory_space=pl.ANY)
    cache_buf = pltpu.VMEM((2, H, tk, HEAD_DIM), F32)
    return pl.pallas_call(
        _attn_sample_kernel,
        grid=(B, P // tk),
        in_specs=[new, cache, cache,
                  pl.BlockSpec((None, N_HEADS, tk), lambda b, j: (b, 0, j)),
                  new, new,
                  pl.BlockSpec((None, N_HEADS, Ls), lambda b, j: (b, 0, 0))],
        out_specs=pl.BlockSpec((Ls, WIDTH), lambda b, j: (b, 0)),
        out_shape=jax.ShapeDtypeStruct((B * Ls, WIDTH), BF16),
        scratch_shapes=[cache_buf, cache_buf, pltpu.SemaphoreType.DMA((2, 2)),
                        pltpu.VMEM((N_HEADS, Ls, 1), F32), pltpu.VMEM((N_HEADS, Ls, 1), F32),
                        pltpu.VMEM((Ls, WIDTH), F32)],
        compiler_params=_params("arbitrary", "arbitrary"),
        name="fox_sample",
    )(q, cache_k, cache_v, cache_bias, k_new, v_new, c_new)


MERGE_COLS = 512


def _merge_kernel(xp_ref, xs_ref, oap_ref, oas_ref, obp_ref, obs_ref, g_ref,
                  wpa_ref, wpb_ref, wo_ref, g2_ref, o_ref, h2_ref, mg_ref, *, n_first):
    first = pl.program_id(0) < n_first
    D = xp_ref.shape[1]
    oa = jnp.where(first, oap_ref[...], oas_ref[...])
    ob = jnp.where(first, obp_ref[...], obs_ref[...])
    cols = min(MERGE_COLS, D)
    for n0 in range(0, D, cols):
        ns = slice(n0, n0 + cols)
        pa = jnp.dot(oa, wpa_ref[:, ns], preferred_element_type=F32)
        pb = jnp.dot(ob, wpb_ref[:, ns], preferred_element_type=F32)
        ga = g_ref[:, ns].astype(F32)
        gb = g_ref[:, D + n0:D + n0 + cols].astype(F32)
        mg_ref[:, ns] = (ga * pa + gb * pb).astype(BF16)
    x = jnp.where(first, xp_ref[...], xs_ref[...])
    x = x + jnp.dot(mg_ref[...], wo_ref[...], preferred_element_type=F32)
    o_ref[...] = x
    h2_ref[...] = _normed(x, g2_ref)


def _merge_out(xp, xs, oa_p, oa_s, ob_p, ob_s, g, w_pa, w_pb, w_o, g2, *, tm):
    D = xp.shape[1]
    T = xp.shape[0] + xs.shape[0]
    n_first = xp.shape[0] // tm
    fixed = lambda i: (0, 0)
    return pl.pallas_call(
        functools.partial(_merge_kernel, n_first=n_first),
        grid=(T // tm,),
        in_specs=[*_two_streams(tm, D, n_first), *_two_streams(tm, WIDTH, n_first),
                  *_two_streams(tm, WIDTH, n_first),
                  pl.BlockSpec((tm, 2 * D), lambda i: (i, 0)),
                  *[pl.BlockSpec(w.shape, fixed, pipeline_mode=pl.Buffered(1))
                    for w in (w_pa, w_pb, w_o)],
                  pl.BlockSpec((1, D), fixed)],
        out_specs=[pl.BlockSpec((tm, D), lambda i: (i, 0))] * 2,
        out_shape=[jax.ShapeDtypeStruct((T, D), F32), jax.ShapeDtypeStruct((T, D), BF16)],
        scratch_shapes=[pltpu.VMEM((tm, D), BF16)],
        compiler_params=_params("parallel"),
        name="merge_out",
    )(xp, xs, oa_p, oa_s, ob_p, ob_s, g, w_pa, w_pb, w_o, g2)


def _ffn_kernel(x_ref, h_ref, w1_ref, w2_ref, gf_ref, yp_ref, ys_ref, acc_ref, *, n_first):
    i, f = pl.program_id(0), pl.program_id(1)

    @pl.when(f == 0)
    def _():
        acc_ref[...] = x_ref[...]

    u = jnp.maximum(jnp.dot(h_ref[...], w1_ref[...], preferred_element_type=F32), 0.0)
    acc_ref[...] += jnp.dot((u * u).astype(BF16), w2_ref[...], preferred_element_type=F32)

    @pl.when(f == pl.num_programs(1) - 1)
    def _():
        x = acc_ref[...]
        ms = jnp.mean(x * x, axis=-1, keepdims=True)
        y = x * lax.rsqrt(ms + EPS) * gf_ref[...]

        @pl.when(i < n_first)
        def _():
            yp_ref[...] = y

        @pl.when(i >= n_first)
        def _():
            ys_ref[...] = y


def _ffn(x, h, w1, w2, gf, *, rows_first, tm, tf):
    T, D = x.shape
    F = w1.shape[1]
    n_first = rows_first // tm
    rows = pl.BlockSpec((tm, D), lambda i, f: (i, 0))
    return pl.pallas_call(
        functools.partial(_ffn_kernel, n_first=n_first),
        grid=(T // tm, F // tf),
        in_specs=[rows, rows,
                  pl.BlockSpec((D, tf), lambda i, f: (0, f)),
                  pl.BlockSpec((tf, D), lambda i, f: (f, 0)),
                  pl.BlockSpec((1, D), lambda i, f: (0, 0))],
        out_specs=list(_two_streams(tm, D, n_first)),
        out_shape=[jax.ShapeDtypeStruct((rows_first, D), F32),
                   jax.ShapeDtypeStruct((T - rows_first, D), F32)],
        scratch_shapes=[pltpu.VMEM((tm, D), F32)],
        compiler_params=_params("arbitrary", "arbitrary"),
        name="ffn",
    )(x, h, w1, w2, gf)


def _tile(n, pref):
    t = min(pref, n)
    while n % t:
        t //= 2
    return t


def kernel(x_prompt, x_sample, cache_fox_k, cache_fox_v, cache_fox_logf, state_hgrn, norm1, w_in,
           b_fox_f, lb_logits, gnorm_a, w_pa, w_pb, w_o, norm2, w1, w2, norm_f):
    B, L, D = x_prompt.shape
    Bs, Ls, _ = x_sample.shape
    depth, _, P, H, DH = cache_fox_k.shape
    assert depth == 1 and H == N_HEADS and DH == HEAD_DIM
    Tp, Ts = B * L, Bs * Ls
    T = Tp + Ts
    W = WIDTH

    xp, xs = x_prompt.reshape(Tp, D), x_sample.reshape(Ts, D)
    w_in_t = jnp.swapaxes(w_in[0], 0, 1)
    w_fl_t = w_in_t[7 * W:7 * W + H].astype(BF16)
    w_g = w_in_t[7 * W + H:]
    g1 = norm1[0].reshape(1, D)

    tm = _tile(math.gcd(Tp, Ts), 512)
    tm1 = _tile(math.gcd(Tp, Ts), 1024)
    h1 = _rmsnorm_rows(xp, xs, g1, tm=tm)
    a_proj = _proj(h1, w_in_t, n_cols=4 * W, act=None, out_dtype=F32, tm=tm1, tn=W)
    gates = _proj(h1, w_g, n_cols=2 * D, act="sigmoid", out_dtype=BF16, tm=tm1,
                  tn=_tile(2 * D, W))
    q_b, k_b, v_b, k_p, k_s, v_p, v_s, fl_t = _proj_fox(h1, w_in_t, w_fl_t, col0=4 * W,
                                                        rows_first=Tp, tm=_tile(tm, 512))

    bias = b_fox_f[0].reshape(H, 1)
    lf_p, c_p = _fox_gates(fl_t, bias, col0=0, cols=Tp, seg=L, block=L, pad_rows=LANES)
    lf_s, c_s = _fox_gates(fl_t, bias, col0=Tp, cols=Ts, seg=Ls, block=_tile(Ts, 1024), pad_rows=H)
    cache_bias = _cache_suffix(jnp.transpose(cache_fox_logf[0], (0, 2, 1)))

    zeros_state = jnp.zeros((B, H, DH, DH), F32)
    chunk_p = _tile(L, 64)
    oa_p, s_p = _hgrn2(a_proj, lb_logits, gnorm_a, zeros_state, batch=B, length=L, row0=0,
                       chunk=chunk_p, n_chunks=_tile(L // chunk_p, 4))
    oa_s, s_s = _hgrn2(a_proj, lb_logits, gnorm_a, state_hgrn[0], batch=Bs, length=Ls, row0=Tp,
                       chunk=_tile(Ls, 64), n_chunks=1)

    ob_p = _fox_prompt(q_b, k_b, v_b, c_p, batch=B, length=L, blk=_tile(L, 512))
    c_s3 = c_s.transpose(1, 0, 2).reshape(H, Bs, Ls).transpose(1, 0, 2)
    ob_s = _fox_sample(q_b, k_b, v_b, c_s3, cache_fox_k[0], cache_fox_v[0], cache_bias,
                       row0=Tp, tk=_tile(P, 1024))

    tm5 = _tile(tm, 512)
    x1, h2 = _merge_out(xp, xs, oa_p, oa_s, ob_p, ob_s, gates, w_pa[0].astype(BF16),
                        w_pb[0].astype(BF16), w_o[0].astype(BF16), norm2[0].reshape(1, D),
                        tm=_tile(tm, 256))
    y_p, y_s = _ffn(x1, h2, w1[0].astype(BF16), w2[0].astype(BF16), norm_f.reshape(1, D),
                    rows_first=Tp, tm=tm5, tf=_tile(w1.shape[-1], 1024))

    return (y_p.reshape(B, L, D), y_s.reshape(Bs, Ls, D),
            k_p.reshape(1, B, L, H, DH), v_p.reshape(1, B, L, H, DH),
            lf_p.T.reshape(1, B, L, H), s_p[None],
            k_s.reshape(1, Bs, Ls, H, DH), v_s.reshape(1, Bs, Ls, H, DH),
            lf_s.T.reshape(1, Bs, Ls, H), s_s[None])
```

```python
import functools
import math

import jax
import jax.numpy as jnp
from jax import lax
from jax.experimental import pallas as pl
from jax.experimental.pallas import tpu as pltpu

F32 = jnp.float32
BF16 = jnp.bfloat16

EPS = 1e-6
N_HEADS = 8
HEAD_DIM = 128
WIDTH = N_HEADS * HEAD_DIM
LANES = 128
VMEM_LIMIT = 56 * 1024 * 1024
LOG2E = math.log2(math.e)
QK_SCALE = HEAD_DIM ** -0.5 * LOG2E

NT_DIMS = (((1,), (1,)), ((), ()))
TN_DIMS = (((0,), (0,)), ((), ()))


def _params(*sem):
    return pltpu.CompilerParams(dimension_semantics=sem, vmem_limit_bytes=VMEM_LIMIT)


def _sigmoid(x):
    return 1.0 / (1.0 + jnp.exp(-x))


def _head(h):
    return slice(h * HEAD_DIM, (h + 1) * HEAD_DIM)


def _two_streams(rows, width, n_first, n_second=None):
    mode = {"pipeline_mode": pl.Buffered(1)} if n_second == 1 else {}
    return (pl.BlockSpec((rows, width), lambda i, *_: (jnp.minimum(i, n_first - 1), 0)),
            pl.BlockSpec((rows, width), lambda i, *_: (jnp.maximum(i - n_first, 0), 0), **mode))


def _head_rows(h, rows):
    return pl.ds(h, rows, stride=N_HEADS)


def _normed(x, g_ref):
    ms = jnp.mean(x * x, axis=-1, keepdims=True)
    return (x * lax.rsqrt(ms + EPS) * g_ref[...]).astype(BF16)


def _rmsnorm_kernel(xp_ref, xs_ref, g_ref, o_ref, *, n_first):
    @pl.when(pl.program_id(0) < n_first)
    def _():
        o_ref[...] = _normed(xp_ref[...], g_ref)

    @pl.when(pl.program_id(0) >= n_first)
    def _():
        o_ref[...] = _normed(xs_ref[...], g_ref)


def _rmsnorm_rows(xp, xs, g, *, tm):
    D = xp.shape[1]
    n_first = xp.shape[0] // tm
    T = xp.shape[0] + xs.shape[0]
    return pl.pallas_call(
        functools.partial(_rmsnorm_kernel, n_first=n_first),
        grid=(T // tm,),
        in_specs=[*_two_streams(tm, D, n_first), pl.BlockSpec((1, D), lambda i: (0, 0))],
        out_specs=pl.BlockSpec((tm, D), lambda i: (i, 0)),
        out_shape=jax.ShapeDtypeStruct((T, D), BF16),
        compiler_params=_params("parallel"),
        name="rmsnorm_rows",
    )(xp, xs, g)


def _first_pass_weights(n_blocks, block0, shape, buffers=2):
    return pl.BlockSpec(
        shape, lambda i, j: (block0 + jnp.where(i == 0, j, n_blocks - 1), 0),
        pipeline_mode=pl.Buffered(buffers))


def _proj_kernel(h_ref, w_ref, o_ref, wb_ref, *, act):
    j = pl.program_id(1)

    @pl.when(pl.program_id(0) == 0)
    def _():
        wb_ref[j] = w_ref[...].astype(BF16)

    acc = lax.dot_general(h_ref[...], wb_ref[j], NT_DIMS, preferred_element_type=F32)
    if act == "sigmoid":
        acc = _sigmoid(acc)
    o_ref[...] = acc.astype(o_ref.dtype)


def _proj(h, w, *, n_cols, act, out_dtype, tm, tn):
    T, D = h.shape
    nj = n_cols // tn
    return pl.pallas_call(
        functools.partial(_proj_kernel, act=act),
        grid=(T // tm, nj),
        in_specs=[pl.BlockSpec((tm, D), lambda i, j: (i, 0)),
                  _first_pass_weights(nj, 0, (tn, D), buffers=1)],
        out_specs=pl.BlockSpec((tm, tn), lambda i, j: (i, j)),
        out_shape=jax.ShapeDtypeStruct((T, n_cols), out_dtype),
        scratch_shapes=[pltpu.VMEM((nj, tn, D), BF16)],
        compiler_params=_params("arbitrary", "arbitrary"),
        name="proj_" + (act or "lin"),
    )(h, w)


def _proj_fox_kernel(h_ref, w_ref, wfl_ref,
                     q_ref, kb_ref, vb_ref, kp_ref, ks_ref, vp_ref, vs_ref, fl_ref, wb_ref,
                     *, n_first):
    i, j = pl.program_id(0), pl.program_id(1)

    @pl.when(i == 0)
    def _():
        wb_ref[j] = w_ref[...].astype(BF16)

    @pl.when(j == 0)
    def _():
        fl_ref[...] = lax.dot_general(wfl_ref[...], h_ref[...], NT_DIMS,
                                      preferred_element_type=F32)

    h = h_ref[...]
    tm = h.shape[0]
    pair = 2 * HEAD_DIM

    def column_pairs():
        for c in range(WIDTH // pair):
            cs = slice(c * pair, (c + 1) * pair)
            yield c, cs, lax.dot_general(h, wb_ref[j, cs, :], NT_DIMS,
                                         preferred_element_type=F32)

    @pl.when(j == 0)
    def _():
        for _, cs, acc in column_pairs():
            q_ref[:, cs] = (acc * QK_SCALE).astype(BF16)

    def key_or_value(bf_ref, heads_ref):
        for c, cs, acc in column_pairs():
            bf_ref[:, cs] = acc.astype(BF16)
            for hh in range(2):
                heads_ref[_head_rows(2 * c + hh, tm), :] = acc[:, _head(hh)]

    for jj, bf_ref, first_ref, second_ref in ((1, kb_ref, kp_ref, ks_ref),
                                              (2, vb_ref, vp_ref, vs_ref)):
        pl.when((j == jj) & (i < n_first))(
            functools.partial(key_or_value, bf_ref, first_ref))
        pl.when((j == jj) & (i >= n_first))(
            functools.partial(key_or_value, bf_ref, second_ref))


def _proj_fox(h, w, w_fl_t, *, col0, rows_first, tm):
    T, D = h.shape
    Tp, Ts = rows_first, T - rows_first
    n_first = Tp // tm
    kv_p, kv_s = _two_streams(tm * N_HEADS, HEAD_DIM, n_first)
    kv_shape = lambda rows: jax.ShapeDtypeStruct((rows * N_HEADS, HEAD_DIM), F32)
    tok = pl.BlockSpec((tm, WIDTH), lambda i, j: (i, 0))
    tok_shape = jax.ShapeDtypeStruct((T, WIDTH), BF16)
    return pl.pallas_call(
        functools.partial(_proj_fox_kernel, n_first=n_first),
        grid=(T // tm, 3),
        in_specs=[pl.BlockSpec((tm, D), lambda i, j: (i, 0)),
                  _first_pass_weights(3, col0 // WIDTH, (WIDTH, D), buffers=1),
                  pl.BlockSpec((N_HEADS, D), lambda i, j: (0, 0))],
        out_specs=[tok, tok, tok, kv_p, kv_s, kv_p, kv_s,
                   pl.BlockSpec((N_HEADS, tm), lambda i, j: (0, i))],
        out_shape=[tok_shape, tok_shape, tok_shape,
                   kv_shape(Tp), kv_shape(Ts), kv_shape(Tp), kv_shape(Ts),
                   jax.ShapeDtypeStruct((N_HEADS, T), F32)],
        scratch_shapes=[pltpu.VMEM((3, WIDTH, D), BF16)],
        compiler_params=_params("arbitrary", "arbitrary"),
        name="proj_fox",
    )(h, w, w_fl_t)


def _lane_cumsum(x, seg):
    pos = lax.broadcasted_iota(jnp.int32, x.shape, x.ndim - 1) & (seg - 1)
    shift = 1
    while shift < seg:
        x = x + jnp.where(pos >= shift, pltpu.roll(x, shift, x.ndim - 1), 0.0)
        shift *= 2
    return x


def _gate_kernel(fl_ref, bias_ref, lf_ref, c_ref, *, seg):
    z = fl_ref[...] + bias_ref[...]
    lf = jnp.minimum(z, 0.0) - jnp.log(1.0 + jnp.exp(-jnp.abs(z)))
    lf_ref[...] = lf
    c = _lane_cumsum(lf, seg) * LOG2E
    c_ref[...] = jnp.zeros(c_ref.shape, F32)
    c_ref[0:c.shape[0], :] = c


def _fox_gates(fl_t, bias, *, col0, cols, seg, block, pad_rows):
    H = fl_t.shape[0]
    nb = cols // block
    base = col0 // block
    return pl.pallas_call(
        functools.partial(_gate_kernel, seg=seg),
        grid=(nb,),
        in_specs=[pl.BlockSpec((H, block), lambda i: (0, base + i)),
                  pl.BlockSpec((H, 1), lambda i: (0, 0))],
        out_specs=[pl.BlockSpec((H, block), lambda i: (0, i)),
                   pl.BlockSpec((None, pad_rows, block), lambda i: (i, 0, 0))],
        out_shape=[jax.ShapeDtypeStruct((H, cols), F32),
                   jax.ShapeDtypeStruct((nb, pad_rows, block), F32)],
        compiler_params=_params("parallel"),
        name="fox_gates",
    )(fl_t, bias)


def _suffix_kernel(lf_ref, o_ref):
    lf = lf_ref[...]
    c = _lane_cumsum(lf, lf.shape[-1])
    o_ref[...] = (c[:, -1:] - c) * LOG2E


def _cache_suffix(clf_t):
    B, H, P = clf_t.shape
    rows = _tile(B * H, 32)
    spec = pl.BlockSpec((rows, P), lambda r: (r, 0))
    return pl.pallas_call(
        _suffix_kernel,
        grid=(B * H // rows,),
        in_specs=[spec],
        out_specs=spec,
        out_shape=jax.ShapeDtypeStruct((B * H, P), F32),
        compiler_params=_params("parallel"),
        name="fox_cache_suffix",
    )(clf_t.reshape(B * H, P)).reshape(B, H, P)


SUB = 16


def _gla_kernel(aq_ref, af_ref, ai_ref, ag_ref, lbl_ref, gn_ref, s0_ref, tri_ref,
                o_ref, s_out_ref, st_ref, *, chunk, n_chunks):
    t = pl.program_id(1)

    @pl.when(t == 0)
    def _():
        for h in range(N_HEADS):
            st_ref[h] = s0_ref[h].T

    lbl = lbl_ref[...]
    e = jnp.exp(lbl - jnp.max(lbl, axis=0, keepdims=True))
    lb = e[0:1] / jnp.sum(e, axis=0, keepdims=True)
    gn = gn_ref[...]
    tri = tri_ref[...]
    n_sub = chunk // SUB

    def one_chunk(ci):
        r = slice(ci * chunk, (ci + 1) * chunk)
        fa = lb + (1.0 - lb) * _sigmoid(af_ref[r, :])
        lf = jnp.log(fa)
        ka = 1.0 - fa
        aq = aq_ref[r, :]
        qa = aq * _sigmoid(aq)
        v = ai_ref[r, :].astype(BF16)
        ag = ag_ref[r, :]
        gate = ag * _sigmoid(ag)

        hi = lf.astype(BF16)
        r1 = lf - hi.astype(F32)
        mid = r1.astype(BF16)
        lo = (r1 - mid.astype(F32)).astype(BF16)
        b = (jnp.dot(tri, hi, preferred_element_type=F32)
             + jnp.dot(tri, mid, preferred_element_type=F32)
             + jnp.dot(tri, lo, preferred_element_type=F32))
        b_end = b[chunk - 1:chunk, :]

        q_in = (qa * jnp.exp(b)).astype(BF16)
        k_out = (ka * jnp.exp(b_end - b)).astype(BF16)
        decay = jnp.exp(b_end)
        qp, kp = [], []
        for i in range(n_sub):
            r0, n = i * SUB, (i + 1) * SUB
            m_i = b[r0 + SUB // 2:r0 + SUB // 2 + 1, :]
            qp.append((qa[r0:n] * jnp.exp(b[r0:n] - m_i)).astype(BF16))
            kp.append((ka[:n] * jnp.exp(m_i - b[:n])).astype(BF16))

        st = [st_ref[h] for h in range(N_HEADS)]
        inter = [lax.dot_general(q_in[:, _head(h)], st[h].astype(BF16), NT_DIMS,
                                 preferred_element_type=F32) for h in range(N_HEADS)]
        att = [[lax.dot_general(qp[i][:, _head(h)], kp[i][:, _head(h)], NT_DIMS,
                                preferred_element_type=F32) for h in range(N_HEADS)]
               for i in range(n_sub)]
        for h in range(N_HEADS):
            st_ref[h] = st[h] * decay[:, _head(h)] + lax.dot_general(
                v[:, _head(h)], k_out[:, _head(h)], TN_DIMS, preferred_element_type=F32)
        for i in range(n_sub):
            r0, n = i * SUB, (i + 1) * SUB
            causal = (lax.broadcasted_iota(jnp.int32, (SUB, n), 1)
                      <= lax.broadcasted_iota(jnp.int32, (SUB, n), 0) + r0)
            att[i] = [jnp.where(causal, a, 0.0).astype(BF16) for a in att[i]]
        for h in range(N_HEADS):
            o = jnp.concatenate(
                [inter[h][i * SUB:(i + 1) * SUB]
                 + jnp.dot(att[i][h], v[:(i + 1) * SUB, _head(h)], preferred_element_type=F32)
                 for i in range(n_sub)], axis=0)
            ms = jnp.mean(o * o, axis=-1, keepdims=True)
            o_ref[r, _head(h)] = (o * lax.rsqrt(ms + EPS) * gn * gate[:, _head(h)]).astype(o_ref.dtype)

    for ci in range(n_chunks):
        one_chunk(ci)

    @pl.when(t == pl.num_programs(1) - 1)
    def _():
        for h in range(N_HEADS):
            s_out_ref[h] = st_ref[h].T


def _hgrn2(a_proj, lb_logits, gnorm, s0, *, batch, length, row0, chunk, n_chunks):
    step = chunk * n_chunks
    nt = length // step
    base = row0 // step

    def col(g):
        return pl.BlockSpec((step, WIDTH), lambda b, t: (base + b * nt + t, g))

    st_spec = pl.BlockSpec((None, N_HEADS, HEAD_DIM, HEAD_DIM), lambda b, t: (b, 0, 0, 0))
    tri = jnp.tril(jnp.ones((chunk, chunk), BF16))
    return pl.pallas_call(
        functools.partial(_gla_kernel, chunk=chunk, n_chunks=n_chunks),
        grid=(batch, nt),
        in_specs=[col(0), col(1), col(2), col(3),
                  pl.BlockSpec(lb_logits.shape, lambda b, t: (0, 0)),
                  pl.BlockSpec((1, HEAD_DIM), lambda b, t: (0, 0)),
                  st_spec,
                  pl.BlockSpec((chunk, chunk), lambda b, t: (0, 0))],
        out_specs=[pl.BlockSpec((step, WIDTH), lambda b, t: (b * nt + t, 0)), st_spec],
        out_shape=[jax.ShapeDtypeStruct((batch * length, WIDTH), BF16),
                   jax.ShapeDtypeStruct((batch, N_HEADS, HEAD_DIM, HEAD_DIM), F32)],
        scratch_shapes=[pltpu.VMEM((N_HEADS, HEAD_DIM, HEAD_DIM), F32)],
        compiler_params=_params("parallel", "arbitrary"),
        name="hgrn2",
    )(a_proj, a_proj, a_proj, a_proj, lb_logits, gnorm, s0, tri)


def _attn_prompt_kernel(q_ref, k_ref, v_ref, c_ref, o_ref,
                        vt_ref, cb_ref, s_ref, m_ref, l_ref, acc_ref, *, blk):
    h = pl.program_id(1)
    length = k_ref.shape[0]
    halves = (slice(0, blk // 2), slice(blk // 2, blk))

    for r0 in range(0, length, blk):
        rs = slice(r0, r0 + blk)
        vt_ref[:, rs] = v_ref[rs, :].astype(F32).T.astype(BF16)
        c_cols = c_ref[:, rs].T
        onehot = lax.broadcasted_iota(jnp.int32, c_cols.shape, 1) == h
        cb_ref[rs, :] = jnp.broadcast_to(
            jnp.sum(jnp.where(onehot, c_cols, 0.0), axis=1, keepdims=True), (blk, LANES))

    def scores(iq, j, buf):
        ks = slice(j * blk, (j + 1) * blk)
        kb = k_ref[ks, :]
        cb = cb_ref[ks, :]
        for hv in halves:
            x = lax.dot_general(kb, q_ref[iq * blk + hv.start:iq * blk + hv.stop, :], NT_DIMS,
                                preferred_element_type=F32)
            s_ref[buf, :, hv] = x - jnp.concatenate([cb] * (x.shape[1] // LANES), axis=1)

    def consume(iq, j, buf):
        st = iq % 2
        masked = j == iq
        vt = vt_ref[:, j * blk:(j + 1) * blk]
        p = []
        for hv in halves:
            keys = hv.stop if masked else blk
            x = s_ref[buf, :keys, hv]
            if masked:
                key = lax.broadcasted_iota(jnp.int32, x.shape, 0)
                qry = lax.broadcasted_iota(jnp.int32, x.shape, 1) + hv.start
                x = jnp.where(key <= qry, x, -jnp.inf)
            if j == 0:
                m_new = jnp.max(x, axis=0, keepdims=True)
                e = jnp.exp2(x - m_new)
                l_ref[st, :, hv] = jnp.sum(e, axis=0, keepdims=True)
            else:
                m_prev = m_ref[st, :, hv]
                m_new = jnp.maximum(m_prev, jnp.max(x, axis=0, keepdims=True))
                alpha = jnp.exp2(m_prev - m_new)
                e = jnp.exp2(x - m_new)
                l_ref[st, :, hv] = alpha * l_ref[st, :, hv] + jnp.sum(e, axis=0, keepdims=True)
                acc_ref[st, :, hv] = alpha * acc_ref[st, :, hv]
            m_ref[st, :, hv] = m_new
            p.append(e.astype(BF16))
        for e, hv in zip(p, halves):
            pv = jnp.dot(vt[:, :e.shape[0]], e, preferred_element_type=F32)
            if j == 0:
                acc_ref[st, :, hv] = pv
            else:
                acc_ref[st, :, hv] += pv

    visits = [(iq, j) for iq in range(length // blk) for j in range(iq + 1)]
    scores(*visits[0], 0)
    for n, (iq, j) in enumerate(visits):
        if n + 1 < len(visits):
            scores(*visits[n + 1], (n + 1) % 2)
        consume(iq, j, n % 2)
        if j == iq:
            st = iq % 2
            o_ref[iq * blk:(iq + 1) * blk, :] = (acc_ref[st] / l_ref[st]).T.astype(o_ref.dtype)


def _fox_prompt(q, k, v, c, *, batch, length, blk):
    spec = pl.BlockSpec((length, HEAD_DIM), lambda b, h: (b, h))
    return pl.pallas_call(
        functools.partial(_attn_prompt_kernel, blk=blk),
        grid=(batch, N_HEADS),
        in_specs=[spec, spec, spec, pl.BlockSpec((None,) + c.shape[1:], lambda b, h: (b, 0, 0))],
        out_specs=spec,
        out_shape=jax.ShapeDtypeStruct((batch * length, WIDTH), BF16),
        scratch_shapes=[pltpu.VMEM((HEAD_DIM, length), BF16),
                        pltpu.VMEM((length, LANES), F32),
                        pltpu.VMEM((2, blk, blk), F32),
                        pltpu.VMEM((2, 1, blk), F32), pltpu.VMEM((2, 1, blk), F32),
                        pltpu.VMEM((2, HEAD_DIM, blk), F32)],
        compiler_params=_params("parallel", "parallel"),
        name="fox_prompt",
    )(q, k, v, c)


def _attend_heads(score_fn, value_fn, m_ref, l_ref, acc_ref):
    s = [score_fn(h) for h in range(N_HEADS)]
    stats = []
    for h in range(N_HEADS):
        m_prev = m_ref[h]
        m_new = jnp.maximum(m_prev, jnp.max(s[h], axis=-1, keepdims=True))
        stats.append((jnp.exp2(m_prev - m_new), m_new))
    p = [jnp.exp2(s[h] - stats[h][1]) for h in range(N_HEADS)]
    for h in range(N_HEADS):
        alpha, m_new = stats[h]
        l_ref[h] = alpha * l_ref[h] + jnp.sum(p[h], axis=-1, keepdims=True)
        m_ref[h] = m_new
        acc_ref[:, _head(h)] = alpha * acc_ref[:, _head(h)] + jnp.dot(
            p[h].astype(BF16), value_fn(h), preferred_element_type=F32)


def _attn_sample_kernel(q_ref, kc_hbm, vc_hbm, bc_ref, kn_ref, vn_ref, bn_ref, o_ref,
                        kbuf_ref, vbuf_ref, sem_ref, m_ref, l_ref, acc_ref):
    b, j = pl.program_id(0), pl.program_id(1)
    nj = pl.num_programs(1)
    tk = bc_ref.shape[1]
    step = b * nj + j
    slot = step % 2

    def cache_copies(stream, blk, into):
        return [pltpu.make_async_copy(hbm.at[stream, pl.ds(blk * tk, tk), h, :],
                                      buf.at[into, h], sem_ref.at[t, into])
                for t, (hbm, buf) in enumerate(((kc_hbm, kbuf_ref), (vc_hbm, vbuf_ref)))
                for h in range(N_HEADS)]

    def start_all(copies):
        for cp in copies:
            cp.start()

    @pl.when(step == 0)
    def _():
        start_all(cache_copies(b, j, slot))

    @pl.when(step + 1 < pl.num_programs(0) * nj)
    def _():
        start_all(cache_copies((step + 1) // nj, (step + 1) % nj, 1 - slot))

    @pl.when(j == 0)
    def _():
        m_ref[...] = jnp.full(m_ref.shape, -jnp.inf, F32)
        l_ref[...] = jnp.zeros(l_ref.shape, F32)
        acc_ref[...] = jnp.zeros(acc_ref.shape, F32)

    for cp in cache_copies(b, j, slot):
        cp.wait()
    _attend_heads(
        lambda h: lax.dot_general(q_ref[:, _head(h)], kbuf_ref[slot, h].astype(BF16),
                                  NT_DIMS, preferred_element_type=F32) + bc_ref[h:h + 1, :],
        lambda h: vbuf_ref[slot, h].astype(BF16), m_ref, l_ref, acc_ref)

    @pl.when(j == pl.num_programs(1) - 1)
    def _():
        tq = q_ref.shape[0]
        causal = (lax.broadcasted_iota(jnp.int32, (tq, tq), 1)
                  <= lax.broadcasted_iota(jnp.int32, (tq, tq), 0))
        _attend_heads(
            lambda h: jnp.where(causal,
                                lax.dot_general(q_ref[:, _head(h)], kn_ref[:, _head(h)], NT_DIMS,
                                                preferred_element_type=F32) - bn_ref[h:h + 1, :],
                                -jnp.inf),
            lambda h: vn_ref[:, _head(h)], m_ref, l_ref, acc_ref)
        for h in range(N_HEADS):
            o_ref[:, _head(h)] = (acc_ref[:, _head(h)] / l_ref[h]).astype(o_ref.dtype)


def _fox_sample(q, k_new, v_new, c_new, cache_k, cache_v, cache_bias, *, row0, tk):
    B, H, P = cache_bias.shape
    Ls = c_new.shape[2]
    base = row0 // Ls
    new = pl.BlockSpec((Ls, WIDTH), lambda b, j: (base + b, 0))
    cache = pl.BlockSpec(memory_space=pl.ANY)
    cache_buf = pltpu.VMEM((2, H, tk, HEAD_DIM), F32)
    return pl.pallas_call(
        _attn_sample_kernel,
        grid=(B, P // tk),
        in_specs=[new, cache, cache,
                  pl.BlockSpec((None, N_HEADS, tk), lambda b, j: (b, 0, j)),
                  new, new,
                  pl.BlockSpec((None, N_HEADS, Ls), lambda b, j: (b, 0, 0))],
        out_specs=pl.BlockSpec((Ls, WIDTH), lambda b, j: (b, 0)),
        out_shape=jax.ShapeDtypeStruct((B * Ls, WIDTH), BF16),
        scratch_shapes=[cache_buf, cache_buf, pltpu.SemaphoreType.DMA((2, 2)),
                        pltpu.VMEM((N_HEADS, Ls, 1), F32), pltpu.VMEM((N_HEADS, Ls, 1), F32),
                        pltpu.VMEM((Ls, WIDTH), F32)],
        compiler_params=_params("arbitrary", "arbitrary"),
        name="fox_sample",
    )(q, cache_k, cache_v, cache_bias, k_new, v_new, c_new)


MERGE_COLS = 512


def _merge_kernel(xp_ref, xs_ref, oap_ref, oas_ref, obp_ref, obs_ref, g_ref,
                  wpa_ref, wpb_ref, wo_ref, g2_ref, o_ref, h2_ref, mg_ref, *, n_first):
    first = pl.program_id(0) < n_first
    D = xp_ref.shape[1]
    oa = jnp.where(first, oap_ref[...], oas_ref[...])
    ob = jnp.where(first, obp_ref[...], obs_ref[...])
    cols = min(MERGE_COLS, D)
    for n0 in range(0, D, cols):
        ns = slice(n0, n0 + cols)
        pa = jnp.dot(oa, wpa_ref[:, ns], preferred_element_type=F32)
        pb = jnp.dot(ob, wpb_ref[:, ns], preferred_element_type=F32)
        ga = g_ref[:, ns].astype(F32)
        gb = g_ref[:, D + n0:D + n0 + cols].astype(F32)
        mg_ref[:, ns] = (ga * pa + gb * pb).astype(BF16)
    x = jnp.where(first, xp_ref[...], xs_ref[...])
    x = x + jnp.dot(mg_ref[...], wo_ref[...], preferred_element_type=F32)
    o_ref[...] = x
    h2_ref[...] = _normed(x, g2_ref)


def _merge_out(xp, xs, oa_p, oa_s, ob_p, ob_s, g, w_pa, w_pb, w_o, g2, *, tm):
    D = xp.shape[1]
    T = xp.shape[0] + xs.shape[0]
    n_first = xp.shape[0] // tm
    fixed = lambda i: (0, 0)
    return pl.pallas_call(
        functools.partial(_merge_kernel, n_first=n_first),
        grid=(T // tm,),
        in_specs=[*_two_streams(tm, D, n_first), *_two_streams(tm, WIDTH, n_first),
                  *_two_streams(tm, WIDTH, n_first),
                  pl.BlockSpec((tm, 2 * D), lambda i: (i, 0)),
                  *[pl.BlockSpec(w.shape, fixed, pipeline_mode=pl.Buffered(1))
                    for w in (w_pa, w_pb, w_o)],
                  pl.BlockSpec((1, D), fixed)],
        out_specs=[pl.BlockSpec((tm, D), lambda i: (i, 0))] * 2,
        out_shape=[jax.ShapeDtypeStruct((T, D), F32), jax.ShapeDtypeStruct((T, D), BF16)],
        scratch_shapes=[pltpu.VMEM((tm, D), BF16)],
        compiler_params=_params("parallel"),
        name="merge_out",
    )(xp, xs, oa_p, oa_s, ob_p, ob_s, g, w_pa, w_pb, w_o, g2)


def _ffn_kernel(x_ref, h_ref, w1_ref, w2_ref, gf_ref, yp_ref, ys_ref, acc_ref, *, n_first):
    i, f = pl.program_id(0), pl.program_id(1)

    @pl.when(f == 0)
    def _():
        acc_ref[...] = x_ref[...]

    u = jnp.maximum(jnp.dot(h_ref[...], w1_ref[...], preferred_element_type=F32), 0.0)
    acc_ref[...] += jnp.dot((u * u).astype(BF16), w2_ref[...], preferred_element_type=F32)

    @pl.when(f == pl.num_programs(1) - 1)
    def _():
        x = acc_ref[...]
        ms = jnp.mean(x * x, axis=-1, keepdims=True)
        y = x * lax.rsqrt(ms + EPS) * gf_ref[...]

        @pl.when(i < n_first)
        def _():
            yp_ref[...] = y

        @pl.when(i >= n_first)
        def _():
            ys_ref[...] = y


def _ffn(x, h, w1, w2, gf, *, rows_first, tm, tf):
    T, D = x.shape
    F = w1.shape[1]
    n_first = rows_first // tm
    rows = pl.BlockSpec((tm, D), lambda i, f: (i, 0))
    return pl.pallas_call(
        functools.partial(_ffn_kernel, n_first=n_first),
        grid=(T // tm, F // tf),
        in_specs=[rows, rows,
                  pl.BlockSpec((D, tf), lambda i, f: (0, f)),
                  pl.BlockSpec((tf, D), lambda i, f: (f, 0)),
                  pl.BlockSpec((1, D), lambda i, f: (0, 0))],
        out_specs=list(_two_streams(tm, D, n_first)),
        out_shape=[jax.ShapeDtypeStruct((rows_first, D), F32),
                   jax.ShapeDtypeStruct((T - rows_first, D), F32)],
        scratch_shapes=[pltpu.VMEM((tm, D), F32)],
        compiler_params=_params("arbitrary", "arbitrary"),
        name="ffn",
    )(x, h, w1, w2, gf)


def _tile(n, pref):
    t = min(pref, n)
    while n % t:
        t //= 2
    return t


def kernel(x_prompt, x_sample, cache_fox_k, cache_fox_v, cache_fox_logf, state_hgrn, norm1, w_in,
           b_fox_f, lb_logits, gnorm_a, w_pa, w_pb, w_o, norm2, w1, w2, norm_f):
    B, L, D = x_prompt.shape
    Bs, Ls, _ = x_sample.shape
    depth, _, P, H, DH = cache_fox_k.shape
    assert depth == 1 and H == N_HEADS and DH == HEAD_DIM
    Tp, Ts = B * L, Bs * Ls
    T = Tp + Ts
    W = WIDTH

    xp, xs = x_prompt.reshape(Tp, D), x_sample.reshape(Ts, D)
    w_in_t = jnp.swapaxes(w_in[0], 0, 1)
    w_fl_t = w_in_t[7 * W:7 * W + H].astype(BF16)
    w_g = w_in_t[7 * W + H:]
    g1 = norm1[0].reshape(1, D)

    tm = _tile(math.gcd(Tp, Ts), 512)
    tm1 = _tile(math.gcd(Tp, Ts), 1024)
    h1 = _rmsnorm_rows(xp, xs, g1, tm=tm)
    a_proj = _proj(h1, w_in_t, n_cols=4 * W, act=None, out_dtype=F32, tm=tm1, tn=W)
    gates = _proj(h1, w_g, n_cols=2 * D, act="sigmoid", out_dtype=BF16, tm=tm1,
                  tn=_tile(2 * D, W))
    q_b, k_b, v_b, k_p, k_s, v_p, v_s, fl_t = _proj_fox(h1, w_in_t, w_fl_t, col0=4 * W,
                                                        rows_first=Tp, tm=_tile(tm, 512))

    bias = b_fox_f[0].reshape(H, 1)
    lf_p, c_p = _fox_gates(fl_t, bias, col0=0, cols=Tp, seg=L, block=L, pad_rows=LANES)
    lf_s, c_s = _fox_gates(fl_t, bias, col0=Tp, cols=Ts, seg=Ls, block=_tile(Ts, 1024), pad_rows=H)
    cache_bias = _cache_suffix(jnp.transpose(cache_fox_logf[0], (0, 2, 1)))

    zeros_state = jnp.zeros((B, H, DH, DH), F32)
    chunk_p = _tile(L, 64)
    oa_p, s_p = _hgrn2(a_proj, lb_logits, gnorm_a, zeros_state, batch=B, length=L, row0=0,
                       chunk=chunk_p, n_chunks=_tile(L // chunk_p, 8))
    oa_s, s_s = _hgrn2(a_proj, lb_logits, gnorm_a, state_hgrn[0], batch=Bs, length=Ls, row0=Tp,
                       chunk=_tile(Ls, 64), n_chunks=1)

    ob_p = _fox_prompt(q_b, k_b, v_b, c_p, batch=B, length=L, blk=_tile(L, 512))
    c_s3 = c_s.transpose(1, 0, 2).reshape(H, Bs, Ls).transpose(1, 0, 2)
    ob_s = _fox_sample(q_b, k_b, v_b, c_s3, cache_fox_k[0], cache_fox_v[0], cache_bias,
                       row0=Tp, tk=_tile(P, 1024))

    tm5 = _tile(tm, 512)
    x1, h2 = _merge_out(xp, xs, oa_p, oa_s, ob_p, ob_s, gates, w_pa[0].astype(BF16),
                        w_pb[0].astype(BF16), w_o[0].astype(BF16), norm2[0].reshape(1, D),
                        tm=_tile(tm, 256))
    y_p, y_s = _ffn(x1, h2, w1[0].astype(BF16), w2[0].astype(BF16), norm_f.reshape(1, D),
                    rows_first=Tp, tm=tm5, tf=_tile(w1.shape[-1], 1024))

    return (y_p.reshape(B, L, D), y_s.reshape(Bs, Ls, D),
            k_p.reshape(1, B, L, H, DH), v_p.reshape(1, B, L, H, DH),
            lf_p.T.reshape(1, B, L, H), s_p[None],
            k_s.reshape(1, Bs, Ls, H, DH), v_s.reshape(1, Bs, Ls, H, DH),
            lf_s.T.reshape(1, Bs, Ls, H), s_s[None])
```

```python
import functools
import math

import jax
import jax.numpy as jnp
from jax import lax
from jax.experimental import pallas as pl
from jax.experimental.pallas import tpu as pltpu

F32 = jnp.float32
BF16 = jnp.bfloat16

EPS = 1e-6
N_HEADS = 8
HEAD_DIM = 128
WIDTH = N_HEADS * HEAD_DIM
LANES = 128
VMEM_LIMIT = 56 * 1024 * 1024
LOG2E = math.log2(math.e)
QK_SCALE = HEAD_DIM ** -0.5 * LOG2E

NT_DIMS = (((1,), (1,)), ((), ()))
TN_DIMS = (((0,), (0,)), ((), ()))


def _params(*sem):
    return pltpu.CompilerParams(dimension_semantics=sem, vmem_limit_bytes=VMEM_LIMIT)


def _sigmoid(x):
    return 1.0 / (1.0 + jnp.exp(-x))


def _head(h):
    return slice(h * HEAD_DIM, (h + 1) * HEAD_DIM)


def _two_streams(rows, width, n_first, n_second=None):
    mode = {"pipeline_mode": pl.Buffered(1)} if n_second == 1 else {}
    return (pl.BlockSpec((rows, width), lambda i, *_: (jnp.minimum(i, n_first - 1), 0)),
            pl.BlockSpec((rows, width), lambda i, *_: (jnp.maximum(i - n_first, 0), 0), **mode))


def _head_rows(h, rows):
    return pl.ds(h, rows, stride=N_HEADS)


def _normed(x, g_ref):
    ms = jnp.mean(x * x, axis=-1, keepdims=True)
    return (x * lax.rsqrt(ms + EPS) * g_ref[...]).astype(BF16)


def _rmsnorm_kernel(xp_ref, xs_ref, g_ref, o_ref, *, n_first):
    @pl.when(pl.program_id(0) < n_first)
    def _():
        o_ref[...] = _normed(xp_ref[...], g_ref)

    @pl.when(pl.program_id(0) >= n_first)
    def _():
        o_ref[...] = _normed(xs_ref[...], g_ref)


def _rmsnorm_rows(xp, xs, g, *, tm):
    D = xp.shape[1]
    n_first = xp.shape[0] // tm
    T = xp.shape[0] + xs.shape[0]
    return pl.pallas_call(
        functools.partial(_rmsnorm_kernel, n_first=n_first),
        grid=(T // tm,),
        in_specs=[*_two_streams(tm, D, n_first), pl.BlockSpec((1, D), lambda i: (0, 0))],
        out_specs=pl.BlockSpec((tm, D), lambda i: (i, 0)),
        out_shape=jax.ShapeDtypeStruct((T, D), BF16),
        compiler_params=_params("parallel"),
        name="rmsnorm_rows",
    )(xp, xs, g)


def _first_pass_weights(n_blocks, row0, shape, buffers=2):
    rows, d = shape
    return pl.BlockSpec(
        (pl.Element(rows), pl.Element(d)),
        lambda i, j: (pl.multiple_of(row0 + jnp.where(i == 0, j, n_blocks - 1) * rows, 8), 0),
        pipeline_mode=pl.Buffered(buffers))


def _proj_kernel(h_ref, w_ref, o_ref, wb_ref, *, act):
    j = pl.program_id(1)

    @pl.when(pl.program_id(0) == 0)
    def _():
        wb_ref[j] = w_ref[...].astype(BF16)

    acc = lax.dot_general(h_ref[...], wb_ref[j], NT_DIMS, preferred_element_type=F32)
    if act == "sigmoid":
        acc = _sigmoid(acc)
    o_ref[...] = acc.astype(o_ref.dtype)


def _proj(h, w, *, row0, n_cols, act, out_dtype, tm, tn):
    T, D = h.shape
    nj = n_cols // tn
    return pl.pallas_call(
        functools.partial(_proj_kernel, act=act),
        grid=(T // tm, nj),
        in_specs=[pl.BlockSpec((tm, D), lambda i, j: (i, 0)),
                  _first_pass_weights(nj, row0, (tn, D), buffers=1)],
        out_specs=pl.BlockSpec((tm, tn), lambda i, j: (i, j)),
        out_shape=jax.ShapeDtypeStruct((T, n_cols), out_dtype),
        scratch_shapes=[pltpu.VMEM((nj, tn, D), BF16)],
        compiler_params=_params("arbitrary", "arbitrary"),
        name="proj_" + (act or "lin"),
    )(h, w)


def _proj_fox_kernel(h_ref, w_ref, wfl_ref,
                     q_ref, kb_ref, vb_ref, kp_ref, ks_ref, vp_ref, vs_ref, fl_ref, wb_ref,
                     *, n_first):
    i, j = pl.program_id(0), pl.program_id(1)

    @pl.when(i == 0)
    def _():
        wb_ref[j] = w_ref[...].astype(BF16)

    @pl.when(j == 0)
    def _():
        fl_ref[...] = lax.dot_general(wfl_ref[...], h_ref[...], NT_DIMS,
                                      preferred_element_type=F32)

    h = h_ref[...]
    tm = h.shape[0]
    pair = 2 * HEAD_DIM

    def column_pairs():
        for c in range(WIDTH // pair):
            cs = slice(c * pair, (c + 1) * pair)
            yield c, cs, lax.dot_general(h, wb_ref[j, cs, :], NT_DIMS,
                                         preferred_element_type=F32)

    @pl.when(j == 0)
    def _():
        for _, cs, acc in column_pairs():
            q_ref[:, cs] = (acc * QK_SCALE).astype(BF16)

    def key_or_value(bf_ref, heads_ref):
        for c, cs, acc in column_pairs():
            bf_ref[:, cs] = acc.astype(BF16)
            for hh in range(2):
                heads_ref[_head_rows(2 * c + hh, tm), :] = acc[:, _head(hh)]

    for jj, bf_ref, first_ref, second_ref in ((1, kb_ref, kp_ref, ks_ref),
                                              (2, vb_ref, vp_ref, vs_ref)):
        pl.when((j == jj) & (i < n_first))(
            functools.partial(key_or_value, bf_ref, first_ref))
        pl.when((j == jj) & (i >= n_first))(
            functools.partial(key_or_value, bf_ref, second_ref))


def _proj_fox(h, w, w_fl_t, *, col0, rows_first, tm):
    T, D = h.shape
    Tp, Ts = rows_first, T - rows_first
    n_first = Tp // tm
    kv_p, kv_s = _two_streams(tm * N_HEADS, HEAD_DIM, n_first)
    kv_shape = lambda rows: jax.ShapeDtypeStruct((rows * N_HEADS, HEAD_DIM), F32)
    tok = pl.BlockSpec((tm, WIDTH), lambda i, j: (i, 0))
    tok_shape = jax.ShapeDtypeStruct((T, WIDTH), BF16)
    return pl.pallas_call(
        functools.partial(_proj_fox_kernel, n_first=n_first),
        grid=(T // tm, 3),
        in_specs=[pl.BlockSpec((tm, D), lambda i, j: (i, 0)),
                  _first_pass_weights(3, col0, (WIDTH, D), buffers=1),
                  pl.BlockSpec((N_HEADS, D), lambda i, j: (0, 0))],
        out_specs=[tok, tok, tok, kv_p, kv_s, kv_p, kv_s,
                   pl.BlockSpec((N_HEADS, tm), lambda i, j: (0, i))],
        out_shape=[tok_shape, tok_shape, tok_shape,
                   kv_shape(Tp), kv_shape(Ts), kv_shape(Tp), kv_shape(Ts),
                   jax.ShapeDtypeStruct((N_HEADS, T), F32)],
        scratch_shapes=[pltpu.VMEM((3, WIDTH, D), BF16)],
        compiler_params=_params("arbitrary", "arbitrary"),
        name="proj_fox",
    )(h, w, w_fl_t)


def _lane_cumsum(x, seg):
    pos = lax.broadcasted_iota(jnp.int32, x.shape, x.ndim - 1) & (seg - 1)
    shift = 1
    while shift < seg:
        x = x + jnp.where(pos >= shift, pltpu.roll(x, shift, x.ndim - 1), 0.0)
        shift *= 2
    return x


def _gate_kernel(fl_ref, bias_ref, lf_ref, c_ref, *, seg):
    z = fl_ref[...] + bias_ref[...]
    lf = jnp.minimum(z, 0.0) - jnp.log(1.0 + jnp.exp(-jnp.abs(z)))
    lf_ref[...] = lf
    c = _lane_cumsum(lf, seg) * LOG2E
    c_ref[...] = jnp.zeros(c_ref.shape, F32)
    c_ref[0:c.shape[0], :] = c


def _fox_gates(fl_t, bias, *, col0, cols, seg, block, pad_rows):
    H = fl_t.shape[0]
    nb = cols // block
    base = col0 // block
    return pl.pallas_call(
        functools.partial(_gate_kernel, seg=seg),
        grid=(nb,),
        in_specs=[pl.BlockSpec((H, block), lambda i: (0, base + i)),
                  pl.BlockSpec((H, 1), lambda i: (0, 0))],
        out_specs=[pl.BlockSpec((H, block), lambda i: (0, i)),
                   pl.BlockSpec((None, pad_rows, block), lambda i: (i, 0, 0))],
        out_shape=[jax.ShapeDtypeStruct((H, cols), F32),
                   jax.ShapeDtypeStruct((nb, pad_rows, block), F32)],
        compiler_params=_params("parallel"),
        name="fox_gates",
    )(fl_t, bias)


def _suffix_kernel(lf_ref, o_ref):
    lf = lf_ref[...]
    c = _lane_cumsum(lf, lf.shape[-1])
    o_ref[...] = (c[:, -1:] - c) * LOG2E


def _cache_suffix(clf_t):
    B, H, P = clf_t.shape
    rows = _tile(B * H, 32)
    spec = pl.BlockSpec((rows, P), lambda r: (r, 0))
    return pl.pallas_call(
        _suffix_kernel,
        grid=(B * H // rows,),
        in_specs=[spec],
        out_specs=spec,
        out_shape=jax.ShapeDtypeStruct((B * H, P), F32),
        compiler_params=_params("parallel"),
        name="fox_cache_suffix",
    )(clf_t.reshape(B * H, P)).reshape(B, H, P)


SUB = 16


def _gla_kernel(aq_ref, af_ref, ai_ref, ag_ref, lbl_ref, gn_ref, s0_ref, tri_ref,
                o_ref, s_out_ref, st_ref, *, chunk, n_chunks):
    t = pl.program_id(1)

    @pl.when(t == 0)
    def _():
        for h in range(N_HEADS):
            st_ref[h] = s0_ref[h].T

    lbl = lbl_ref[...]
    e = jnp.exp(lbl - jnp.max(lbl, axis=0, keepdims=True))
    lb = e[0:1] / jnp.sum(e, axis=0, keepdims=True)
    gn = gn_ref[...]
    tri = tri_ref[...]
    n_sub = chunk // SUB

    def one_chunk(ci):
        r = slice(ci * chunk, (ci + 1) * chunk)
        fa = lb + (1.0 - lb) * _sigmoid(af_ref[r, :])
        lf = jnp.log(fa)
        ka = 1.0 - fa
        aq = aq_ref[r, :]
        qa = aq * _sigmoid(aq)
        v = ai_ref[r, :].astype(BF16)
        ag = ag_ref[r, :]
        gate = ag * _sigmoid(ag)

        hi = lf.astype(BF16)
        r1 = lf - hi.astype(F32)
        mid = r1.astype(BF16)
        lo = (r1 - mid.astype(F32)).astype(BF16)
        b = (jnp.dot(tri, hi, preferred_element_type=F32)
             + jnp.dot(tri, mid, preferred_element_type=F32)
             + jnp.dot(tri, lo, preferred_element_type=F32))
        b_end = b[chunk - 1:chunk, :]

        q_in = (qa * jnp.exp(b)).astype(BF16)
        k_out = (ka * jnp.exp(b_end - b)).astype(BF16)
        decay = jnp.exp(b_end)
        qp, kp = [], []
        for i in range(n_sub):
            r0, n = i * SUB, (i + 1) * SUB
            m_i = b[r0 + SUB // 2:r0 + SUB // 2 + 1, :]
            qp.append((qa[r0:n] * jnp.exp(b[r0:n] - m_i)).astype(BF16))
            kp.append((ka[:n] * jnp.exp(m_i - b[:n])).astype(BF16))

        st = [st_ref[h] for h in range(N_HEADS)]
        inter = [lax.dot_general(q_in[:, _head(h)], st[h].astype(BF16), NT_DIMS,
                                 preferred_element_type=F32) for h in range(N_HEADS)]
        att = [[lax.dot_general(qp[i][:, _head(h)], kp[i][:, _head(h)], NT_DIMS,
                                preferred_element_type=F32) for h in range(N_HEADS)]
               for i in range(n_sub)]
        for h in range(N_HEADS):
            st_ref[h] = st[h] * decay[:, _head(h)] + lax.dot_general(
                v[:, _head(h)], k_out[:, _head(h)], TN_DIMS, preferred_element_type=F32)
        for i in range(n_sub):
            r0, n = i * SUB, (i + 1) * SUB
            causal = (lax.broadcasted_iota(jnp.int32, (SUB, n), 1)
                      <= lax.broadcasted_iota(jnp.int32, (SUB, n), 0) + r0)
            att[i] = [jnp.where(causal, a, 0.0).astype(BF16) for a in att[i]]
        for h in range(N_HEADS):
            o = jnp.concatenate(
                [inter[h][i * SUB:(i + 1) * SUB]
                 + jnp.dot(att[i][h], v[:(i + 1) * SUB, _head(h)], preferred_element_type=F32)
                 for i in range(n_sub)], axis=0)
            ms = jnp.mean(o * o, axis=-1, keepdims=True)
            o_ref[r, _head(h)] = (o * lax.rsqrt(ms + EPS) * gn * gate[:, _head(h)]).astype(o_ref.dtype)

    for ci in range(n_chunks):
        one_chunk(ci)

    @pl.when(t == pl.num_programs(1) - 1)
    def _():
        for h in range(N_HEADS):
            s_out_ref[h] = st_ref[h].T


def _hgrn2(a_proj, lb_logits, gnorm, s0, *, batch, length, row0, chunk, n_chunks):
    step = chunk * n_chunks
    nt = length // step
    base = row0 // step

    def col(g):
        return pl.BlockSpec((step, WIDTH), lambda b, t: (base + b * nt + t, g))

    st_spec = pl.BlockSpec((None, N_HEADS, HEAD_DIM, HEAD_DIM), lambda b, t: (b, 0, 0, 0))
    tri = jnp.tril(jnp.ones((chunk, chunk), BF16))
    return pl.pallas_call(
        functools.partial(_gla_kernel, chunk=chunk, n_chunks=n_chunks),
        grid=(batch, nt),
        in_specs=[col(0), col(1), col(2), col(3),
                  pl.BlockSpec(lb_logits.shape, lambda b, t: (0, 0)),
                  pl.BlockSpec((1, HEAD_DIM), lambda b, t: (0, 0)),
                  st_spec,
                  pl.BlockSpec((chunk, chunk), lambda b, t: (0, 0))],
        out_specs=[pl.BlockSpec((step, WIDTH), lambda b, t: (b * nt + t, 0)), st_spec],
        out_shape=[jax.ShapeDtypeStruct((batch * length, WIDTH), BF16),
                   jax.ShapeDtypeStruct((batch, N_HEADS, HEAD_DIM, HEAD_DIM), F32)],
        scratch_shapes=[pltpu.VMEM((N_HEADS, HEAD_DIM, HEAD_DIM), F32)],
        compiler_params=_params("parallel", "arbitrary"),
        name="hgrn2",
    )(a_proj, a_proj, a_proj, a_proj, lb_logits, gnorm, s0, tri)


def _attn_prompt_kernel(q_ref, k_ref, v_ref, c_ref, o_ref,
                        vt_ref, cb_ref, s_ref, m_ref, l_ref, acc_ref, *, blk):
    h = pl.program_id(1)
    length = k_ref.shape[0]
    halves = (slice(0, blk // 2), slice(blk // 2, blk))

    for r0 in range(0, length, blk):
        rs = slice(r0, r0 + blk)
        vt_ref[:, rs] = v_ref[rs, :].astype(F32).T.astype(BF16)
        c_cols = c_ref[:, rs].T
        onehot = lax.broadcasted_iota(jnp.int32, c_cols.shape, 1) == h
        cb_ref[rs, :] = jnp.broadcast_to(
            jnp.sum(jnp.where(onehot, c_cols, 0.0), axis=1, keepdims=True), (blk, LANES))

    def scores(iq, j, buf):
        ks = slice(j * blk, (j + 1) * blk)
        kb = k_ref[ks, :]
        cb = cb_ref[ks, :]
        for hv in halves:
            x = lax.dot_general(kb, q_ref[iq * blk + hv.start:iq * blk + hv.stop, :], NT_DIMS,
                                preferred_element_type=F32)
            s_ref[buf, :, hv] = x - jnp.concatenate([cb] * (x.shape[1] // LANES), axis=1)

    def consume(iq, j, buf):
        st = iq % 2
        masked = j == iq
        vt = vt_ref[:, j * blk:(j + 1) * blk]
        p = []
        for hv in halves:
            keys = hv.stop if masked else blk
            x = s_ref[buf, :keys, hv]
            if masked:
                key = lax.broadcasted_iota(jnp.int32, x.shape, 0)
                qry = lax.broadcasted_iota(jnp.int32, x.shape, 1) + hv.start
                x = jnp.where(key <= qry, x, -jnp.inf)
            if j == 0:
                m_new = jnp.max(x, axis=0, keepdims=True)
                e = jnp.exp2(x - m_new)
                l_ref[st, :, hv] = jnp.sum(e, axis=0, keepdims=True)
            else:
                m_prev = m_ref[st, :, hv]
                m_new = jnp.maximum(m_prev, jnp.max(x, axis=0, keepdims=True))
                alpha = jnp.exp2(m_prev - m_new)
                e = jnp.exp2(x - m_new)
                l_ref[st, :, hv] = alpha * l_ref[st, :, hv] + jnp.sum(e, axis=0, keepdims=True)
                acc_ref[st, :, hv] = alpha * acc_ref[st, :, hv]
            m_ref[st, :, hv] = m_new
            p.append(e.astype(BF16))
        for e, hv in zip(p, halves):
            pv = jnp.dot(vt[:, :e.shape[0]], e, preferred_element_type=F32)
            if j == 0:
                acc_ref[st, :, hv] = pv
            else:
                acc_ref[st, :, hv] += pv

    visits = [(iq, j) for iq in range(length // blk) for j in range(iq + 1)]
    scores(*visits[0], 0)
    for n, (iq, j) in enumerate(visits):
        if n + 1 < len(visits):
            scores(*visits[n + 1], (n + 1) % 2)
        consume(iq, j, n % 2)
        if j == iq:
            st = iq % 2
            o_ref[iq * blk:(iq + 1) * blk, :] = (acc_ref[st] / l_ref[st]).T.astype(o_ref.dtype)


def _fox_prompt(q, k, v, c, *, batch, length, blk):
    spec = pl.BlockSpec((length, HEAD_DIM), lambda b, h: (b, h))
    return pl.pallas_call(
        functools.partial(_attn_prompt_kernel, blk=blk),
        grid=(batch, N_HEADS),
        in_specs=[spec, spec, spec, pl.BlockSpec((None,) + c.shape[1:], lambda b, h: (b, 0, 0))],
        out_specs=spec,
        out_shape=jax.ShapeDtypeStruct((batch * length, WIDTH), BF16),
        scratch_shapes=[pltpu.VMEM((HEAD_DIM, length), BF16),
                        pltpu.VMEM((length, LANES), F32),
                        pltpu.VMEM((2, blk, blk), F32),
                        pltpu.VMEM((2, 1, blk), F32), pltpu.VMEM((2, 1, blk), F32),
                        pltpu.VMEM((2, HEAD_DIM, blk), F32)],
        compiler_params=_params("parallel", "parallel"),
        name="fox_prompt",
    )(q, k, v, c)


def _attend_heads(score_fn, value_fn, m_ref, l_ref, acc_ref):
    s = [score_fn(h) for h in range(N_HEADS)]
    stats = []
    for h in range(N_HEADS):
        m_prev = m_ref[h]
        m_new = jnp.maximum(m_prev, jnp.max(s[h], axis=-1, keepdims=True))
        stats.append((jnp.exp2(m_prev - m_new), m_new))
    p = [jnp.exp2(s[h] - stats[h][1]) for h in range(N_HEADS)]
    for h in range(N_HEADS):
        alpha, m_new = stats[h]
        l_ref[h] = alpha * l_ref[h] + jnp.sum(p[h], axis=-1, keepdims=True)
        m_ref[h] = m_new
        acc_ref[:, _head(h)] = alpha * acc_ref[:, _head(h)] + jnp.dot(
            p[h].astype(BF16), value_fn(h), preferred_element_type=F32)


def _attn_sample_kernel(q_ref, kc_hbm, vc_hbm, bc_ref, kn_ref, vn_ref, bn_ref, o_ref,
                        kbuf_ref, vbuf_ref, sem_ref, m_ref, l_ref, acc_ref):
    b, j = pl.program_id(0), pl.program_id(1)
    nj = pl.num_programs(1)
    tk = bc_ref.shape[1]
    step = b * nj + j
    slot = step % 2

    def cache_copies(stream, blk, into):
        return [pltpu.make_async_copy(hbm.at[stream, pl.ds(blk * tk, tk), h, :],
                                      buf.at[into, h], sem_ref.at[t, into])
                for t, (hbm, buf) in enumerate(((kc_hbm, kbuf_ref), (vc_hbm, vbuf_ref)))
                for h in range(N_HEADS)]

    def start_all(copies):
        for cp in copies:
            cp.start()

    @pl.when(step == 0)
    def _():
        start_all(cache_copies(b, j, slot))

    @pl.when(step + 1 < pl.num_programs(0) * nj)
    def _():
        start_all(cache_copies((step + 1) // nj, (step + 1) % nj, 1 - slot))

    @pl.when(j == 0)
    def _():
        m_ref[...] = jnp.full(m_ref.shape, -jnp.inf, F32)
        l_ref[...] = jnp.zeros(l_ref.shape, F32)
        acc_ref[...] = jnp.zeros(acc_ref.shape, F32)

    for cp in cache_copies(b, j, slot):
        cp.wait()
    _attend_heads(
        lambda h: lax.dot_general(q_ref[:, _head(h)], kbuf_ref[slot, h].astype(BF16),
                                  NT_DIMS, preferred_element_type=F32) + bc_ref[h:h + 1, :],
        lambda h: vbuf_ref[slot, h].astype(BF16), m_ref, l_ref, acc_ref)

    @pl.when(j == pl.num_programs(1) - 1)
    def _():
        tq = q_ref.shape[0]
        causal = (lax.broadcasted_iota(jnp.int32, (tq, tq), 1)
                  <= lax.broadcasted_iota(jnp.int32, (tq, tq), 0))
        _attend_heads(
            lambda h: jnp.where(causal,
                                lax.dot_general(q_ref[:, _head(h)], kn_ref[:, _head(h)], NT_DIMS,
                                                preferred_element_type=F32) - bn_ref[h:h + 1, :],
                                -jnp.inf),
            lambda h: vn_ref[:, _head(h)], m_ref, l_ref, acc_ref)
        for h in range(N_HEADS):
            o_ref[:, _head(h)] = (acc_ref[:, _head(h)] / l_ref[h]).astype(o_ref.dtype)


def _fox_sample(q, k_new, v_new, c_new, cache_k, cache_v, cache_bias, *, row0, tk):
    B, H, P = cache_bias.shape
    Ls = c_new.shape[2]
    base = row0 // Ls
    new = pl.BlockSpec((Ls, WIDTH), lambda b, j: (base + b, 0))
    cache = pl.BlockSpec(memory_space=pl.ANY)
    cache_buf = pltpu.VMEM((2, H, tk, HEAD_DIM), F32)
    return pl.pallas_call(
        _attn_sample_kernel,
        grid=(B, P // tk),
        in_specs=[new, cache, cache,
                  pl.BlockSpec((None, N_HEADS, tk), lambda b, j: (b, 0, j)),
                  new, new,
                  pl.BlockSpec((None, N_HEADS, Ls), lambda b, j: (b, 0, 0))],
        out_specs=pl.BlockSpec((Ls, WIDTH), lambda b, j: (b, 0)),
        out_shape=jax.ShapeDtypeStruct((B * Ls, WIDTH), BF16),
        scratch_shapes=[cache_buf, cache_buf, pltpu.SemaphoreType.DMA((2, 2)),
                        pltpu.VMEM((N_HEADS, Ls, 1), F32), pltpu.VMEM((N_HEADS, Ls, 1), F32),
                        pltpu.VMEM((Ls, WIDTH), F32)],
        compiler_params=_params("arbitrary", "arbitrary"),
        name="fox_sample",
    )(q, cache_k, cache_v, cache_bias, k_new, v_new, c_new)


MERGE_COLS = 512


def _merge_kernel(xp_ref, xs_ref, oap_ref, oas_ref, obp_ref, obs_ref, g_ref,
                  wpa_ref, wpb_ref, wo_ref, g2_ref, o_ref, h2_ref, mg_ref, *, n_first):
    first = pl.program_id(0) < n_first
    D = xp_ref.shape[1]
    oa = jnp.where(first, oap_ref[...], oas_ref[...])
    ob = jnp.where(first, obp_ref[...], obs_ref[...])
    cols = min(MERGE_COLS, D)
    for n0 in range(0, D, cols):
        ns = slice(n0, n0 + cols)
        pa = jnp.dot(oa, wpa_ref[:, ns], preferred_element_type=F32)
        pb = jnp.dot(ob, wpb_ref[:, ns], preferred_element_type=F32)
        ga = g_ref[:, ns].astype(F32)
        gb = g_ref[:, D + n0:D + n0 + cols].astype(F32)
        mg_ref[:, ns] = (ga * pa + gb * pb).astype(BF16)
    x = jnp.where(first, xp_ref[...], xs_ref[...])
    x = x + jnp.dot(mg_ref[...], wo_ref[...], preferred_element_type=F32)
    o_ref[...] = x
    h2_ref[...] = _normed(x, g2_ref)


def _merge_out(xp, xs, oa_p, oa_s, ob_p, ob_s, g, w_pa, w_pb, w_o, g2, *, tm):
    D = xp.shape[1]
    T = xp.shape[0] + xs.shape[0]
    n_first = xp.shape[0] // tm
    fixed = lambda i: (0, 0)
    return pl.pallas_call(
        functools.partial(_merge_kernel, n_first=n_first),
        grid=(T // tm,),
        in_specs=[*_two_streams(tm, D, n_first), *_two_streams(tm, WIDTH, n_first),
                  *_two_streams(tm, WIDTH, n_first),
                  pl.BlockSpec((tm, 2 * D), lambda i: (i, 0)),
                  *[pl.BlockSpec(w.shape, fixed, pipeline_mode=pl.Buffered(1))
                    for w in (w_pa, w_pb, w_o)],
                  pl.BlockSpec((1, D), fixed)],
        out_specs=[pl.BlockSpec((tm, D), lambda i: (i, 0))] * 2,
        out_shape=[jax.ShapeDtypeStruct((T, D), F32), jax.ShapeDtypeStruct((T, D), BF16)],
        scratch_shapes=[pltpu.VMEM((tm, D), BF16)],
        compiler_params=_params("parallel"),
        name="merge_out",
    )(xp, xs, oa_p, oa_s, ob_p, ob_s, g, w_pa, w_pb, w_o, g2)


def _ffn_kernel(x_ref, h_ref, w1_ref, w2_ref, gf_ref, yp_ref, ys_ref, acc_ref, *, n_first):
    i, f = pl.program_id(0), pl.program_id(1)

    @pl.when(f == 0)
    def _():
        acc_ref[...] = x_ref[...]

    u = jnp.maximum(jnp.dot(h_ref[...], w1_ref[...], preferred_element_type=F32), 0.0)
    acc_ref[...] += jnp.dot((u * u).astype(BF16), w2_ref[...], preferred_element_type=F32)

    @pl.when(f == pl.num_programs(1) - 1)
    def _():
        x = acc_ref[...]
        ms = jnp.mean(x * x, axis=-1, keepdims=True)
        y = x * lax.rsqrt(ms + EPS) * gf_ref[...]

        @pl.when(i < n_first)
        def _():
            yp_ref[...] = y

        @pl.when(i >= n_first)
        def _():
            ys_ref[...] = y


def _ffn(x, h, w1, w2, gf, *, rows_first, tm, tf):
    T, D = x.shape
    F = w1.shape[1]
    n_first = rows_first // tm
    rows = pl.BlockSpec((tm, D), lambda i, f: (i, 0))
    return pl.pallas_call(
        functools.partial(_ffn_kernel, n_first=n_first),
        grid=(T // tm, F // tf),
        in_specs=[rows, rows,
                  pl.BlockSpec((D, tf), lambda i, f: (0, f)),
                  pl.BlockSpec((tf, D), lambda i, f: (f, 0)),
                  pl.BlockSpec((1, D), lambda i, f: (0, 0))],
        out_specs=list(_two_streams(tm, D, n_first)),
        out_shape=[jax.ShapeDtypeStruct((rows_first, D), F32),
                   jax.ShapeDtypeStruct((T - rows_first, D), F32)],
        scratch_shapes=[pltpu.VMEM((tm, D), F32)],
        compiler_params=_params("arbitrary", "arbitrary"),
        name="ffn",
    )(x, h, w1, w2, gf)


def _tile(n, pref):
    t = min(pref, n)
    while n % t:
        t //= 2
    return t


def kernel(x_prompt, x_sample, cache_fox_k, cache_fox_v, cache_fox_logf, state_hgrn, norm1, w_in,
           b_fox_f, lb_logits, gnorm_a, w_pa, w_pb, w_o, norm2, w1, w2, norm_f):
    B, L, D = x_prompt.shape
    Bs, Ls, _ = x_sample.shape
    depth, _, P, H, DH = cache_fox_k.shape
    assert depth == 1 and H == N_HEADS and DH == HEAD_DIM
    Tp, Ts = B * L, Bs * Ls
    T = Tp + Ts
    W = WIDTH

    xp, xs = x_prompt.reshape(Tp, D), x_sample.reshape(Ts, D)
    w_in_t = jnp.swapaxes(w_in[0], 0, 1)
    w_fl_t = w_in_t[7 * W:7 * W + H].astype(BF16)
    g1 = norm1[0].reshape(1, D)

    tm = _tile(math.gcd(Tp, Ts), 512)
    tm1 = _tile(math.gcd(Tp, Ts), 1024)
    h1 = _rmsnorm_rows(xp, xs, g1, tm=tm)
    a_proj = _proj(h1, w_in_t, row0=0, n_cols=4 * W, act=None, out_dtype=F32, tm=tm1, tn=W)
    gates = _proj(h1, w_in_t, row0=7 * W + H, n_cols=2 * D, act="sigmoid", out_dtype=BF16, tm=tm1,
                  tn=_tile(2 * D, W))
    q_b, k_b, v_b, k_p, k_s, v_p, v_s, fl_t = _proj_fox(h1, w_in_t, w_fl_t, col0=4 * W,
                                                        rows_first=Tp, tm=_tile(tm, 512))

    bias = b_fox_f[0].reshape(H, 1)
    lf_p, c_p = _fox_gates(fl_t, bias, col0=0, cols=Tp, seg=L, block=L, pad_rows=LANES)
    lf_s, c_s = _fox_gates(fl_t, bias, col0=Tp, cols=Ts, seg=Ls, block=_tile(Ts, 1024), pad_rows=H)
    cache_bias = _cache_suffix(jnp.transpose(cache_fox_logf[0], (0, 2, 1)))

    zeros_state = jnp.zeros((B, H, DH, DH), F32)
    chunk_p = _tile(L, 64)
    oa_p, s_p = _hgrn2(a_proj, lb_logits, gnorm_a, zeros_state, batch=B, length=L, row0=0,
                       chunk=chunk_p, n_chunks=_tile(L // chunk_p, 8))
    oa_s, s_s = _hgrn2(a_proj, lb_logits, gnorm_a, state_hgrn[0], batch=Bs, length=Ls, row0=Tp,
                       chunk=_tile(Ls, 64), n_chunks=1)

    ob_p = _fox_prompt(q_b, k_b, v_b, c_p, batch=B, length=L, blk=_tile(L, 512))
    c_s3 = c_s.transpose(1, 0, 2).reshape(H, Bs, Ls).transpose(1, 0, 2)
    ob_s = _fox_sample(q_b, k_b, v_b, c_s3, cache_fox_k[0], cache_fox_v[0], cache_bias,
                       row0=Tp, tk=_tile(P, 1024))

    tm5 = _tile(tm, 512)
    x1, h2 = _merge_out(xp, xs, oa_p, oa_s, ob_p, ob_s, gates, w_pa[0].astype(BF16),
                        w_pb[0].astype(BF16), w_o[0].astype(BF16), norm2[0].reshape(1, D),
                        tm=_tile(tm, 256))
    y_p, y_s = _ffn(x1, h2, w1[0].astype(BF16), w2[0].astype(BF16), norm_f.reshape(1, D),
                    rows_first=Tp, tm=tm5, tf=_tile(w1.shape[-1], 1024))

    return (y_p.reshape(B, L, D), y_s.reshape(Bs, Ls, D),
            k_p.reshape(1, B, L, H, DH), v_p.reshape(1, B, L, H, DH),
            lf_p.T.reshape(1, B, L, H), s_p[None],
            k_s.reshape(1, Bs, Ls, H, DH), v_s.reshape(1, Bs, Ls, H, DH),
            lf_s.T.reshape(1, Bs, Ls, H), s_s[None])
```

```python
import functools
import math

import jax
import jax.numpy as jnp
from jax import lax
from jax.experimental import pallas as pl
from jax.experimental.pallas import tpu as pltpu

F32 = jnp.float32
BF16 = jnp.bfloat16

EPS = 1e-6
N_HEADS = 8
HEAD_DIM = 128
WIDTH = N_HEADS * HEAD_DIM
LANES = 128
VMEM_LIMIT = 56 * 1024 * 1024
LOG2E = math.log2(math.e)
QK_SCALE = HEAD_DIM ** -0.5 * LOG2E

NT_DIMS = (((1,), (1,)), ((), ()))
TN_DIMS = (((0,), (0,)), ((), ()))


def _params(*sem):
    return pltpu.CompilerParams(dimension_semantics=sem, vmem_limit_bytes=VMEM_LIMIT)


def _sigmoid(x):
    return 1.0 / (1.0 + jnp.exp(-x))


def _head(h):
    return slice(h * HEAD_DIM, (h + 1) * HEAD_DIM)


def _two_streams(rows, width, n_first, n_second=None):
    mode = {"pipeline_mode": pl.Buffered(1)} if n_second == 1 else {}
    return (pl.BlockSpec((rows, width), lambda i, *_: (jnp.minimum(i, n_first - 1), 0)),
            pl.BlockSpec((rows, width), lambda i, *_: (jnp.maximum(i - n_first, 0), 0), **mode))


def _head_rows(h, rows):
    return pl.ds(h, rows, stride=N_HEADS)


def _normed(x, g_ref):
    ms = jnp.mean(x * x, axis=-1, keepdims=True)
    return (x * lax.rsqrt(ms + EPS) * g_ref[...]).astype(BF16)


def _rmsnorm_kernel(xp_ref, xs_ref, g_ref, o_ref, *, n_first):
    @pl.when(pl.program_id(0) < n_first)
    def _():
        o_ref[...] = _normed(xp_ref[...], g_ref)

    @pl.when(pl.program_id(0) >= n_first)
    def _():
        o_ref[...] = _normed(xs_ref[...], g_ref)


def _rmsnorm_rows(xp, xs, g, *, tm):
    D = xp.shape[1]
    n_first = xp.shape[0] // tm
    T = xp.shape[0] + xs.shape[0]
    return pl.pallas_call(
        functools.partial(_rmsnorm_kernel, n_first=n_first),
        grid=(T // tm,),
        in_specs=[*_two_streams(tm, D, n_first), pl.BlockSpec((1, D), lambda i: (0, 0))],
        out_specs=pl.BlockSpec((tm, D), lambda i: (i, 0)),
        out_shape=jax.ShapeDtypeStruct((T, D), BF16),
        compiler_params=_params("parallel"),
        name="rmsnorm_rows",
    )(xp, xs, g)


def _first_pass_weights(n_blocks, row0, shape, buffers=2):
    rows, d = shape
    return pl.BlockSpec(
        (pl.Element(rows), pl.Element(d)),
        lambda i, j: (pl.multiple_of(row0 + jnp.where(i == 0, j, n_blocks - 1) * rows, 8), 0),
        pipeline_mode=pl.Buffered(buffers))


def _proj_kernel(h_ref, w_ref, o_ref, wb_ref, *, act):
    j = pl.program_id(1)

    @pl.when(pl.program_id(0) == 0)
    def _():
        for r0 in range(0, w_ref.shape[0], 256):
            wb_ref[j, :, r0:r0 + 256] = w_ref[r0:r0 + 256, :].T.astype(BF16)

    acc = jnp.dot(h_ref[...], wb_ref[j], preferred_element_type=F32)
    if act == "sigmoid":
        acc = _sigmoid(acc)
    o_ref[...] = acc.astype(o_ref.dtype)


def _proj(h, w, *, row0, n_cols, act, out_dtype, tm, tn):
    T, D = h.shape
    nj = n_cols // tn
    return pl.pallas_call(
        functools.partial(_proj_kernel, act=act),
        grid=(T // tm, nj),
        in_specs=[pl.BlockSpec((tm, D), lambda i, j: (i, 0)),
                  _first_pass_weights(nj, row0, (tn, D), buffers=1)],
        out_specs=pl.BlockSpec((tm, tn), lambda i, j: (i, j)),
        out_shape=jax.ShapeDtypeStruct((T, n_cols), out_dtype),
        scratch_shapes=[pltpu.VMEM((nj, D, tn), BF16)],
        compiler_params=_params("arbitrary", "arbitrary"),
        name="proj_" + (act or "lin"),
    )(h, w)


def _proj_fox_kernel(h_ref, w_ref, wfl_ref,
                     q_ref, kb_ref, vb_ref, kp_ref, ks_ref, vp_ref, vs_ref, fl_ref, wb_ref,
                     *, n_first):
    i, j = pl.program_id(0), pl.program_id(1)

    @pl.when(i == 0)
    def _():
        wb_ref[j] = w_ref[...].astype(BF16)

    @pl.when(j == 0)
    def _():
        fl_ref[...] = lax.dot_general(wfl_ref[...], h_ref[...], NT_DIMS,
                                      preferred_element_type=F32)

    h = h_ref[...]
    tm = h.shape[0]
    pair = 2 * HEAD_DIM

    def column_pairs():
        for c in range(WIDTH // pair):
            cs = slice(c * pair, (c + 1) * pair)
            yield c, cs, lax.dot_general(h, wb_ref[j, cs, :], NT_DIMS,
                                         preferred_element_type=F32)

    @pl.when(j == 0)
    def _():
        for _, cs, acc in column_pairs():
            q_ref[:, cs] = (acc * QK_SCALE).astype(BF16)

    def key_or_value(bf_ref, heads_ref):
        for c, cs, acc in column_pairs():
            bf_ref[:, cs] = acc.astype(BF16)
            for hh in range(2):
                heads_ref[_head_rows(2 * c + hh, tm), :] = acc[:, _head(hh)]

    for jj, bf_ref, first_ref, second_ref in ((1, kb_ref, kp_ref, ks_ref),
                                              (2, vb_ref, vp_ref, vs_ref)):
        pl.when((j == jj) & (i < n_first))(
            functools.partial(key_or_value, bf_ref, first_ref))
        pl.when((j == jj) & (i >= n_first))(
            functools.partial(key_or_value, bf_ref, second_ref))


def _proj_fox(h, w, w_fl_t, *, col0, rows_first, tm):
    T, D = h.shape
    Tp, Ts = rows_first, T - rows_first
    n_first = Tp // tm
    kv_p, kv_s = _two_streams(tm * N_HEADS, HEAD_DIM, n_first)
    kv_shape = lambda rows: jax.ShapeDtypeStruct((rows * N_HEADS, HEAD_DIM), F32)
    tok = pl.BlockSpec((tm, WIDTH), lambda i, j: (i, 0))
    tok_shape = jax.ShapeDtypeStruct((T, WIDTH), BF16)
    return pl.pallas_call(
        functools.partial(_proj_fox_kernel, n_first=n_first),
        grid=(T // tm, 3),
        in_specs=[pl.BlockSpec((tm, D), lambda i, j: (i, 0)),
                  _first_pass_weights(3, col0, (WIDTH, D), buffers=1),
                  pl.BlockSpec((N_HEADS, D), lambda i, j: (0, 0))],
        out_specs=[tok, tok, tok, kv_p, kv_s, kv_p, kv_s,
                   pl.BlockSpec((N_HEADS, tm), lambda i, j: (0, i))],
        out_shape=[tok_shape, tok_shape, tok_shape,
                   kv_shape(Tp), kv_shape(Ts), kv_shape(Tp), kv_shape(Ts),
                   jax.ShapeDtypeStruct((N_HEADS, T), F32)],
        scratch_shapes=[pltpu.VMEM((3, WIDTH, D), BF16)],
        compiler_params=_params("arbitrary", "arbitrary"),
        name="proj_fox",
    )(h, w, w_fl_t)


def _lane_cumsum(x, seg):
    pos = lax.broadcasted_iota(jnp.int32, x.shape, x.ndim - 1) & (seg - 1)
    shift = 1
    while shift < seg:
        x = x + jnp.where(pos >= shift, pltpu.roll(x, shift, x.ndim - 1), 0.0)
        shift *= 2
    return x


def _gate_kernel(fl_ref, bias_ref, lf_ref, c_ref, *, seg):
    z = fl_ref[...] + bias_ref[...]
    lf = jnp.minimum(z, 0.0) - jnp.log(1.0 + jnp.exp(-jnp.abs(z)))
    lf_ref[...] = lf
    c = _lane_cumsum(lf, seg) * LOG2E
    c_ref[...] = jnp.zeros(c_ref.shape, F32)
    c_ref[0:c.shape[0], :] = c


def _fox_gates(fl_t, bias, *, col0, cols, seg, block, pad_rows):
    H = fl_t.shape[0]
    nb = cols // block
    base = col0 // block
    return pl.pallas_call(
        functools.partial(_gate_kernel, seg=seg),
        grid=(nb,),
        in_specs=[pl.BlockSpec((H, block), lambda i: (0, base + i)),
                  pl.BlockSpec((H, 1), lambda i: (0, 0))],
        out_specs=[pl.BlockSpec((H, block), lambda i: (0, i)),
                   pl.BlockSpec((None, pad_rows, block), lambda i: (i, 0, 0))],
        out_shape=[jax.ShapeDtypeStruct((H, cols), F32),
                   jax.ShapeDtypeStruct((nb, pad_rows, block), F32)],
        compiler_params=_params("parallel"),
        name="fox_gates",
    )(fl_t, bias)


def _suffix_kernel(lf_ref, o_ref):
    lf = lf_ref[...]
    c = _lane_cumsum(lf, lf.shape[-1])
    o_ref[...] = (c[:, -1:] - c) * LOG2E


def _cache_suffix(clf_t):
    B, H, P = clf_t.shape
    rows = _tile(B * H, 32)
    spec = pl.BlockSpec((rows, P), lambda r: (r, 0))
    return pl.pallas_call(
        _suffix_kernel,
        grid=(B * H // rows,),
        in_specs=[spec],
        out_specs=spec,
        out_shape=jax.ShapeDtypeStruct((B * H, P), F32),
        compiler_params=_params("parallel"),
        name="fox_cache_suffix",
    )(clf_t.reshape(B * H, P)).reshape(B, H, P)


SUB = 16


def _gla_kernel(aq_ref, af_ref, ai_ref, ag_ref, lbl_ref, gn_ref, s0_ref, tri_ref,
                o_ref, s_out_ref, st_ref, *, chunk, n_chunks):
    t = pl.program_id(1)

    @pl.when(t == 0)
    def _():
        for h in range(N_HEADS):
            st_ref[h] = s0_ref[h].T

    lbl = lbl_ref[...]
    e = jnp.exp(lbl - jnp.max(lbl, axis=0, keepdims=True))
    lb = e[0:1] / jnp.sum(e, axis=0, keepdims=True)
    gn = gn_ref[...]
    tri = tri_ref[...]
    n_sub = chunk // SUB

    def one_chunk(ci):
        r = slice(ci * chunk, (ci + 1) * chunk)
        fa = lb + (1.0 - lb) * _sigmoid(af_ref[r, :])
        lf = jnp.log(fa)
        ka = 1.0 - fa
        aq = aq_ref[r, :]
        qa = aq * _sigmoid(aq)
        v = ai_ref[r, :].astype(BF16)
        ag = ag_ref[r, :]
        gate = ag * _sigmoid(ag)

        hi = lf.astype(BF16)
        r1 = lf - hi.astype(F32)
        mid = r1.astype(BF16)
        lo = (r1 - mid.astype(F32)).astype(BF16)
        b = (jnp.dot(tri, hi, preferred_element_type=F32)
             + jnp.dot(tri, mid, preferred_element_type=F32)
             + jnp.dot(tri, lo, preferred_element_type=F32))
        b_end = b[chunk - 1:chunk, :]

        q_in = (qa * jnp.exp(b)).astype(BF16)
        k_out = (ka * jnp.exp(b_end - b)).astype(BF16)
        decay = jnp.exp(b_end)
        qp, kp = [], []
        for i in range(n_sub):
            r0, n = i * SUB, (i + 1) * SUB
            m_i = b[r0 + SUB // 2:r0 + SUB // 2 + 1, :]
            qp.append((qa[r0:n] * jnp.exp(b[r0:n] - m_i)).astype(BF16))
            kp.append((ka[:n] * jnp.exp(m_i - b[:n])).astype(BF16))

        st = [st_ref[h] for h in range(N_HEADS)]
        inter = [lax.dot_general(q_in[:, _head(h)], st[h].astype(BF16), NT_DIMS,
                                 preferred_element_type=F32) for h in range(N_HEADS)]
        att = [[lax.dot_general(qp[i][:, _head(h)], kp[i][:, _head(h)], NT_DIMS,
                                preferred_element_type=F32) for h in range(N_HEADS)]
               for i in range(n_sub)]
        for h in range(N_HEADS):
            st_ref[h] = st[h] * decay[:, _head(h)] + lax.dot_general(
                v[:, _head(h)], k_out[:, _head(h)], TN_DIMS, preferred_element_type=F32)
        for i in range(n_sub):
            r0, n = i * SUB, (i + 1) * SUB
            causal = (lax.broadcasted_iota(jnp.int32, (SUB, n), 1)
                      <= lax.broadcasted_iota(jnp.int32, (SUB, n), 0) + r0)
            att[i] = [jnp.where(causal, a, 0.0).astype(BF16) for a in att[i]]
        for h in range(N_HEADS):
            o = jnp.concatenate(
                [inter[h][i * SUB:(i + 1) * SUB]
                 + jnp.dot(att[i][h], v[:(i + 1) * SUB, _head(h)], preferred_element_type=F32)
                 for i in range(n_sub)], axis=0)
            ms = jnp.mean(o * o, axis=-1, keepdims=True)
            o_ref[r, _head(h)] = (o * lax.rsqrt(ms + EPS) * gn * gate[:, _head(h)]).astype(o_ref.dtype)

    for ci in range(n_chunks):
        one_chunk(ci)

    @pl.when(t == pl.num_programs(1) - 1)
    def _():
        for h in range(N_HEADS):
            s_out_ref[h] = st_ref[h].T


def _hgrn2(a_proj, lb_logits, gnorm, s0, *, batch, length, row0, chunk, n_chunks):
    step = chunk * n_chunks
    nt = length // step
    base = row0 // step

    def col(g):
        return pl.BlockSpec((step, WIDTH), lambda b, t: (base + b * nt + t, g))

    st_spec = pl.BlockSpec((None, N_HEADS, HEAD_DIM, HEAD_DIM), lambda b, t: (b, 0, 0, 0))
    tri = jnp.tril(jnp.ones((chunk, chunk), BF16))
    return pl.pallas_call(
        functools.partial(_gla_kernel, chunk=chunk, n_chunks=n_chunks),
        grid=(batch, nt),
        in_specs=[col(0), col(1), col(2), col(3),
                  pl.BlockSpec(lb_logits.shape, lambda b, t: (0, 0)),
                  pl.BlockSpec((1, HEAD_DIM), lambda b, t: (0, 0)),
                  st_spec,
                  pl.BlockSpec((chunk, chunk), lambda b, t: (0, 0))],
        out_specs=[pl.BlockSpec((step, WIDTH), lambda b, t: (b * nt + t, 0)), st_spec],
        out_shape=[jax.ShapeDtypeStruct((batch * length, WIDTH), BF16),
                   jax.ShapeDtypeStruct((batch, N_HEADS, HEAD_DIM, HEAD_DIM), F32)],
        scratch_shapes=[pltpu.VMEM((N_HEADS, HEAD_DIM, HEAD_DIM), F32)],
        compiler_params=_params("parallel", "arbitrary"),
        name="hgrn2",
    )(a_proj, a_proj, a_proj, a_proj, lb_logits, gnorm, s0, tri)


def _attn_prompt_kernel(q_ref, k_ref, v_ref, c_ref, o_ref,
                        vt_ref, cb_ref, s_ref, m_ref, l_ref, acc_ref, *, blk):
    h = pl.program_id(1)
    length = k_ref.shape[0]
    halves = (slice(0, blk // 2), slice(blk // 2, blk))

    for r0 in range(0, length, blk):
        rs = slice(r0, r0 + blk)
        vt_ref[:, rs] = v_ref[rs, :].astype(F32).T.astype(BF16)
        c_cols = c_ref[:, rs].T
        onehot = lax.broadcasted_iota(jnp.int32, c_cols.shape, 1) == h
        cb_ref[rs, :] = jnp.broadcast_to(
            jnp.sum(jnp.where(onehot, c_cols, 0.0), axis=1, keepdims=True), (blk, LANES))

    def scores(iq, j, buf):
        ks = slice(j * blk, (j + 1) * blk)
        kb = k_ref[ks, :]
        cb = cb_ref[ks, :]
        for hv in halves:
            x = lax.dot_general(kb, q_ref[iq * blk + hv.start:iq * blk + hv.stop, :], NT_DIMS,
                                preferred_element_type=F32)
            s_ref[buf, :, hv] = x - jnp.concatenate([cb] * (x.shape[1] // LANES), axis=1)

    def consume(iq, j, buf):
        st = iq % 2
        masked = j == iq
        vt = vt_ref[:, j * blk:(j + 1) * blk]
        p = []
        for hv in halves:
            keys = hv.stop if masked else blk
            x = s_ref[buf, :keys, hv]
            if masked:
                key = lax.broadcasted_iota(jnp.int32, x.shape, 0)
                qry = lax.broadcasted_iota(jnp.int32, x.shape, 1) + hv.start
                x = jnp.where(key <= qry, x, -jnp.inf)
            if j == 0:
                m_new = jnp.max(x, axis=0, keepdims=True)
                e = jnp.exp2(x - m_new)
                l_ref[st, :, hv] = jnp.sum(e, axis=0, keepdims=True)
            else:
                m_prev = m_ref[st, :, hv]
                m_new = jnp.maximum(m_prev, jnp.max(x, axis=0, keepdims=True))
                alpha = jnp.exp2(m_prev - m_new)
                e = jnp.exp2(x - m_new)
                l_ref[st, :, hv] = alpha * l_ref[st, :, hv] + jnp.sum(e, axis=0, keepdims=True)
                acc_ref[st, :, hv] = alpha * acc_ref[st, :, hv]
            m_ref[st, :, hv] = m_new
            p.append(e.astype(BF16))
        for e, hv in zip(p, halves):
            pv = jnp.dot(vt[:, :e.shape[0]], e, preferred_element_type=F32)
            if j == 0:
                acc_ref[st, :, hv] = pv
            else:
                acc_ref[st, :, hv] += pv

    visits = [(iq, j) for iq in range(length // blk) for j in range(iq + 1)]
    scores(*visits[0], 0)
    for n, (iq, j) in enumerate(visits):
        if n + 1 < len(visits):
            scores(*visits[n + 1], (n + 1) % 2)
        consume(iq, j, n % 2)
        if j == iq:
            st = iq % 2
            o_ref[iq * blk:(iq + 1) * blk, :] = (acc_ref[st] / l_ref[st]).T.astype(o_ref.dtype)


def _fox_prompt(q, k, v, c, *, batch, length, blk):
    spec = pl.BlockSpec((length, HEAD_DIM), lambda b, h: (b, h))
    return pl.pallas_call(
        functools.partial(_attn_prompt_kernel, blk=blk),
        grid=(batch, N_HEADS),
        in_specs=[spec, spec, spec, pl.BlockSpec((None,) + c.shape[1:], lambda b, h: (b, 0, 0))],
        out_specs=spec,
        out_shape=jax.ShapeDtypeStruct((batch * length, WIDTH), BF16),
        scratch_shapes=[pltpu.VMEM((HEAD_DIM, length), BF16),
                        pltpu.VMEM((length, LANES), F32),
                        pltpu.VMEM((2, blk, blk), F32),
                        pltpu.VMEM((2, 1, blk), F32), pltpu.VMEM((2, 1, blk), F32),
                        pltpu.VMEM((2, HEAD_DIM, blk), F32)],
        compiler_params=_params("parallel", "parallel"),
        name="fox_prompt",
    )(q, k, v, c)


def _attend_heads(score_fn, value_fn, m_ref, l_ref, acc_ref):
    s = [score_fn(h) for h in range(N_HEADS)]
    stats = []
    for h in range(N_HEADS):
        m_prev = m_ref[h]
        m_new = jnp.maximum(m_prev, jnp.max(s[h], axis=-1, keepdims=True))
        stats.append((jnp.exp2(m_prev - m_new), m_new))
    p = [jnp.exp2(s[h] - stats[h][1]) for h in range(N_HEADS)]
    for h in range(N_HEADS):
        alpha, m_new = stats[h]
        l_ref[h] = alpha * l_ref[h] + jnp.sum(p[h], axis=-1, keepdims=True)
        m_ref[h] = m_new
        acc_ref[:, _head(h)] = alpha * acc_ref[:, _head(h)] + jnp.dot(
            p[h].astype(BF16), value_fn(h), preferred_element_type=F32)


def _attn_sample_kernel(q_ref, kc_hbm, vc_hbm, bc_ref, kn_ref, vn_ref, bn_ref, o_ref,
                        kbuf_ref, vbuf_ref, sem_ref, m_ref, l_ref, acc_ref):
    b, j = pl.program_id(0), pl.program_id(1)
    nj = pl.num_programs(1)
    tk = bc_ref.shape[1]
    step = b * nj + j
    slot = step % 2

    def cache_copies(stream, blk, into):
        return [pltpu.make_async_copy(hbm.at[stream, pl.ds(blk * tk, tk), h, :],
                                      buf.at[into, h], sem_ref.at[t, into])
                for t, (hbm, buf) in enumerate(((kc_hbm, kbuf_ref), (vc_hbm, vbuf_ref)))
                for h in range(N_HEADS)]

    def start_all(copies):
        for cp in copies:
            cp.start()

    @pl.when(step == 0)
    def _():
        start_all(cache_copies(b, j, slot))

    @pl.when(step + 1 < pl.num_programs(0) * nj)
    def _():
        start_all(cache_copies((step + 1) // nj, (step + 1) % nj, 1 - slot))

    @pl.when(j == 0)
    def _():
        m_ref[...] = jnp.full(m_ref.shape, -jnp.inf, F32)
        l_ref[...] = jnp.zeros(l_ref.shape, F32)
        acc_ref[...] = jnp.zeros(acc_ref.shape, F32)

    for cp in cache_copies(b, j, slot):
        cp.wait()
    _attend_heads(
        lambda h: lax.dot_general(q_ref[:, _head(h)], kbuf_ref[slot, h].astype(BF16),
                                  NT_DIMS, preferred_element_type=F32) + bc_ref[h:h + 1, :],
        lambda h: vbuf_ref[slot, h].astype(BF16), m_ref, l_ref, acc_ref)

    @pl.when(j == pl.num_programs(1) - 1)
    def _():
        tq = q_ref.shape[0]
        causal = (lax.broadcasted_iota(jnp.int32, (tq, tq), 1)
                  <= lax.broadcasted_iota(jnp.int32, (tq, tq), 0))
        _attend_heads(
            lambda h: jnp.where(causal,
                                lax.dot_general(q_ref[:, _head(h)], kn_ref[:, _head(h)], NT_DIMS,
                                                preferred_element_type=F32) - bn_ref[h:h + 1, :],
                                -jnp.inf),
            lambda h: vn_ref[:, _head(h)], m_ref, l_ref, acc_ref)
        for h in range(N_HEADS):
            o_ref[:, _head(h)] = (acc_ref[:, _head(h)] / l_ref[h]).astype(o_ref.dtype)


def _fox_sample(q, k_new, v_new, c_new, cache_k, cache_v, cache_bias, *, row0, tk):
    B, H, P = cache_bias.shape
    Ls = c_new.shape[2]
    base = row0 // Ls
    new = pl.BlockSpec((Ls, WIDTH), lambda b, j: (base + b, 0))
    cache = pl.BlockSpec(memory_space=pl.ANY)
    cache_buf = pltpu.VMEM((2, H, tk, HEAD_DIM), F32)
    return pl.pallas_call(
        _attn_sample_kernel,
        grid=(B, P // tk),
        in_specs=[new, cache, cache,
                  pl.BlockSpec((None, N_HEADS, tk), lambda b, j: (b, 0, j)),
                  new, new,
                  pl.BlockSpec((None, N_HEADS, Ls), lambda b, j: (b, 0, 0))],
        out_specs=pl.BlockSpec((Ls, WIDTH), lambda b, j: (b, 0)),
        out_shape=jax.ShapeDtypeStruct((B * Ls, WIDTH), BF16),
        scratch_shapes=[cache_buf, cache_buf, pltpu.SemaphoreType.DMA((2, 2)),
                        pltpu.VMEM((N_HEADS, Ls, 1), F32), pltpu.VMEM((N_HEADS, Ls, 1), F32),
                        pltpu.VMEM((Ls, WIDTH), F32)],
        compiler_params=_params("arbitrary", "arbitrary"),
        name="fox_sample",
    )(q, cache_k, cache_v, cache_bias, k_new, v_new, c_new)


MERGE_COLS = 512


def _merge_kernel(xp_ref, xs_ref, oap_ref, oas_ref, obp_ref, obs_ref, g_ref,
                  wpa_ref, wpb_ref, wo_ref, g2_ref, o_ref, h2_ref, mg_ref, *, n_first):
    first = pl.program_id(0) < n_first
    D = xp_ref.shape[1]
    oa = jnp.where(first, oap_ref[...], oas_ref[...])
    ob = jnp.where(first, obp_ref[...], obs_ref[...])
    cols = min(MERGE_COLS, D)
    for n0 in range(0, D, cols):
        ns = slice(n0, n0 + cols)
        pa = jnp.dot(oa, wpa_ref[:, ns], preferred_element_type=F32)
        pb = jnp.dot(ob, wpb_ref[:, ns], preferred_element_type=F32)
        ga = g_ref[:, ns].astype(F32)
        gb = g_ref[:, D + n0:D + n0 + cols].astype(F32)
        mg_ref[:, ns] = (ga * pa + gb * pb).astype(BF16)
    x = jnp.where(first, xp_ref[...], xs_ref[...])
    x = x + jnp.dot(mg_ref[...], wo_ref[...], preferred_element_type=F32)
    o_ref[...] = x
    h2_ref[...] = _normed(x, g2_ref)


def _merge_out(xp, xs, oa_p, oa_s, ob_p, ob_s, g, w_pa, w_pb, w_o, g2, *, tm):
    D = xp.shape[1]
    T = xp.shape[0] + xs.shape[0]
    n_first = xp.shape[0] // tm
    fixed = lambda i: (0, 0)
    return pl.pallas_call(
        functools.partial(_merge_kernel, n_first=n_first),
        grid=(T // tm,),
        in_specs=[*_two_streams(tm, D, n_first), *_two_streams(tm, WIDTH, n_first),
                  *_two_streams(tm, WIDTH, n_first),
                  pl.BlockSpec((tm, 2 * D), lambda i: (i, 0)),
                  *[pl.BlockSpec(w.shape, fixed, pipeline_mode=pl.Buffered(1))
                    for w in (w_pa, w_pb, w_o)],
                  pl.BlockSpec((1, D), fixed)],
        out_specs=[pl.BlockSpec((tm, D), lambda i: (i, 0))] * 2,
        out_shape=[jax.ShapeDtypeStruct((T, D), F32), jax.ShapeDtypeStruct((T, D), BF16)],
        scratch_shapes=[pltpu.VMEM((tm, D), BF16)],
        compiler_params=_params("parallel"),
        name="merge_out",
    )(xp, xs, oa_p, oa_s, ob_p, ob_s, g, w_pa, w_pb, w_o, g2)


def _ffn_kernel(x_ref, h_ref, w1_ref, w2_ref, gf_ref, yp_ref, ys_ref, acc_ref, *, n_first):
    i, f = pl.program_id(0), pl.program_id(1)

    @pl.when(f == 0)
    def _():
        acc_ref[...] = x_ref[...]

    u = jnp.maximum(jnp.dot(h_ref[...], w1_ref[...], preferred_element_type=F32), 0.0)
    acc_ref[...] += jnp.dot((u * u).astype(BF16), w2_ref[...], preferred_element_type=F32)

    @pl.when(f == pl.num_programs(1) - 1)
    def _():
        x = acc_ref[...]
        ms = jnp.mean(x * x, axis=-1, keepdims=True)
        y = x * lax.rsqrt(ms + EPS) * gf_ref[...]

        @pl.when(i < n_first)
        def _():
            yp_ref[...] = y

        @pl.when(i >= n_first)
        def _():
            ys_ref[...] = y


def _ffn(x, h, w1, w2, gf, *, rows_first, tm, tf):
    T, D = x.shape
    F = w1.shape[1]
    n_first = rows_first // tm
    rows = pl.BlockSpec((tm, D), lambda i, f: (i, 0))
    return pl.pallas_call(
        functools.partial(_ffn_kernel, n_first=n_first),
        grid=(T // tm, F // tf),
        in_specs=[rows, rows,
                  pl.BlockSpec((D, tf), lambda i, f: (0, f)),
                  pl.BlockSpec((tf, D), lambda i, f: (f, 0)),
                  pl.BlockSpec((1, D), lambda i, f: (0, 0))],
        out_specs=list(_two_streams(tm, D, n_first)),
        out_shape=[jax.ShapeDtypeStruct((rows_first, D), F32),
                   jax.ShapeDtypeStruct((T - rows_first, D), F32)],
        scratch_shapes=[pltpu.VMEM((tm, D), F32)],
        compiler_params=_params("arbitrary", "arbitrary"),
        name="ffn",
    )(x, h, w1, w2, gf)


def _tile(n, pref):
    t = min(pref, n)
    while n % t:
        t //= 2
    return t


def kernel(x_prompt, x_sample, cache_fox_k, cache_fox_v, cache_fox_logf, state_hgrn, norm1, w_in,
           b_fox_f, lb_logits, gnorm_a, w_pa, w_pb, w_o, norm2, w1, w2, norm_f):
    B, L, D = x_prompt.shape
    Bs, Ls, _ = x_sample.shape
    depth, _, P, H, DH = cache_fox_k.shape
    assert depth == 1 and H == N_HEADS and DH == HEAD_DIM
    Tp, Ts = B * L, Bs * Ls
    T = Tp + Ts
    W = WIDTH

    xp, xs = x_prompt.reshape(Tp, D), x_sample.reshape(Ts, D)
    w_in_t = jnp.swapaxes(w_in[0], 0, 1)
    w_fl_t = w_in_t[7 * W:7 * W + H].astype(BF16)
    g1 = norm1[0].reshape(1, D)

    tm = _tile(math.gcd(Tp, Ts), 512)
    tm1 = _tile(math.gcd(Tp, Ts), 1024)
    h1 = _rmsnorm_rows(xp, xs, g1, tm=tm)
    a_proj = _proj(h1, w_in_t, row0=0, n_cols=4 * W, act=None, out_dtype=F32, tm=tm1, tn=W)
    gates = _proj(h1, w_in_t, row0=7 * W + H, n_cols=2 * D, act="sigmoid", out_dtype=BF16, tm=tm1,
                  tn=_tile(2 * D, W))
    q_b, k_b, v_b, k_p, k_s, v_p, v_s, fl_t = _proj_fox(h1, w_in_t, w_fl_t, col0=4 * W,
                                                        rows_first=Tp, tm=_tile(tm, 512))

    bias = b_fox_f[0].reshape(H, 1)
    lf_p, c_p = _fox_gates(fl_t, bias, col0=0, cols=Tp, seg=L, block=L, pad_rows=LANES)
    lf_s, c_s = _fox_gates(fl_t, bias, col0=Tp, cols=Ts, seg=Ls, block=_tile(Ts, 1024), pad_rows=H)
    cache_bias = _cache_suffix(jnp.transpose(cache_fox_logf[0], (0, 2, 1)))

    zeros_state = jnp.zeros((B, H, DH, DH), F32)
    chunk_p = _tile(L, 64)
    oa_p, s_p = _hgrn2(a_proj, lb_logits, gnorm_a, zeros_state, batch=B, length=L, row0=0,
                       chunk=chunk_p, n_chunks=_tile(L // chunk_p, 8))
    oa_s, s_s = _hgrn2(a_proj, lb_logits, gnorm_a, state_hgrn[0], batch=Bs, length=Ls, row0=Tp,
                       chunk=_tile(Ls, 64), n_chunks=1)

    ob_p = _fox_prompt(q_b, k_b, v_b, c_p, batch=B, length=L, blk=_tile(L, 512))
    c_s3 = c_s.transpose(1, 0, 2).reshape(H, Bs, Ls).transpose(1, 0, 2)
    ob_s = _fox_sample(q_b, k_b, v_b, c_s3, cache_fox_k[0], cache_fox_v[0], cache_bias,
                       row0=Tp, tk=_tile(P, 1024))

    tm5 = _tile(tm, 512)
    x1, h2 = _merge_out(xp, xs, oa_p, oa_s, ob_p, ob_s, gates, w_pa[0].astype(BF16),
                        w_pb[0].astype(BF16), w_o[0].astype(BF16), norm2[0].reshape(1, D),
                        tm=_tile(tm, 256))
    y_p, y_s = _ffn(x1, h2, w1[0].astype(BF16), w2[0].astype(BF16), norm_f.reshape(1, D),
                    rows_first=Tp, tm=tm5, tf=_tile(w1.shape[-1], 1024))

    return (y_p.reshape(B, L, D), y_s.reshape(Bs, Ls, D),
            k_p.reshape(1, B, L, H, DH), v_p.reshape(1, B, L, H, DH),
            lf_p.T.reshape(1, B, L, H), s_p[None],
            k_s.reshape(1, Bs, Ls, H, DH), v_s.reshape(1, Bs, Ls, H, DH),
            lf_s.T.reshape(1, Bs, Ls, H), s_s[None])
```
